```python
import math
import jax
import jax.numpy as jnp
from jax import lax
import numpy as np

D_MODEL = 1024
BATCH = 8
SEQ = 2048
DEPTH = 2

CTX_LEN = 256
GRID_W = 64
EPS = 1e-6
N_DIR = 2
N_BRANCH = 3

GLA_HEADS = 4
GLA_DK = 128
GLA_DV = 256
GLA_GATE_RANK = 16
GLA_GATE_NORMALIZER = 16.0
GLA_CHUNK = 64

GDN_HEADS = 8
GDN_DK = 128
GDN_DV = 128
GDN_CONV = 5
GDN_CHUNK = 64
GDN_QKV = 2 * GDN_HEADS * GDN_DK + GDN_HEADS * GDN_DV

MLA_HEADS = 8
MLA_Q_LORA = 384
MLA_KV_LORA = 256
MLA_NOPE = 128
MLA_ROPE = 64
MLA_DV = 128
MLA_SCALE = (MLA_NOPE + MLA_ROPE) ** -0.5
MLA_Q_BLOCK = 128
ROPE_FREQ = MLA_ROPE // 4
ROPE_BASE = 10000.0

BRANCH_W = GLA_HEADS * GLA_DV

IN_NAMES = ('gla_q', 'gla_k', 'gla_v', 'gla_gate_lr', 'gla_z',
            'gdn_qkv', 'gdn_a', 'gdn_b', 'gdn_z',
            'mla_q_down', 'mla_kv_down', 'mla_k_rope', 'mla_z',
            'merge_gate')
IN_SIZES = (GLA_HEADS * GLA_DK, GLA_HEADS * GLA_DK, GLA_HEADS * GLA_DV, N_DIR * GLA_GATE_RANK, BRANCH_W,
            GDN_QKV, N_DIR * GDN_HEADS, N_DIR * GDN_HEADS, BRANCH_W,
            MLA_Q_LORA, MLA_KV_LORA, MLA_ROPE, BRANCH_W,
            N_BRANCH * D_MODEL)
N_IN = sum(IN_SIZES)

kernel_name = 'hybrid_gla_gdn_mla_flow_block'


def rmsnorm(x, w):
    xf = x.astype(jnp.float32)
    y = xf * lax.rsqrt(jnp.mean(xf * xf, axis=-1, keepdims=True) + EPS)
    return (y * w.astype(jnp.float32)).astype(x.dtype)


def l2norm(x):
    xf = x.astype(jnp.float32)
    return (xf * lax.rsqrt(jnp.sum(xf * xf, axis=-1, keepdims=True) + EPS)).astype(x.dtype)


def in_projection(h, w_in):
    parts = jnp.split(h @ w_in, np.cumsum(IN_SIZES)[:-1].tolist(), axis=-1)
    return dict(zip(IN_NAMES, parts))


def axial_rope_tables(rows):
    row = jnp.repeat(jnp.arange(rows, dtype=jnp.float32), GRID_W)
    col = jnp.tile(jnp.arange(GRID_W, dtype=jnp.float32), rows)
    inv_freq = jnp.power(ROPE_BASE, -jnp.arange(ROPE_FREQ, dtype=jnp.float32) / ROPE_FREQ)
    ang = jnp.stack([row[:, None] * inv_freq, col[:, None] * inv_freq], axis=1)
    return jnp.cos(ang), jnp.sin(ang)


def apply_axial_rope(x, cos, sin):
    shp = x.shape
    xr = x.reshape(shp[:-1] + (2, 2, ROPE_FREQ))
    x1, x2 = xr[..., 0, :], xr[..., 1, :]
    cos = cos.astype(x.dtype)
    sin = sin.astype(x.dtype)
    out = jnp.stack([x1 * cos - x2 * sin, x2 * cos + x1 * sin], axis=-2)
    return out.reshape(shp)


def centred_depthwise_conv(x, w):
    half = w.shape[0] // 2
    return lax.conv_general_dilated(x, w[:, None, :].astype(x.dtype), window_strides=(1,),
                                    padding=[(half, half)], dimension_numbers=('NWC', 'WIO', 'NWC'),
                                    feature_group_count=x.shape[-1])


def gla_chunk_scan(q, k, v, log_a, s0):
    B, T, H, _ = q.shape
    DV = v.shape[-1]
    N = T // GLA_CHUNK

    def blocks(a):
        return a.astype(jnp.float32).reshape((B, N, GLA_CHUNK) + a.shape[2:])

    q, k, v, log_a = blocks(q), blocks(k), blocks(v), blocks(log_a)
    b = jnp.cumsum(log_a, axis=2)
    b_last = b[:, :, -1]
    q_dec = q * jnp.exp(b)
    k_inv = k * jnp.exp(-b)
    k_end = k * jnp.exp(b_last[:, :, None] - b)
    incl = jnp.tril(jnp.ones((GLA_CHUNK, GLA_CHUNK), dtype=bool))
    att = jnp.where(incl, jnp.einsum('bnthd,bnshd->bnhts', q_dec, k_inv), 0.0)
    o_intra = jnp.einsum('bnhts,bnshv->bnthv', att, v)
    writes = jnp.einsum('bnshd,bnshv->bnhdv', k_end, v)

    def step(s, inp):
        decay, write = inp
        return s * decay[..., None] + write, s

    s_fin, s_prev = lax.scan(step, s0, (jnp.moveaxis(jnp.exp(b_last), 1, 0), jnp.moveaxis(writes, 1, 0)))
    o_inter = jnp.einsum('bnthd,nbhdv->bnthv', q_dec, s_prev)
    return (o_intra + o_inter).reshape(B, T, H, DV), s_fin


def gdn_chunk_scan(q, k, v, g, beta, s0):
    B, T, H, _ = q.shape
    DV = v.shape[-1]
    C = GDN_CHUNK
    N = T // C

    def blocks(a):
        return a.astype(jnp.float32).reshape((B, N, C) + a.shape[2:])

    q, k, v, g, beta = map(blocks, (q, k, v, g, beta))
    g_cum = jnp.cumsum(g, axis=2)
    g_h = jnp.swapaxes(g_cum, 2, 3)
    incl = jnp.tril(jnp.ones((C, C), dtype=bool))
    strict = jnp.tril(jnp.ones((C, C), dtype=bool), k=-1)
    diff = g_h[..., :, None] - g_h[..., None, :]
    decay = jnp.where(incl, jnp.exp(jnp.where(incl, diff, 0.0)), 0.0)
    k_beta = k * beta[..., None]
    v_beta = v * beta[..., None]
    l_strict = jnp.where(strict, jnp.einsum('bnthd,bnshd->bnhts', k_beta, k) * decay, 0.0)
    eye = jnp.eye(C, dtype=jnp.float32)
    t_inv = lax.linalg.triangular_solve(eye + l_strict, jnp.broadcast_to(eye, l_strict.shape),
                                        left_side=True, lower=True, unit_diagonal=True)
    u = jnp.einsum('bnhts,bnshv->bnthv', t_inv, v_beta)
    w = jnp.einsum('bnhts,bnshd->bnthd', t_inv, k_beta * jnp.exp(g_cum)[..., None])
    att = jnp.where(incl, jnp.einsum('bnthd,bnshd->bnhts', q, k) * decay, 0.0)
    q_dec = q * jnp.exp(g_cum)[..., None]
    k_end = k * jnp.exp(g_cum[:, :, -1:] - g_cum)[..., None]
    chunk_decay = jnp.exp(g_cum[:, :, -1])

    def step(s, inp):
        w_n, u_n, qd_n, att_n, ke_n, d_n = inp
        v_new = u_n - jnp.einsum('bthd,bhdv->bthv', w_n, s)
        o_n = jnp.einsum('bthd,bhdv->bthv', qd_n, s) + jnp.einsum('bhts,bshv->bthv', att_n, v_new)
        s = s * d_n[..., None, None] + jnp.einsum('bshd,bshv->bhdv', ke_n, v_new)
        return s, o_n

    xs = tuple(jnp.moveaxis(a, 1, 0) for a in (w, u, q_dec, att, k_end, chunk_decay))
    s_fin, o = lax.scan(step, s0, xs)
    return jnp.moveaxis(o, 0, 1).reshape(B, T, H, DV), s_fin


def bidirectional(scan_fn, ctx_fwd, ctx_bwd, lat_fwd, lat_bwd, s0):
    flip = lambda a: jnp.flip(a, axis=1)
    o_cf, s_cf = scan_fn(*ctx_fwd, s0)
    o_cb, s_cb = scan_fn(*map(flip, ctx_bwd), s0)
    o_xf, _ = scan_fn(*lat_fwd, s_cf)
    o_xb, _ = scan_fn(*map(flip, lat_bwd), s_cb)
    return o_cf + flip(o_cb), o_xf + flip(o_xb)


def gla_prepare(pt, gate_w2, gate_b):
    B, T, _ = pt['gla_q'].shape
    q = pt['gla_q'].reshape(B, T, GLA_HEADS, GLA_DK) * GLA_DK ** -0.5
    k = pt['gla_k'].reshape(B, T, GLA_HEADS, GLA_DK)
    v = pt['gla_v'].reshape(B, T, GLA_HEADS, GLA_DV)
    lr = pt['gla_gate_lr'].reshape(B, T, N_DIR, GLA_GATE_RANK)
    logit = jnp.einsum('btir,irk->btik', lr, gate_w2) + gate_b
    log_a = (jax.nn.log_sigmoid(logit.astype(jnp.float32)) / GLA_GATE_NORMALIZER).reshape(
        B, T, N_DIR, GLA_HEADS, GLA_DK)
    return (q, k, v, log_a[:, :, 0]), (q, k, v, log_a[:, :, 1])


def gdn_prepare(pt, conv_w, a_log, dt_bias):
    qkv = jax.nn.silu(centred_depthwise_conv(pt['gdn_qkv'], conv_w))
    B, T, _ = qkv.shape
    hk = GDN_HEADS * GDN_DK
    q, k, v = jnp.split(qkv, [hk, 2 * hk], axis=-1)
    q = l2norm(q.reshape(B, T, GDN_HEADS, GDN_DK)) * GDN_DK ** -0.5
    k = l2norm(k.reshape(B, T, GDN_HEADS, GDN_DK))
    v = v.reshape(B, T, GDN_HEADS, GDN_DV)
    a = pt['gdn_a'].reshape(B, T, N_DIR, GDN_HEADS).astype(jnp.float32)
    g = -jnp.exp(a_log.astype(jnp.float32)) * jax.nn.softplus(a + dt_bias.astype(jnp.float32))
    beta = jax.nn.sigmoid(pt['gdn_b'].reshape(B, T, N_DIR, GDN_HEADS).astype(jnp.float32))
    return (q, k, v, g[:, :, 0], beta[:, :, 0]), (q, k, v, g[:, :, 1], beta[:, :, 1])


def mla_prepare(pt, p, cos, sin):
    B, T, _ = pt['mla_q_down'].shape
    q = (rmsnorm(pt['mla_q_down'], p['mla_q_norm_w']) @ p['mla_q_up']).reshape(
        B, T, MLA_HEADS, MLA_NOPE + MLA_ROPE)
    kv = (rmsnorm(pt['mla_kv_down'], p['mla_kv_norm_w']) @ p['mla_kv_up']).reshape(
        B, T, MLA_HEADS, MLA_NOPE + MLA_DV)
    q_nope = rmsnorm(q[..., :MLA_NOPE], p['mla_qn_nope'])
    q_rope = rmsnorm(q[..., MLA_NOPE:], p['mla_qn_rope'])
    k_nope = rmsnorm(kv[..., :MLA_NOPE], p['mla_kn_nope'])
    k_rope = rmsnorm(pt['mla_k_rope'], p['mla_kn_rope'])
    v = kv[..., MLA_NOPE:]
    if cos is not None:
        q_rope = apply_axial_rope(q_rope, cos[:, None], sin[:, None])
        k_rope = apply_axial_rope(k_rope, cos, sin)
    return q_nope, q_rope, k_nope, k_rope, v


def mla_attend(qn, qr, kn, kr, v):
    s = jnp.einsum('bqhd,bkhd->bhqk', qn, kn) + jnp.einsum('bqhr,bkr->bhqk', qr, kr)
    prob = jax.nn.softmax(s.astype(jnp.float32) * MLA_SCALE, axis=-1)
    return jnp.einsum('bhqk,bkhd->bqhd', prob.astype(v.dtype), v)


def mla_attend_blocked(qn, qr, kn, kr, v):
    B, T = qn.shape[:2]
    nb = T // MLA_Q_BLOCK

    def to_blocks(a):
        return jnp.moveaxis(a.reshape((B, nb, MLA_Q_BLOCK) + a.shape[2:]), 1, 0)

    o = lax.map(lambda qs: mla_attend(qs[0], qs[1], kn, kr, v), (to_blocks(qn), to_blocks(qr)))
    return jnp.moveaxis(o, 0, 1).reshape(B, T, MLA_HEADS * MLA_DV)


def gated_head_norm(o, w, z):
    B, T, H, Dh = o.shape
    return rmsnorm(o.astype(z.dtype), w).reshape(B, T, H * Dh) * jax.nn.silu(z)


def merge_branches(ya, yb, yc, gate_logits, w_branch, w_out):
    B, T, _ = ya.shape
    ys = jnp.einsum('btiw,iwd->btid', jnp.stack([ya, yb, yc], axis=2), w_branch)
    gates = jax.nn.sigmoid(gate_logits.reshape(B, T, N_BRANCH, D_MODEL))
    return jnp.einsum('btd,de->bte', jnp.sum(gates * ys, axis=2), w_out)


def hybrid_layer(x, ctx, c, c_ctx, p, cos, sin, update_ctx):
    B, T, _ = x.shape
    Lc = ctx.shape[1]
    mod_x = jax.nn.silu(c) @ p['ada_w'] + p['ada_b']
    mod_c = jax.nn.silu(c_ctx) @ p['ada_w'] + p['ada_b']
    shift_x, scale_x, gate_x = jnp.split(mod_x[:, None, :], 3, axis=-1)
    shift_c, scale_c, gate_c = jnp.split(mod_c, 3)
    px = in_projection(rmsnorm(x, p['norm_w']) * (1 + scale_x) + shift_x, p['w_in'])
    pc = in_projection(rmsnorm(ctx, p['norm_w']) * (1 + scale_c) + shift_c, p['w_in'])

    a_cf, a_cb = gla_prepare(pc, p['gla_gate_w2'], p['gla_gate_b'])
    a_xf, a_xb = gla_prepare(px, p['gla_gate_w2'], p['gla_gate_b'])
    s0_a = jnp.zeros((B, GLA_HEADS, GLA_DK, GLA_DV), jnp.float32)
    oa_c, oa_x = bidirectional(gla_chunk_scan, a_cf, a_cb, a_xf, a_xb, s0_a)

    b_cf, b_cb = gdn_prepare(pc, p['gdn_conv_w'], p['gdn_a_log'], p['gdn_dt_bias'])
    b_xf, b_xb = gdn_prepare(px, p['gdn_conv_w'], p['gdn_a_log'], p['gdn_dt_bias'])
    s0_b = jnp.zeros((B, GDN_HEADS, GDN_DK, GDN_DV), jnp.float32)
    ob_c, ob_x = bidirectional(gdn_chunk_scan, b_cf, b_cb, b_xf, b_xb, s0_b)

    qn_c, qr_c, kn_c, kr_c, v_c = mla_prepare(pc, p, None, None)
    qn_x, qr_x, kn_x, kr_x, v_x = mla_prepare(px, p, cos, sin)
    oc_x = mla_attend_blocked(qn_x, qr_x,
                              jnp.concatenate([kn_c, kn_x], axis=1),
                              jnp.concatenate([kr_c, kr_x], axis=1),
                              jnp.concatenate([v_c, v_x], axis=1))

    x = x + gate_x * merge_branches(
        gated_head_norm(oa_x, p['gla_norm_w'], px['gla_z']),
        gated_head_norm(ob_x, p['gdn_norm_w'], px['gdn_z']),
        oc_x * jax.nn.silu(px['mla_z']),
        px['merge_gate'], p['w_branch'], p['w_out'])

    if update_ctx:
        oc_c = mla_attend(qn_c, qr_c, kn_c, kr_c, v_c).reshape(B, Lc, BRANCH_W)
        ctx = ctx + gate_c * merge_branches(
            gated_head_norm(oa_c, p['gla_norm_w'], pc['gla_z']),
            gated_head_norm(ob_c, p['gdn_norm_w'], pc['gdn_z']),
            oc_c * jax.nn.silu(pc['mla_z']),
            pc['merge_gate'], p['w_branch'], p['w_out'])
    return x, ctx


def setup_inputs(seed: int = 0) -> dict:
    key = jax.random.key(seed)
    ks = jax.random.split(key, 25)
    L, D = DEPTH, D_MODEL

    def normal(k, shape, scale):
        return jax.random.normal(k, shape, jnp.float32) * scale

    def gain(k, shape):
        return 1.0 + 0.02 * jax.random.normal(k, shape, jnp.float32)

    dt = jnp.exp(jax.random.uniform(ks[13], (L, N_DIR, GDN_HEADS), jnp.float32,
                                    math.log(1e-3), math.log(1e-1)))
    return {
        'x': normal(ks[0], (BATCH, SEQ, D), 1.0),
        'c': normal(ks[1], (BATCH, D), 1.0),
        'ctx': normal(ks[2], (BATCH, CTX_LEN, D), 1.0),
        'c_ctx': normal(ks[3], (D,), 1.0),
        'norm_w': gain(ks[4], (L, D)),
        'ada_w': normal(ks[5], (L, D, 3 * D), 0.5 * D ** -0.5),
        'ada_b': normal(ks[6], (L, 3 * D), 0.02),
        'w_in': normal(ks[7], (L, D, N_IN), D ** -0.5),
        'gla_gate_w2': normal(ks[8], (L, N_DIR, GLA_GATE_RANK, GLA_HEADS * GLA_DK), GLA_GATE_RANK ** -0.5),
        'gla_gate_b': normal(ks[9], (L, N_DIR, GLA_HEADS * GLA_DK), 0.1),
        'gla_norm_w': gain(ks[10], (L, GLA_DV)),
        'gdn_conv_w': normal(ks[11], (L, GDN_CONV, GDN_QKV), GDN_CONV ** -0.5),
        'gdn_a_log': jnp.log(jax.random.uniform(ks[12], (L, N_DIR, GDN_HEADS), jnp.float32, 1.0, 16.0)),
        'gdn_dt_bias': dt + jnp.log(-jnp.expm1(-dt)),
        'gdn_norm_w': gain(ks[14], (L, GDN_DV)),
        'mla_q_norm_w': gain(ks[15], (L, MLA_Q_LORA)),
        'mla_q_up': normal(ks[16], (L, MLA_Q_LORA, MLA_HEADS * (MLA_NOPE + MLA_ROPE)), MLA_Q_LORA ** -0.5),
        'mla_kv_norm_w': gain(ks[17], (L, MLA_KV_LORA)),
        'mla_kv_up': normal(ks[18], (L, MLA_KV_LORA, MLA_HEADS * (MLA_NOPE + MLA_DV)), MLA_KV_LORA ** -0.5),
        'mla_qn_nope': gain(ks[19], (L, MLA_NOPE)),
        'mla_qn_rope': gain(ks[20], (L, MLA_ROPE)),
        'mla_kn_nope': gain(ks[21], (L, MLA_NOPE)),
        'mla_kn_rope': gain(ks[22], (L, MLA_ROPE)),
        'w_branch': normal(ks[23], (L, N_BRANCH, BRANCH_W, D), BRANCH_W ** -0.5),
        'w_out': normal(ks[24], (L, D, D), D ** -0.5),
    }


def reference(x, c, ctx, c_ctx, norm_w, ada_w, ada_b, w_in, gla_gate_w2, gla_gate_b, gla_norm_w,
              gdn_conv_w, gdn_a_log, gdn_dt_bias, gdn_norm_w, mla_q_norm_w, mla_q_up, mla_kv_norm_w,
              mla_kv_up, mla_qn_nope, mla_qn_rope, mla_kn_nope, mla_kn_rope, w_branch, w_out):
    ROWS = x.shape[1] // GRID_W
    cos, sin = axial_rope_tables(ROWS)
    for l in range(DEPTH):
        p = dict(norm_w=norm_w[l], ada_w=ada_w[l], ada_b=ada_b[l], w_in=w_in[l],
                 gla_gate_w2=gla_gate_w2[l], gla_gate_b=gla_gate_b[l], gla_norm_w=gla_norm_w[l],
                 gdn_conv_w=gdn_conv_w[l], gdn_a_log=gdn_a_log[l], gdn_dt_bias=gdn_dt_bias[l],
                 gdn_norm_w=gdn_norm_w[l], mla_q_norm_w=mla_q_norm_w[l], mla_q_up=mla_q_up[l],
                 mla_kv_norm_w=mla_kv_norm_w[l], mla_kv_up=mla_kv_up[l], mla_qn_nope=mla_qn_nope[l],
                 mla_qn_rope=mla_qn_rope[l], mla_kn_nope=mla_kn_nope[l], mla_kn_rope=mla_kn_rope[l],
                 w_branch=w_branch[l], w_out=w_out[l])
        x, ctx = hybrid_layer(x, ctx, c, c_ctx, p, cos, sin, update_ctx=(l < DEPTH - 1))
    return x
```

```python
import functools
import math

import numpy as np
import jax
import jax.numpy as jnp
from jax import lax
from jax.experimental import pallas as pl
from jax.experimental.pallas import tpu as pltpu

F32 = jnp.float32
BF16 = jnp.bfloat16

D_MODEL = 1024
EPS = 1e-6
GRID_W = 64
CHUNK = 64

GLA_HEADS, GLA_DK, GLA_DV, GLA_RANK, GLA_NORMALIZER = 4, 128, 256, 16, 16.0
GDN_HEADS, GDN_DK, GDN_DV, GDN_CONV = 8, 128, 128, 5
GDN_HB = 2
MLA_HEADS, MLA_Q_LORA, MLA_KV_LORA, MLA_NOPE, MLA_ROPE, MLA_DV = 8, 384, 256, 128, 64, 128
MLA_SCALE = (MLA_NOPE + MLA_ROPE) ** -0.5
ROPE_FREQ = MLA_ROPE // 4
ROPE_BASE = 10000.0
BRANCH_W = 1024

_IN_SIZES = (512, 512, 1024, 32, 1024, 3072, 16, 16, 1024, 384, 256, 64, 1024, 3072)
_IN_NAMES = ('gla_q', 'gla_k', 'gla_v', 'gla_gate_lr', 'gla_z', 'gdn_qkv', 'gdn_a', 'gdn_b', 'gdn_z',
             'mla_q_down', 'mla_kv_down', 'mla_k_rope', 'mla_z', 'merge_gate')
_IN_OFF = dict(zip(_IN_NAMES, np.concatenate([[0], np.cumsum(_IN_SIZES)[:-1]]).tolist()))
_IN_LEN = dict(zip(_IN_NAMES, _IN_SIZES))

SM_ROPE, SM_LR, SM_A, SM_B = 0, 64, 96, 112
MLA_GROUP_W = MLA_Q_LORA + MLA_KV_LORA + 128
SMALL_BLOCK = (MLA_Q_LORA + MLA_KV_LORA) // 128

VMEM_LIMIT = 48 * 1024 * 1024


def _params(sem):
    return pltpu.CompilerParams(dimension_semantics=sem, vmem_limit_bytes=VMEM_LIMIT)


def _pick(n, cands):
    for c in cands:
        if n % c == 0:
            return c
    raise ValueError(f"no tile for {n} in {cands}")


def _mm(a, b):
    return jnp.dot(a.astype(BF16), b.astype(BF16), preferred_element_type=F32)


def _mm_nt(a, b):
    return lax.dot_general(a.astype(BF16), b.astype(BF16), (((1,), (1,)), ((), ())),
                           preferred_element_type=F32)


def _mm_tn(a, b):
    return lax.dot_general(a.astype(BF16), b.astype(BF16), (((0,), (0,)), ((), ())),
                           preferred_element_type=F32)


def _mm_hp(a, b):
    return jnp.dot(a, b, precision=lax.Precision.HIGHEST, preferred_element_type=F32)


def _split(a):
    hi = a.astype(BF16)
    lo = (a - hi.astype(F32)).astype(BF16)
    return hi, lo


def _mm3(a, b):
    ah, al = _split(a)
    bh, bl = _split(b)
    d = functools.partial(jnp.dot, preferred_element_type=F32)
    return d(ah, bh) + (d(ah, bl) + d(al, bh))


def _split3(a):
    hi = a.astype(BF16)
    r1 = a - hi.astype(F32)
    mid = r1.astype(BF16)
    lo = (r1 - mid.astype(F32)).astype(BF16)
    return hi, mid, lo


def _cum_rows(cum, x):
    cb = cum.astype(BF16)
    d = functools.partial(jnp.dot, preferred_element_type=F32)
    hi, mid, lo = _split3(x)
    return d(cb, hi) + (d(cb, mid) + d(cb, lo))


def _cum_cols(x, cum):
    cb = cum.astype(BF16)
    d = functools.partial(jnp.dot, preferred_element_type=F32)
    hi, mid, lo = _split3(x)
    return d(hi, cb) + (d(mid, cb) + d(lo, cb))


def _softplus(x):
    return jnp.maximum(x, 0.0) + jnp.log1p(jnp.exp(-jnp.abs(x)))


def _log_sigmoid(x):
    return jnp.minimum(x, 0.0) - jnp.log1p(jnp.exp(-jnp.abs(x)))


def _silu(x):
    return x * jax.nn.sigmoid(x)


def _tri_masks(n, reverse):
    r = lax.broadcasted_iota(jnp.int32, (n, n), 0)
    c = lax.broadcasted_iota(jnp.int32, (n, n), 1)
    incl = (c >= r) if reverse else (c <= r)
    strict = (c > r) if reverse else (c < r)
    return r, c, incl, strict


def _ada_kernel(c_ref, w_ref, b_ref, o_ref):
    o_ref[0] = _mm_hp(_silu(c_ref[...]), w_ref[0]) + b_ref[0]


def _ada_call(c_all, ada_w, ada_b):
    nl, d, n3 = ada_w.shape
    r = c_all.shape[0]
    tn = 1024
    return pl.pallas_call(
        _ada_kernel,
        grid=(nl, n3 // tn),
        in_specs=[pl.BlockSpec((r, d), lambda l, j: (0, 0)),
                  pl.BlockSpec((1, d, tn), lambda l, j: (l, 0, j)),
                  pl.BlockSpec((1, 1, tn), lambda l, j: (l, 0, j))],
        out_specs=pl.BlockSpec((1, r, tn), lambda l, j: (l, 0, j)),
        out_shape=jax.ShapeDtypeStruct((nl, r, n3), F32),
        compiler_params=_params(("arbitrary", "arbitrary")),
        name="ada_rows",
    )(c_all, ada_w, ada_b.reshape(nl, 1, n3))


def _row_select(t, tm, lc, ctx_row, lat_row):
    rows = t * tm + lax.broadcasted_iota(jnp.int32, (tm, 1), 0)
    return jnp.where(rows < lc, ctx_row, lat_row)


def _prenorm_kernel(h_ref, modx_ref, modc_ref, nw_ref, o_ref, *, lc, tm):
    t = pl.program_id(1)
    x = h_ref[0]
    y = x * lax.rsqrt(jnp.mean(x * x, axis=-1, keepdims=True) + EPS) * nw_ref[...]
    mx = modx_ref[0]
    mc = modc_ref[...]
    d = D_MODEL
    shift = _row_select(t, tm, lc, mc[:, :d], mx[:, :d])
    scale = _row_select(t, tm, lc, mc[:, d:2 * d], mx[:, d:2 * d])
    o_ref[0] = (y * (1.0 + scale) + shift).astype(o_ref.dtype)


def _prenorm_call(h, modx, modc, norm_w, lc):
    b, l, d = h.shape
    tm = _pick(l, (768, 512, 384, 256, 128))
    return pl.pallas_call(
        functools.partial(_prenorm_kernel, lc=lc, tm=tm),
        grid=(b, l // tm),
        in_specs=[pl.BlockSpec((1, tm, d), lambda i, t: (i, t, 0)),
                  pl.BlockSpec((1, 1, 3 * d), lambda i, t: (i, 0, 0)),
                  pl.BlockSpec((1, 3 * d), lambda i, t: (0, 0)),
                  pl.BlockSpec((1, d), lambda i, t: (0, 0))],
        out_specs=pl.BlockSpec((1, tm, d), lambda i, t: (i, t, 0)),
        out_shape=jax.ShapeDtypeStruct((b, l, d), BF16),
        compiler_params=_params(("parallel", "parallel")),
        name="prenorm",
    )(h, modx, modc, norm_w.reshape(1, d))


def _proj_kernel(x_ref, w_ref, o_ref):
    o_ref[...] = jnp.dot(x_ref[...], w_ref[...], preferred_element_type=F32).astype(o_ref.dtype)


def _proj_call(x, w, out_dtype, name):
    m, k = x.shape
    n = w.shape[1]
    tm = _pick(m, (1024, 768, 512, 384, 256, 128))
    tn = _pick(n, (1536, 1024, 768, 512, 256, 128))
    return pl.pallas_call(
        _proj_kernel,
        grid=(m // tm, n // tn),
        in_specs=[pl.BlockSpec((tm, k), lambda i, j: (i, 0)),
                  pl.BlockSpec((k, tn), lambda i, j: (0, j))],
        out_specs=pl.BlockSpec((tm, tn), lambda i, j: (i, j)),
        out_shape=jax.ShapeDtypeStruct((m, n), out_dtype),
        compiler_params=_params(("parallel", "parallel")),
        name=name,
    )(x, w)


def _bwd_chunk(i, ncc, nc):
    return jnp.where(i < ncc, ncc - 1 - i, ncc + nc - 1 - i)


def _chunk_rows(c):
    return pl.ds(pl.multiple_of(c * CHUNK, CHUNK), CHUNK)


def _gla_kernel(q_ref, k_ref, v_ref, sm_ref, w2_ref, gb_ref, o_ref, la_ref, st_ref, *, ncc, nc):
    sm = sm_ref[0]
    for d in range(2):
        logit = _mm_hp(sm, w2_ref[d]) + gb_ref[d]
        la_ref[d] = _log_sigmoid(logit) * (1.0 / GLA_NORMALIZER)
    st_ref[...] = jnp.zeros_like(st_ref)
    o_ref[...] = jnp.zeros_like(o_ref)
    masks = [_tri_masks(CHUNK, rev) for rev in (False, True)]
    cum_mats = [m[2].astype(F32) for m in masks]

    def chunk_step(d, c):
        rows = _chunk_rows(c)
        incl = masks[d][2]
        la = la_ref[d, rows, :]
        bcum = _cum_rows(cum_mats[d], la)
        b_last = bcum[CHUNK - 1:CHUNK] if d == 0 else bcum[0:1]
        q = q_ref[0, rows, :].astype(F32) * GLA_DK ** -0.5
        k = k_ref[0, rows, :].astype(F32)
        v = v_ref[0, rows, :].astype(F32)
        q_dec = q * jnp.exp(bcum)
        k_inv = k * jnp.exp(-bcum)
        k_end = k * jnp.exp(b_last - bcum)
        att = jnp.where(incl, _mm_nt(q_dec, k_inv), 0.0)
        st = st_ref[d]
        o = _mm(att, v) + _mm_nt(q_dec, st)
        o_ref[0, rows, :] += o
        st_ref[d] = st * jnp.exp(b_last) + _mm_tn(v, k_end)

    def body(i, carry):
        chunk_step(0, i)
        chunk_step(1, _bwd_chunk(i, ncc, nc))
        return carry

    lax.fori_loop(0, nc, body, 0)


def _gla_call(p_gla, p_mla, w2e, gate_b, lc):
    b, l, _ = p_gla.shape
    nc, ncc = l // CHUNK, lc // CHUNK
    return pl.pallas_call(
        functools.partial(_gla_kernel, ncc=ncc, nc=nc),
        grid=(b, GLA_HEADS),
        in_specs=[pl.BlockSpec((1, l, GLA_DK), lambda i, h: (i, 0, h)),
                  pl.BlockSpec((1, l, GLA_DK), lambda i, h: (i, 0, GLA_HEADS + h)),
                  pl.BlockSpec((1, l, GLA_DV), lambda i, h: (i, 0, GLA_HEADS + h)),
                  pl.BlockSpec((1, l, 128), lambda i, h: (i, 0, SMALL_BLOCK)),
                  pl.BlockSpec((2, 128, GLA_DK), lambda i, h: (0, 0, h)),
                  pl.BlockSpec((2, 1, GLA_DK), lambda i, h: (0, 0, h))],
        out_specs=pl.BlockSpec((1, l, GLA_DV), lambda i, h: (i, 0, h)),
        out_shape=jax.ShapeDtypeStruct((b, l, GLA_HEADS * GLA_DV), F32),
        scratch_shapes=[pltpu.VMEM((2, l, GLA_DK), F32), pltpu.VMEM((2, GLA_DV, GLA_DK), F32)],
        compiler_params=_params(("parallel", "parallel")),
        name="gla_scan",
    )(p_gla, p_gla, p_gla, p_mla, w2e, gate_b)


def _gdn_prep_kernel(x_ref, w_ref, o_ref, *, lc, l):
    j = pl.program_id(1)
    x = x_ref[0].astype(F32)
    w = w_ref[...]
    t = lax.broadcasted_iota(jnp.int32, (l, 128), 0)
    start = jnp.where(t < lc, 0, lc)
    end = jnp.where(t < lc, lc, l)
    half = GDN_CONV // 2
    acc = x * w[half:half + 1]
    for s in range(-half, half + 1):
        if s == 0:
            continue
        xs = pltpu.roll(x, (-s) % l, axis=0)
        ok = (t + s >= start) if s < 0 else (t + s < end)
        acc = acc + jnp.where(ok, xs, 0.0) * w[s + half:s + half + 1]
    y = _silu(acc)
    nblk_head = GDN_HEADS * GDN_DK // 128
    inv = lax.rsqrt(jnp.sum(y * y, axis=-1, keepdims=True) + EPS)
    fac = jnp.where(j < nblk_head, inv * GDN_DK ** -0.5, jnp.where(j < 2 * nblk_head, inv, 1.0))
    o_ref[0] = (y * fac).astype(o_ref.dtype)


def _gdn_prep_call(p_gdn, conv_w, lc):
    b, l, n = p_gdn.shape
    return pl.pallas_call(
        functools.partial(_gdn_prep_kernel, lc=lc, l=l),
        grid=(b, n // 128),
        in_specs=[pl.BlockSpec((1, l, 128), lambda i, j: (i, 0, j)),
                  pl.BlockSpec((GDN_CONV, 128), lambda i, j: (0, j))],
        out_specs=pl.BlockSpec((1, l, 128), lambda i, j: (i, 0, j)),
        out_shape=jax.ShapeDtypeStruct((b, l, n), F32),
        compiler_params=_params(("parallel", "parallel")),
        name="gdn_prep",
    )(p_gdn, conv_w)


def _unit_tri_inverse(lm, r, c):
    eye = (r == c).astype(F32)
    same16 = (r // 16) == (c // 16)
    same32 = (r // 32) == (c // 32)
    y = -jnp.where(same16, lm, 0.0)
    y2 = _mm3(y, y)
    y4 = _mm3(y2, y2)
    y8 = _mm3(y4, y4)
    t = eye + y
    t = t + _mm3(t, y2)
    t = t + _mm3(t, y4)
    t = t + _mm3(t, y8)
    c1 = jnp.where(same32 & jnp.logical_not(same16), lm, 0.0)
    t = t - _mm3(_mm3(t, c1), t)
    c2 = jnp.where(same32, 0.0, lm)
    t = t - _mm3(_mm3(t, c2), t)
    return t


def _gdn_kernel(rate_ref, q_ref, k_ref, v_ref, abc_ref, abr_ref, o_ref, s_ref, *, ncc, nc):
    s_ref[...] = jnp.zeros_like(s_ref)
    o_ref[...] = jnp.zeros_like(o_ref)
    masks = [_tri_masks(CHUNK, rev) for rev in (False, True)]
    cum_mats = [m[2].astype(F32) for m in masks]

    def chunk_step(d, c):
        rows = _chunk_rows(c)
        r, cc, incl, strict = masks[d]
        abc = abc_ref[0, 0, rows, :]
        abr = abr_ref[0, 0, c]
        for hh in range(GDN_HB):
            ia = d * GDN_HB + hh
            ib = (2 + d) * GDN_HB + hh
            neg_rate = -jnp.exp(rate_ref[0, 0, ia:ia + 1, :])
            dtb = rate_ref[0, 1, ia:ia + 1, :]
            g_col = neg_rate[:, :1] * _softplus(abc[:, ia:ia + 1] + dtb[:, :1])
            beta_col = jax.nn.sigmoid(abc[:, ib:ib + 1])
            g_row = neg_rate[:, :CHUNK] * _softplus(abr[ia:ia + 1, :] + dtb[:, :CHUNK])
            gc_col = _cum_rows(cum_mats[d], jnp.broadcast_to(g_col, (CHUNK, 128)))
            gc_row = _cum_cols(jnp.broadcast_to(g_row, (8, CHUNK)), cum_mats[1 - d])[0:1]
            g_last = gc_col[CHUNK - 1:CHUNK] if d == 0 else gc_col[0:1]
            diff = gc_col[:, :CHUNK] - gc_row
            decay = jnp.where(incl, jnp.exp(jnp.where(incl, diff, 0.0)), 0.0)

            lanes = slice(hh * GDN_DK, (hh + 1) * GDN_DK)
            q = q_ref[0, rows, lanes].astype(F32)
            k = k_ref[0, rows, lanes].astype(F32)
            v = v_ref[0, rows, lanes].astype(F32)
            k_beta = k * beta_col
            v_beta = v * beta_col
            e_col = jnp.exp(gc_col)
            lm = jnp.where(strict, _mm_nt(k_beta, k) * decay, 0.0)
            t_inv = _unit_tri_inverse(lm, r, cc)
            u = _mm(t_inv, v_beta)
            w = _mm(t_inv, k_beta * e_col)
            att = jnp.where(incl, _mm_nt(q, k) * decay, 0.0)
            q_dec = q * e_col
            k_end = k * jnp.exp(g_last - gc_col)
            si = d * GDN_HB + hh
            s = s_ref[si]
            v_new = u - _mm(w, s)
            o = _mm(q_dec, s) + _mm(att, v_new)
            o_ref[0, rows, lanes] += o
            s_ref[si] = s * jnp.exp(g_last) + _mm_tn(k_end, v_new)

    def body(i, carry):
        chunk_step(0, i)
        chunk_step(1, _bwd_chunk(i, ncc, nc))
        return carry

    lax.fori_loop(0, nc, body, 0)


def _gdn_call(qkvn, abc, abr, a_log, dt_bias, lc):
    b, l, _ = qkvn.shape
    nc, ncc = l // CHUNK, lc // CHUNK
    nhb = GDN_HEADS // GDN_HB
    wb = GDN_HB * GDN_DK
    rate = jnp.stack([a_log, dt_bias]).reshape(2, 2, nhb, GDN_HB).transpose(2, 0, 1, 3).reshape(nhb, 2, 2 * GDN_HB)
    rate = jnp.pad(rate, ((0, 0), (0, 0), (0, 8 - 2 * GDN_HB)))
    rate = jnp.broadcast_to(rate[..., None], (nhb, 2, 8, 128))
    return pl.pallas_call(
        functools.partial(_gdn_kernel, ncc=ncc, nc=nc),
        grid=(b, nhb),
        in_specs=[pl.BlockSpec((1, 2, 8, 128), lambda i, h: (h, 0, 0, 0)),
                  pl.BlockSpec((1, l, wb), lambda i, h: (i, 0, h)),
                  pl.BlockSpec((1, l, wb), lambda i, h: (i, 0, nhb + h)),
                  pl.BlockSpec((1, l, wb), lambda i, h: (i, 0, 2 * nhb + h)),
                  pl.BlockSpec((1, 1, l, 4 * GDN_HB), lambda i, h: (i, h, 0, 0)),
                  pl.BlockSpec((1, 1, nc, 8, CHUNK), lambda i, h: (i, h, 0, 0, 0))],
        out_specs=pl.BlockSpec((1, l, wb), lambda i, h: (i, 0, h)),
        out_shape=jax.ShapeDtypeStruct((b, l, GDN_HEADS * GDN_DV), F32),
        scratch_shapes=[pltpu.VMEM((2 * GDN_HB, GDN_DK, GDN_DV), F32)],
        compiler_params=_params(("parallel", "parallel")),
        name="gdn_scan",
    )(rate, qkvn, qkvn, qkvn, abc, abr)


def _gdn_gate_layouts(p_mla, lc):
    b, l, _ = p_mla.shape
    nhb = GDN_HEADS // GDN_HB
    o = MLA_Q_LORA + MLA_KV_LORA
    ab = p_mla[:, :, o + SM_A:o + SM_A + 32].reshape(b, l, 2, 2, nhb, GDN_HB)
    ab = ab.transpose(0, 4, 1, 2, 3, 5).reshape(b, nhb, l, 4 * GDN_HB)
    abr = ab.reshape(b, nhb, l // CHUNK, CHUNK, 4 * GDN_HB).transpose(0, 1, 2, 4, 3)
    pad = 8 - 4 * GDN_HB
    if pad:
        abr = jnp.pad(abr, ((0, 0), (0, 0), (0, 0), (0, pad), (0, 0)))
    return ab, abr


def _rope_swap(x):
    lane = lax.broadcasted_iota(jnp.int32, x.shape, 1)
    n = x.shape[1]
    return jnp.where(lane % 32 < 16, pltpu.roll(x, n - 16, axis=1), pltpu.roll(x, 16, axis=1))


def _mla_prep_kernel(p_ref, cos_ref, sin_ref, qnw_ref, qup_ref, kvnw_ref, kvup_ref,
                     wqn_ref, wqr_ref, wkn_ref, wkr_ref, q_ref, k_ref, v_ref):
    p = p_ref[0]
    cos_t = cos_ref[...]
    sin_t = sin_ref[...]

    def rms(x, n):
        return x * lax.rsqrt(jnp.sum(x * x, axis=-1, keepdims=True) * (1.0 / n) + EPS)

    def rope(x):
        return x * cos_t + _rope_swap(x) * sin_t

    qd = p[:, :MLA_Q_LORA]
    kvd = p[:, MLA_Q_LORA:MLA_Q_LORA + MLA_KV_LORA]
    sm = p[:, MLA_Q_LORA + MLA_KV_LORA:]
    q = _mm(rms(qd, MLA_Q_LORA) * qnw_ref[...], qup_ref[...])
    kv = _mm(rms(kvd, MLA_KV_LORA) * kvnw_ref[...], kvup_ref[...])
    lane = lax.broadcasted_iota(jnp.int32, sm.shape, 1)
    kr = jnp.where(lane < MLA_ROPE, sm, 0.0)
    kr = rope(rms(kr, MLA_ROPE) * wkr_ref[...])
    for h in range(MLA_HEADS):
        base = h * 256
        q_ref[0, :, base:base + 128] = (rms(q[:, base:base + 128], MLA_NOPE) * wqn_ref[...]).astype(q_ref.dtype)
        qr = rms(q[:, base + 128:base + 256], MLA_ROPE) * wqr_ref[...]
        q_ref[0, :, base + 128:base + 256] = rope(qr).astype(q_ref.dtype)
        kn = kv[:, h * 128:(h + 1) * 128]
        k_ref[0, :, base:base + 128] = (rms(kn, MLA_NOPE) * wkn_ref[...]).astype(k_ref.dtype)
        k_ref[0, :, base + 128:base + 256] = kr.astype(k_ref.dtype)
    v_ref[0] = kv[:, MLA_HEADS * MLA_NOPE:].astype(v_ref.dtype)


def _mla_prep_call(p_mla, cos_t, sin_t, qnw, qup, kvnw, kvup, wqn, wqr, wkn, wkr):
    b, l, gw = p_mla.shape
    tm = _pick(l, (256, 128))
    hq = MLA_HEADS * 256
    full = lambda shape: pl.BlockSpec(shape, lambda i, t: tuple(0 for _ in shape))
    return pl.pallas_call(
        _mla_prep_kernel,
        grid=(b, l // tm),
        in_specs=[pl.BlockSpec((1, tm, gw), lambda i, t: (i, t, 0)),
                  pl.BlockSpec((tm, 128), lambda i, t: (t, 0)),
                  pl.BlockSpec((tm, 128), lambda i, t: (t, 0)),
                  full((1, MLA_Q_LORA)), full((MLA_Q_LORA, hq)),
                  full((1, MLA_KV_LORA)), full((MLA_KV_LORA, hq)),
                  full((1, 128)), full((1, 128)), full((1, 128)), full((1, 128))],
        out_specs=[pl.BlockSpec((1, tm, hq), lambda i, t: (i, t, 0)),
                   pl.BlockSpec((1, tm, hq), lambda i, t: (i, t, 0)),
                   pl.BlockSpec((1, tm, MLA_HEADS * MLA_DV), lambda i, t: (i, t, 0))],
        out_shape=[jax.ShapeDtypeStruct((b, l, hq), BF16),
                   jax.ShapeDtypeStruct((b, l, hq), BF16),
                   jax.ShapeDtypeStruct((b, l, MLA_HEADS * MLA_DV), BF16)],
        compiler_params=_params(("parallel", "parallel")),
        name="mla_prep",
    )(p_mla, cos_t, sin_t, qnw, qup, kvnw, kvup, wqn, wqr, wkn, wkr)


def _mla_attn_kernel(q_ref, k_ref, v_ref, o_ref, *, lc, tq, q_off):
    qi = pl.program_id(2) + q_off
    q = q_ref[0]

    def attend(nk):
        s = lax.dot_general(q, k_ref[0, :nk, :], (((1,), (1,)), ((), ())),
                            preferred_element_type=F32) * MLA_SCALE
        m = jnp.max(s, axis=-1, keepdims=True)
        e = jnp.exp(s - m)
        den = jnp.sum(e, axis=-1, keepdims=True)
        o = jnp.dot(e.astype(BF16), v_ref[0, :nk, :], preferred_element_type=F32)
        o_ref[0] = (o / den).astype(o_ref.dtype)

    if q_off == 0:
        @pl.when(qi * tq < lc)
        def _():
            attend(lc)

        @pl.when(qi * tq >= lc)
        def _():
            attend(k_ref.shape[1])
    else:
        attend(k_ref.shape[1])


def _mla_attn_call(q, k, v, lc, with_ctx):
    b, l, _ = q.shape
    tq = _pick(lc, (256, 128))
    q_off = 0 if with_ctx else lc // tq
    nq = l // tq - q_off
    return pl.pallas_call(
        functools.partial(_mla_attn_kernel, lc=lc, tq=tq, q_off=q_off),
        grid=(b, MLA_HEADS, nq),
        in_specs=[pl.BlockSpec((1, tq, 256), lambda i, h, t: (i, t + q_off, h)),
                  pl.BlockSpec((1, l, 256), lambda i, h, t: (i, 0, h)),
                  pl.BlockSpec((1, l, MLA_DV), lambda i, h, t: (i, 0, h))],
        out_specs=pl.BlockSpec((1, tq, MLA_DV), lambda i, h, t: (i, t, h)),
        out_shape=jax.ShapeDtypeStruct((b, nq * tq, MLA_HEADS * MLA_DV), F32),
        compiler_params=_params(("parallel", "parallel", "arbitrary")),
        name="mla_attn",
    )(q, k, v)


def _merge_kernel(h_ref, oa_ref, ob_ref, oc_ref, za_ref, zb_ref, zc_ref, g0_ref, g1_ref, g2_ref,
                  modx_ref, modc_ref, wa_ref, wb_ref, wbr_ref, wout_ref, o_ref, *, lc, tm, t_off):
    t = pl.program_id(1) + t_off

    def head_norm(o, width):
        parts = []
        for h in range(BRANCH_W // width):
            x = o[:, h * width:(h + 1) * width]
            parts.append(x * lax.rsqrt(jnp.mean(x * x, axis=-1, keepdims=True) + EPS))
        return jnp.concatenate(parts, axis=-1)

    ya = head_norm(oa_ref[0], GLA_DV) * wa_ref[...] * _silu(za_ref[0].astype(F32))
    yb = head_norm(ob_ref[0], GDN_DV) * wb_ref[...] * _silu(zb_ref[0].astype(F32))
    yc = oc_ref[0] * _silu(zc_ref[0].astype(F32))
    acc = jax.nn.sigmoid(g0_ref[0].astype(F32)) * _mm(ya, wbr_ref[0])
    acc = acc + jax.nn.sigmoid(g1_ref[0].astype(F32)) * _mm(yb, wbr_ref[1])
    acc = acc + jax.nn.sigmoid(g2_ref[0].astype(F32)) * _mm(yc, wbr_ref[2])
    d = D_MODEL
    gate = _row_select(t, tm, lc, modc_ref[:, 2 * d:], modx_ref[0][:, 2 * d:])
    o_ref[0] = h_ref[0] + gate * _mm(acc, wout_ref[...])


def _merge_call(h, oa, ob, oc, p_zg, modx, modc, wa, wb, w_branch, w_out, lc, with_ctx):
    b, l, d = h.shape
    tm = _pick(lc, (256, 128))
    t_off = 0 if with_ctx else lc // tm
    nt = l // tm - t_off
    oc_off = 0 if oc.shape[1] == l else lc // tm
    row = lambda i, t: (i, t + t_off, 0)
    zg = lambda n: pl.BlockSpec((1, tm, d), lambda i, t: (i, t + t_off, n))
    return pl.pallas_call(
        functools.partial(_merge_kernel, lc=lc, tm=tm, t_off=t_off),
        grid=(b, nt),
        in_specs=[pl.BlockSpec((1, tm, d), row), pl.BlockSpec((1, tm, d), row), pl.BlockSpec((1, tm, d), row),
                  pl.BlockSpec((1, tm, d), lambda i, t: (i, t + t_off - oc_off, 0)),
                  zg(0), zg(1), zg(2), zg(3), zg(4), zg(5),
                  pl.BlockSpec((1, 1, 3 * d), lambda i, t: (i, 0, 0)),
                  pl.BlockSpec((1, 3 * d), lambda i, t: (0, 0)),
                  pl.BlockSpec((1, d), lambda i, t: (0, 0)),
                  pl.BlockSpec((1, d), lambda i, t: (0, 0)),
                  pl.BlockSpec((3, d, d), lambda i, t: (0, 0, 0)),
                  pl.BlockSpec((d, d), lambda i, t: (0, 0))],
        out_specs=pl.BlockSpec((1, tm, d), lambda i, t: (i, t, 0)),
        out_shape=jax.ShapeDtypeStruct((b, nt * tm, d), F32),
        compiler_params=_params(("parallel", "parallel")),
        name="merge",
    )(h, oa, ob, oc, p_zg, p_zg, p_zg, p_zg, p_zg, p_zg, modx, modc, wa, wb, w_branch, w_out)


def _cols(w, *names):
    return jnp.concatenate([w[:, _IN_OFF[n]:_IN_OFF[n] + _IN_LEN[n]] for n in names], axis=1)


def _rope_tables(t_lat, lc):
    rows = t_lat // GRID_W
    row = jnp.repeat(jnp.arange(rows, dtype=F32), GRID_W)
    col = jnp.tile(jnp.arange(GRID_W, dtype=F32), rows)
    inv_freq = jnp.power(ROPE_BASE, -jnp.arange(ROPE_FREQ, dtype=F32) / ROPE_FREQ)
    ar = row[:, None] * inv_freq
    ac = col[:, None] * inv_freq
    zeros = jnp.zeros((t_lat, 128 - MLA_ROPE), F32)
    cos_t = jnp.concatenate([jnp.cos(ar), jnp.cos(ar), jnp.cos(ac), jnp.cos(ac), zeros], axis=1)
    sin_t = jnp.concatenate([-jnp.sin(ar), jnp.sin(ar), -jnp.sin(ac), jnp.sin(ac), zeros], axis=1)
    cos_c = jnp.concatenate([jnp.ones((lc, MLA_ROPE), F32), jnp.zeros((lc, 128 - MLA_ROPE), F32)], axis=1)
    return (jnp.concatenate([cos_c, cos_t], axis=0),
            jnp.concatenate([jnp.zeros((lc, 128), F32), sin_t], axis=0))


def _pad_lanes(w, n=128):
    return jnp.pad(w, (0, n - w.shape[0])).reshape(1, n)


def kernel(x, c, ctx, c_ctx, norm_w, ada_w, ada_b, w_in, gla_gate_w2, gla_gate_b, gla_norm_w, gdn_conv_w,
           gdn_a_log, gdn_dt_bias, gdn_norm_w, mla_q_norm_w, mla_q_up, mla_kv_norm_w, mla_kv_up,
           mla_qn_nope, mla_qn_rope, mla_kn_nope, mla_kn_rope, w_branch, w_out):
    b, t_lat, d = x.shape
    lc = ctx.shape[1]
    l = lc + t_lat
    depth = w_in.shape[0]
    assert d == D_MODEL and lc % 128 == 0 and t_lat % 128 == 0

    h = jnp.concatenate([ctx, x], axis=1)
    rows = ((b + 1 + 7) // 8) * 8
    c_all = jnp.concatenate([c, c_ctx[None, :], jnp.zeros((rows - b - 1, d), F32)], axis=0)
    mod = _ada_call(c_all, ada_w, ada_b)
    cos_t, sin_t = _rope_tables(t_lat, lc)

    for li in range(depth):
        last = li == depth - 1
        modx = mod[li, :b].reshape(b, 1, 3 * d)
        modc = mod[li, b:b + 1]
        w = w_in[li]
        w_mla = _cols(w, 'mla_q_down', 'mla_kv_down', 'mla_k_rope', 'gla_gate_lr', 'gdn_a', 'gdn_b').astype(BF16)
        w_gla = _cols(w, 'gla_q', 'gla_k', 'gla_v').astype(BF16)
        w_gdn = _cols(w, 'gdn_qkv').astype(BF16)
        w_zg = _cols(w, 'gla_z', 'gdn_z', 'mla_z', 'merge_gate').astype(BF16)

        hn = _prenorm_call(h, modx, modc, norm_w[li], lc).reshape(b * l, d)
        p_mla = _proj_call(hn, w_mla, F32, "proj_mla").reshape(b, l, -1)
        p_gla = _proj_call(hn, w_gla, F32, "proj_gla").reshape(b, l, -1)
        p_gdn = _proj_call(hn, w_gdn, F32, "proj_gdn").reshape(b, l, -1)
        p_zg = _proj_call(hn, w_zg, F32, "proj_zg").reshape(b, l, -1)

        w2e = jnp.zeros((2, 128, GLA_HEADS * GLA_DK), F32).at[0, SM_LR:SM_LR + GLA_RANK].set(
            gla_gate_w2[li, 0]).at[1, SM_LR + GLA_RANK:SM_LR + 2 * GLA_RANK].set(gla_gate_w2[li, 1])
        oa = _gla_call(p_gla, p_mla, w2e, gla_gate_b[li].reshape(2, 1, -1), lc)

        qkvn = _gdn_prep_call(p_gdn, gdn_conv_w[li], lc)
        abc, abr = _gdn_gate_layouts(p_mla, lc)
        ob = _gdn_call(qkvn, abc, abr, gdn_a_log[li], gdn_dt_bias[li], lc)

        qup = mla_q_up[li].reshape(MLA_Q_LORA, MLA_HEADS, MLA_NOPE + MLA_ROPE)
        qup = jnp.pad(qup, ((0, 0), (0, 0), (0, 256 - MLA_NOPE - MLA_ROPE))).reshape(MLA_Q_LORA, -1).astype(BF16)
        kvup = mla_kv_up[li].reshape(MLA_KV_LORA, MLA_HEADS, MLA_NOPE + MLA_DV)
        kvup = jnp.concatenate([kvup[:, :, :MLA_NOPE].reshape(MLA_KV_LORA, -1),
                                kvup[:, :, MLA_NOPE:].reshape(MLA_KV_LORA, -1)], axis=1).astype(BF16)
        qm, km, vm = _mla_prep_call(
            p_mla, cos_t, sin_t, mla_q_norm_w[li].reshape(1, -1), qup, mla_kv_norm_w[li].reshape(1, -1), kvup,
            mla_qn_nope[li].reshape(1, -1), _pad_lanes(mla_qn_rope[li]),
            mla_kn_nope[li].reshape(1, -1), _pad_lanes(mla_kn_rope[li]))
        oc = _mla_attn_call(qm, km, vm, lc, with_ctx=not last)

        h = _merge_call(h, oa, ob, oc, p_zg, modx, modc,
                        jnp.tile(gla_norm_w[li], GLA_HEADS).reshape(1, -1),
                        jnp.tile(gdn_norm_w[li], GDN_HEADS).reshape(1, -1),
                        w_branch[li].astype(BF16), w_out[li].astype(BF16), lc, with_ctx=not last)
    return h
```

```python
import functools
import math

import numpy as np
import jax
import jax.numpy as jnp
from jax import lax
from jax.experimental import pallas as pl
from jax.experimental.pallas import tpu as pltpu

F32 = jnp.float32
BF16 = jnp.bfloat16

D_MODEL = 1024
EPS = 1e-6
GRID_W = 64
CHUNK = 64

GLA_HEADS, GLA_DK, GLA_DV, GLA_RANK, GLA_NORMALIZER = 4, 128, 256, 16, 16.0
GDN_HEADS, GDN_DK, GDN_DV, GDN_CONV = 8, 128, 128, 5
GDN_HB = 4
MLA_HEADS, MLA_Q_LORA, MLA_KV_LORA, MLA_NOPE, MLA_ROPE, MLA_DV = 8, 384, 256, 128, 64, 128
MLA_SCALE = (MLA_NOPE + MLA_ROPE) ** -0.5
ROPE_FREQ = MLA_ROPE // 4
ROPE_BASE = 10000.0
BRANCH_W = 1024

_IN_SIZES = (512, 512, 1024, 32, 1024, 3072, 16, 16, 1024, 384, 256, 64, 1024, 3072)
_IN_NAMES = ('gla_q', 'gla_k', 'gla_v', 'gla_gate_lr', 'gla_z', 'gdn_qkv', 'gdn_a', 'gdn_b', 'gdn_z',
             'mla_q_down', 'mla_kv_down', 'mla_k_rope', 'mla_z', 'merge_gate')
_IN_OFF = dict(zip(_IN_NAMES, np.concatenate([[0], np.cumsum(_IN_SIZES)[:-1]]).tolist()))
_IN_LEN = dict(zip(_IN_NAMES, _IN_SIZES))

SM_ROPE, SM_LR, SM_A, SM_B = 0, 64, 96, 112
MLA_GROUP_W = MLA_Q_LORA + MLA_KV_LORA + 128
SMALL_BLOCK = (MLA_Q_LORA + MLA_KV_LORA) // 128

VMEM_LIMIT = 48 * 1024 * 1024


def _params(sem):
    return pltpu.CompilerParams(dimension_semantics=sem, vmem_limit_bytes=VMEM_LIMIT)


def _pick(n, cands):
    for c in cands:
        if n % c == 0:
            return c
    raise ValueError(f"no tile for {n} in {cands}")


def _mm(a, b):
    return jnp.dot(a.astype(BF16), b.astype(BF16), preferred_element_type=F32)


def _mm_nt(a, b):
    return lax.dot_general(a.astype(BF16), b.astype(BF16), (((1,), (1,)), ((), ())),
                           preferred_element_type=F32)


def _mm_tn(a, b):
    return lax.dot_general(a.astype(BF16), b.astype(BF16), (((0,), (0,)), ((), ())),
                           preferred_element_type=F32)


def _mm_hp(a, b):
    return jnp.dot(a, b, precision=lax.Precision.HIGHEST, preferred_element_type=F32)


def _split(a):
    hi = a.astype(BF16)
    lo = (a - hi.astype(F32)).astype(BF16)
    return hi, lo


def _mm3(a, b):
    ah, al = _split(a)
    bh, bl = _split(b)
    d = functools.partial(jnp.dot, preferred_element_type=F32)
    return d(ah, bh) + (d(ah, bl) + d(al, bh))


_MMX = _mm
_TRI_BLOCK = 8


def _split3(a):
    hi = a.astype(BF16)
    r1 = a - hi.astype(F32)
    mid = r1.astype(BF16)
    lo = (r1 - mid.astype(F32)).astype(BF16)
    return hi, mid, lo


def _cum_rows(cum, x):
    cb = cum.astype(BF16)
    d = functools.partial(jnp.dot, preferred_element_type=F32)
    hi, mid, lo = _split3(x)
    return d(cb, hi) + (d(cb, mid) + d(cb, lo))


def _cum_cols(x, cum):
    cb = cum.astype(BF16)
    d = functools.partial(jnp.dot, preferred_element_type=F32)
    hi, mid, lo = _split3(x)
    return d(hi, cb) + (d(mid, cb) + d(lo, cb))


def _softplus(x):
    return jnp.maximum(x, 0.0) + jnp.log1p(jnp.exp(-jnp.abs(x)))


def _log_sigmoid(x):
    return jnp.minimum(x, 0.0) - jnp.log1p(jnp.exp(-jnp.abs(x)))


def _silu(x):
    return x * jax.nn.sigmoid(x)


def _tri_masks(n, reverse):
    r = lax.broadcasted_iota(jnp.int32, (n, n), 0)
    c = lax.broadcasted_iota(jnp.int32, (n, n), 1)
    incl = (c >= r) if reverse else (c <= r)
    strict = (c > r) if reverse else (c < r)
    return r, c, incl, strict


def _ada_kernel(c_ref, w_ref, b_ref, o_ref):
    o_ref[0] = _mm_hp(_silu(c_ref[...]), w_ref[0]) + b_ref[0]


def _ada_call(c_all, ada_w, ada_b):
    nl, d, n3 = ada_w.shape
    r = c_all.shape[0]
    tn = 1024
    return pl.pallas_call(
        _ada_kernel,
        grid=(nl, n3 // tn),
        in_specs=[pl.BlockSpec((r, d), lambda l, j: (0, 0)),
                  pl.BlockSpec((1, d, tn), lambda l, j: (l, 0, j)),
                  pl.BlockSpec((1, 1, tn), lambda l, j: (l, 0, j))],
        out_specs=pl.BlockSpec((1, r, tn), lambda l, j: (l, 0, j)),
        out_shape=jax.ShapeDtypeStruct((nl, r, n3), F32),
        compiler_params=_params(("arbitrary", "arbitrary")),
        name="ada_rows",
    )(c_all, ada_w, ada_b.reshape(nl, 1, n3))


def _row_select(t, tm, lc, ctx_row, lat_row):
    rows = t * tm + lax.broadcasted_iota(jnp.int32, (tm, 1), 0)
    return jnp.where(rows < lc, ctx_row, lat_row)


def _prenorm_kernel(h_ref, modx_ref, modc_ref, nw_ref, o_ref, *, lc, tm):
    t = pl.program_id(1)
    x = h_ref[0]
    y = x * lax.rsqrt(jnp.mean(x * x, axis=-1, keepdims=True) + EPS) * nw_ref[...]
    mx = modx_ref[0]
    mc = modc_ref[...]
    d = D_MODEL
    shift = _row_select(t, tm, lc, mc[:, :d], mx[:, :d])
    scale = _row_select(t, tm, lc, mc[:, d:2 * d], mx[:, d:2 * d])
    o_ref[0] = (y * (1.0 + scale) + shift).astype(o_ref.dtype)


def _prenorm_call(h, modx, modc, norm_w, lc):
    b, l, d = h.shape
    tm = _pick(l, (768, 512, 384, 256, 128))
    return pl.pallas_call(
        functools.partial(_prenorm_kernel, lc=lc, tm=tm),
        grid=(b, l // tm),
        in_specs=[pl.BlockSpec((1, tm, d), lambda i, t: (i, t, 0)),
                  pl.BlockSpec((1, 1, 3 * d), lambda i, t: (i, 0, 0)),
                  pl.BlockSpec((1, 3 * d), lambda i, t: (0, 0)),
                  pl.BlockSpec((1, d), lambda i, t: (0, 0))],
        out_specs=pl.BlockSpec((1, tm, d), lambda i, t: (i, t, 0)),
        out_shape=jax.ShapeDtypeStruct((b, l, d), BF16),
        compiler_params=_params(("parallel", "parallel")),
        name="prenorm",
    )(h, modx, modc, norm_w.reshape(1, d))


def _proj_kernel(x_ref, w_ref, o_ref):
    o_ref[...] = jnp.dot(x_ref[...], w_ref[...], preferred_element_type=F32).astype(o_ref.dtype)


def _proj_call(x, w, out_dtype, name):
    m, k = x.shape
    n = w.shape[1]
    tm = _pick(m, (1024, 768, 512, 384, 256, 128))
    tn = _pick(n, (1536, 1024, 768, 512, 256, 128))
    return pl.pallas_call(
        _proj_kernel,
        grid=(m // tm, n // tn),
        in_specs=[pl.BlockSpec((tm, k), lambda i, j: (i, 0)),
                  pl.BlockSpec((k, tn), lambda i, j: (0, j))],
        out_specs=pl.BlockSpec((tm, tn), lambda i, j: (i, j)),
        out_shape=jax.ShapeDtypeStruct((m, n), out_dtype),
        compiler_params=_params(("parallel", "parallel")),
        name=name,
    )(x, w)


def _bwd_chunk(i, ncc, nc):
    return jnp.where(i < ncc, ncc - 1 - i, ncc + nc - 1 - i)


def _chunk_rows(c):
    return pl.ds(pl.multiple_of(c * CHUNK, CHUNK), CHUNK)


def _gla_kernel(q_ref, k_ref, v_ref, sm_ref, w2_ref, gb_ref, o_ref, la_ref, st_ref, *, ncc, nc):
    sm = sm_ref[0]
    for d in range(2):
        logit = _mm_hp(sm, w2_ref[d]) + gb_ref[d]
        la_ref[d] = _log_sigmoid(logit) * (1.0 / GLA_NORMALIZER)
    st_ref[...] = jnp.zeros_like(st_ref)
    o_ref[...] = jnp.zeros_like(o_ref)
    masks = [_tri_masks(CHUNK, rev) for rev in (False, True)]
    cum_mats = [m[2].astype(F32) for m in masks]

    def body(i, carry):
        dirs = (0, 1)
        rows = [_chunk_rows(i), _chunk_rows(_bwd_chunk(i, ncc, nc))]
        bcum = [_cum_rows(cum_mats[d], la_ref[d, rows[d], :]) for d in dirs]
        b_last = [bcum[0][CHUNK - 1:CHUNK], bcum[1][0:1]]
        q = [q_ref[0, rows[d], :].astype(F32) * GLA_DK ** -0.5 for d in dirs]
        k = [k_ref[0, rows[d], :].astype(F32) for d in dirs]
        v = [v_ref[0, rows[d], :].astype(F32) for d in dirs]
        q_dec = [q[d] * jnp.exp(bcum[d]) for d in dirs]
        k_inv = [k[d] * jnp.exp(-bcum[d]) for d in dirs]
        k_end = [k[d] * jnp.exp(b_last[d] - bcum[d]) for d in dirs]
        st = [st_ref[d] for d in dirs]
        att = [jnp.where(masks[d][2], _mm_nt(q_dec[d], k_inv[d]), 0.0) for d in dirs]
        o_inter = [_mm_nt(q_dec[d], st[d]) for d in dirs]
        upd = [_mm_tn(v[d], k_end[d]) for d in dirs]
        o_intra = [_mm(att[d], v[d]) for d in dirs]
        for d in dirs:
            st_ref[d] = st[d] * jnp.exp(b_last[d]) + upd[d]
        for d in dirs:
            o_ref[0, rows[d], :] += o_intra[d] + o_inter[d]
        return carry

    lax.fori_loop(0, nc, body, 0)


def _gla_call(p_gla, p_mla, w2e, gate_b, lc):
    b, l, _ = p_gla.shape
    nc, ncc = l // CHUNK, lc // CHUNK
    return pl.pallas_call(
        functools.partial(_gla_kernel, ncc=ncc, nc=nc),
        grid=(b, GLA_HEADS),
        in_specs=[pl.BlockSpec((1, l, GLA_DK), lambda i, h: (i, 0, h)),
                  pl.BlockSpec((1, l, GLA_DK), lambda i, h: (i, 0, GLA_HEADS + h)),
                  pl.BlockSpec((1, l, GLA_DV), lambda i, h: (i, 0, GLA_HEADS + h)),
                  pl.BlockSpec((1, l, 128), lambda i, h: (i, 0, SMALL_BLOCK)),
                  pl.BlockSpec((2, 128, GLA_DK), lambda i, h: (0, 0, h)),
                  pl.BlockSpec((2, 1, GLA_DK), lambda i, h: (0, 0, h))],
        out_specs=pl.BlockSpec((1, l, GLA_DV), lambda i, h: (i, 0, h)),
        out_shape=jax.ShapeDtypeStruct((b, l, GLA_HEADS * GLA_DV), F32),
        scratch_shapes=[pltpu.VMEM((2, l, GLA_DK), F32), pltpu.VMEM((2, GLA_DV, GLA_DK), F32)],
        compiler_params=_params(("parallel", "parallel")),
        name="gla_scan",
    )(p_gla, p_gla, p_gla, p_mla, w2e, gate_b)


def _gdn_prep_kernel(x_ref, w_ref, o_ref, *, lc, l):
    j = pl.program_id(1)
    x = x_ref[0].astype(F32)
    w = w_ref[...]
    t = lax.broadcasted_iota(jnp.int32, (l, 128), 0)
    start = jnp.where(t < lc, 0, lc)
    end = jnp.where(t < lc, lc, l)
    half = GDN_CONV // 2
    acc = x * w[half:half + 1]
    for s in range(-half, half + 1):
        if s == 0:
            continue
        xs = pltpu.roll(x, (-s) % l, axis=0)
        ok = (t + s >= start) if s < 0 else (t + s < end)
        acc = acc + jnp.where(ok, xs, 0.0) * w[s + half:s + half + 1]
    y = _silu(acc)
    nblk_head = GDN_HEADS * GDN_DK // 128
    inv = lax.rsqrt(jnp.sum(y * y, axis=-1, keepdims=True) + EPS)
    fac = jnp.where(j < nblk_head, inv * GDN_DK ** -0.5, jnp.where(j < 2 * nblk_head, inv, 1.0))
    o_ref[0] = (y * fac).astype(o_ref.dtype)


def _gdn_prep_call(p_gdn, conv_w, lc):
    b, l, n = p_gdn.shape
    return pl.pallas_call(
        functools.partial(_gdn_prep_kernel, lc=lc, l=l),
        grid=(b, n // 128),
        in_specs=[pl.BlockSpec((1, l, 128), lambda i, j: (i, 0, j)),
                  pl.BlockSpec((GDN_CONV, 128), lambda i, j: (0, j))],
        out_specs=pl.BlockSpec((1, l, 128), lambda i, j: (i, 0, j)),
        out_shape=jax.ShapeDtypeStruct((b, l, n), F32),
        compiler_params=_params(("parallel", "parallel")),
        name="gdn_prep",
    )(p_gdn, conv_w)


def _each(fn, *lists):
    return [fn(*args) for args in zip(*lists)]


def _unit_tri_solves(lms, rhs, r, c):
    eye = (r == c).astype(F32)
    same = lambda s: (r // s) == (c // s)
    nb = _TRI_BLOCK
    y = [-jnp.where(same(nb), lm, 0.0) for lm in lms]
    t = [eye + a for a in y]
    p = y
    for _ in range(int(math.log2(nb)) - 1):
        p = _each(_MMX, p, p)
        t = _each(lambda a, b: a + _MMX(a, b), t, p)
    sizes = [nb * 2 ** j for j in range(int(math.log2(CHUNK // nb)))]
    offs = [[jnp.where(same(2 * s), jnp.where(same(s), 0.0, lm), 0.0) for lm in lms] for s in sizes]
    a = _each(_mm, t, rhs)
    bs = [_each(_mm, t, cj) for cj in offs]
    for j in range(len(bs)):
        a = _each(lambda x, m: x - _mm(m, x), a, bs[j])
        for kk in range(j + 1, len(bs)):
            bs[kk] = _each(lambda x, m: x - _mm(m, x), bs[kk], bs[j])
    return a


def _gdn_kernel(rate_ref, ratel_ref, q_ref, k_ref, v_ref, abc_ref, abr_ref, o_ref, s_ref, *, ncc, nc):
    na = 2 * GDN_HB
    s_ref[...] = jnp.zeros_like(s_ref)
    o_ref[...] = jnp.zeros_like(o_ref)
    masks = [_tri_masks(CHUNK, rev) for rev in (False, True)]
    cum_mats = [m[2].astype(F32) for m in masks]
    r, cc = masks[0][0], masks[0][1]
    probs = [(d, hh) for d in range(2) for hh in range(GDN_HB)]
    neg_rate_rows = -jnp.exp(rate_ref[0, 0])[:, :CHUNK]
    dtb_rows = rate_ref[0, 1][:, :CHUNK]
    neg_rate_lanes = -jnp.exp(ratel_ref[0, 0, 0:1, :])
    dtb_lanes = ratel_ref[0, 1, 0:1, :]

    def body(i, carry):
        dirs = (0, 1)
        chunk = [i, _bwd_chunk(i, ncc, nc)]
        rows = [_chunk_rows(c) for c in chunk]
        abc = [abc_ref[0, 0, rw, :] for rw in rows]
        abr = [abr_ref[0, 0, c] for c in chunk]
        incl = [masks[d][2] for d, _ in probs]
        strict = [masks[d][3] for d, _ in probs]
        g_cols = [neg_rate_lanes * _softplus(abc[d] + dtb_lanes) for d in dirs]
        g_rows = [neg_rate_rows * _softplus(abr[d] + dtb_rows) for d in dirs]
        gc_cols = [_cum_rows(cum_mats[d], g_cols[d]) for d in dirs]
        gc_rows = [_cum_cols(g_rows[d], cum_mats[1 - d]) for d in dirs]
        beta_cols = [jax.nn.sigmoid(abc[d]) for d in dirs]
        col = [d * GDN_HB + hh for d, hh in probs]
        gc_col = [jnp.broadcast_to(gc_cols[d][:, j:j + 1], (CHUNK, 128)) for (d, _), j in zip(probs, col)]
        gc_row = [gc_rows[d][j:j + 1, :] for (d, _), j in zip(probs, col)]
        beta_col = [beta_cols[d][:, na + j:na + j + 1] for (d, _), j in zip(probs, col)]
        g_last =[gc[CHUNK - 1:CHUNK] if d == 0 else gc[0:1] for (d, _), gc in zip(probs, gc_col)]
        decay = [jnp.where(m, jnp.exp(jnp.where(m, a[:, :CHUNK] - b, 0.0)), 0.0)
                 for m, a, b in zip(incl, gc_col, gc_row)]

        lanes = [slice(hh * GDN_DK, (hh + 1) * GDN_DK) for _, hh in probs]
        q = [q_ref[0, rows[d], ln].astype(F32) for (d, _), ln in zip(probs, lanes)]
        k = [k_ref[0, rows[d], ln].astype(F32) for (d, _), ln in zip(probs, lanes)]
        v = [v_ref[0, rows[d], ln].astype(F32) for (d, _), ln in zip(probs, lanes)]
        k_beta = _each(jnp.multiply, k, beta_col)
        v_beta = _each(jnp.multiply, v, beta_col)
        e_col = _each(jnp.exp, gc_col)
        kk = _each(_mm_nt, k_beta, k)
        qk = _each(_mm_nt, q, k)
        lm = [jnp.where(m, a * dc, 0.0) for m, a, dc in zip(strict, kk, decay)]
        rhs = [jnp.concatenate([vb, kb * e], axis=1) for vb, kb, e in zip(v_beta, k_beta, e_col)]
        uw = _unit_tri_solves(lm, rhs, r, cc)
        u = [x[:, :GDN_DV] for x in uw]
        w = [x[:, GDN_DV:] for x in uw]
        att =[jnp.where(m, a * dc, 0.0) for m, a, dc in zip(incl, qk, decay)]
        q_dec = _each(jnp.multiply, q, e_col)
        k_end = [kx * jnp.exp(gl - gc) for kx, gl, gc in zip(k, g_last, gc_col)]
        s = [s_ref[d * GDN_HB + hh] for d, hh in probs]
        ws = _each(_mm, w, s)
        o_inter = _each(_mm, q_dec, s)
        v_new = _each(jnp.subtract, u, ws)
        upd = _each(_mm_tn, k_end, v_new)
        o_intra = _each(_mm, att, v_new)
        for (d, hh), sx, gl, up in zip(probs, s, g_last, upd):
            s_ref[d * GDN_HB + hh] = sx * jnp.exp(gl) + up
        for (d, _), ln, a, b in zip(probs, lanes, o_intra, o_inter):
            o_ref[0, rows[d], ln] += a + b
        return carry

    lax.fori_loop(0, nc, body, 0)


def _gdn_call(qkvn, abc, abr, a_log, dt_bias, lc):
    b, l, _ = qkvn.shape
    nc, ncc = l // CHUNK, lc // CHUNK
    nhb = GDN_HEADS // GDN_HB
    wb = GDN_HB * GDN_DK
    na = 2 * GDN_HB
    rate = jnp.stack([a_log, dt_bias]).reshape(2, 2, nhb, GDN_HB).transpose(2, 0, 1, 3).reshape(nhb, 2, na)
    rate = jnp.pad(rate, ((0, 0), (0, 0), (0, 128 - na)))
    rate_rows = jnp.broadcast_to(rate[:, :, :8, None], (nhb, 2, 8, 128))
    rate_lanes = jnp.broadcast_to(rate[:, :, None, :], (nhb, 2, 8, 128))
    return pl.pallas_call(
        functools.partial(_gdn_kernel, ncc=ncc, nc=nc),
        grid=(b, nhb),
        in_specs=[pl.BlockSpec((1, 2, 8, 128), lambda i, h: (h, 0, 0, 0)),
                  pl.BlockSpec((1, 2, 8, 128), lambda i, h: (h, 0, 0, 0)),
                  pl.BlockSpec((1, l, wb), lambda i, h: (i, 0, h)),
                  pl.BlockSpec((1, l, wb), lambda i, h: (i, 0, nhb + h)),
                  pl.BlockSpec((1, l, wb), lambda i, h: (i, 0, 2 * nhb + h)),
                  pl.BlockSpec((1, 1, l, 128), lambda i, h: (i, h, 0, 0)),
                  pl.BlockSpec((1, 1, nc, 8, CHUNK), lambda i, h: (i, h, 0, 0, 0))],
        out_specs=pl.BlockSpec((1, l, wb), lambda i, h: (i, 0, h)),
        out_shape=jax.ShapeDtypeStruct((b, l, GDN_HEADS * GDN_DV), F32),
        scratch_shapes=[pltpu.VMEM((2 * GDN_HB, GDN_DK, GDN_DV), F32)],
        compiler_params=_params(("parallel", "parallel")),
        name="gdn_scan",
    )(rate_rows, rate_lanes, qkvn, qkvn, qkvn, abc, abr)


def _gdn_gate_layouts(p_mla, lc):
    b, l, _ = p_mla.shape
    nhb = GDN_HEADS // GDN_HB
    na = 2 * GDN_HB
    o = MLA_Q_LORA + MLA_KV_LORA
    ab = p_mla[:, :, o + SM_A:o + SM_A + 32].reshape(b, l, 2, 2, nhb, GDN_HB)
    ab = ab.transpose(0, 4, 1, 2, 3, 5).reshape(b, nhb, l, 2 * na)
    abr = ab[..., :na].reshape(b, nhb, l // CHUNK, CHUNK, na).transpose(0, 1, 2, 4, 3)
    abr = jnp.pad(abr, ((0, 0), (0, 0), (0, 0), (0, 8 - na), (0, 0)))
    abc = jnp.pad(ab, ((0, 0), (0, 0), (0, 0), (0, 128 - 2 * na)))
    return abc, abr


def _rope_swap(x):
    lane = lax.broadcasted_iota(jnp.int32, x.shape, 1)
    n = x.shape[1]
    return jnp.where(lane % 32 < 16, pltpu.roll(x, n - 16, axis=1), pltpu.roll(x, 16, axis=1))


def _mla_prep_kernel(p_ref, cos_ref, sin_ref, qnw_ref, qup_ref, kvnw_ref, kvup_ref,
                     wqn_ref, wqr_ref, wkn_ref, wkr_ref, q_ref, k_ref, v_ref):
    p = p_ref[0]
    cos_t = cos_ref[...]
    sin_t = sin_ref[...]

    def rms(x, n):
        return x * lax.rsqrt(jnp.sum(x * x, axis=-1, keepdims=True) * (1.0 / n) + EPS)

    def rope(x):
        return x * cos_t + _rope_swap(x) * sin_t

    qd = p[:, :MLA_Q_LORA]
    kvd = p[:, MLA_Q_LORA:MLA_Q_LORA + MLA_KV_LORA]
    sm = p[:, MLA_Q_LORA + MLA_KV_LORA:]
    q = _mm(rms(qd, MLA_Q_LORA) * qnw_ref[...], qup_ref[...])
    kv = _mm(rms(kvd, MLA_KV_LORA) * kvnw_ref[...], kvup_ref[...])
    lane = lax.broadcasted_iota(jnp.int32, sm.shape, 1)
    kr = jnp.where(lane < MLA_ROPE, sm, 0.0)
    kr = rope(rms(kr, MLA_ROPE) * wkr_ref[...])
    for h in range(MLA_HEADS):
        base = h * 256
        q_ref[0, :, base:base + 128] = (rms(q[:, base:base + 128], MLA_NOPE) * wqn_ref[...]).astype(q_ref.dtype)
        qr = rms(q[:, base + 128:base + 256], MLA_ROPE) * wqr_ref[...]
        q_ref[0, :, base + 128:base + 256] = rope(qr).astype(q_ref.dtype)
        kn = kv[:, h * 128:(h + 1) * 128]
        k_ref[0, :, base:base + 128] = (rms(kn, MLA_NOPE) * wkn_ref[...]).astype(k_ref.dtype)
        k_ref[0, :, base + 128:base + 256] = kr.astype(k_ref.dtype)
    v_ref[0] = kv[:, MLA_HEADS * MLA_NOPE:].astype(v_ref.dtype)


def _mla_prep_call(p_mla, cos_t, sin_t, qnw, qup, kvnw, kvup, wqn, wqr, wkn, wkr):
    b, l, gw = p_mla.shape
    tm = _pick(l, (256, 128))
    hq = MLA_HEADS * 256
    full = lambda shape: pl.BlockSpec(shape, lambda i, t: tuple(0 for _ in shape))
    return pl.pallas_call(
        _mla_prep_kernel,
        grid=(b, l // tm),
        in_specs=[pl.BlockSpec((1, tm, gw), lambda i, t: (i, t, 0)),
                  pl.BlockSpec((tm, 128), lambda i, t: (t, 0)),
                  pl.BlockSpec((tm, 128), lambda i, t: (t, 0)),
                  full((1, MLA_Q_LORA)), full((MLA_Q_LORA, hq)),
                  full((1, MLA_KV_LORA)), full((MLA_KV_LORA, hq)),
                  full((1, 128)), full((1, 128)), full((1, 128)), full((1, 128))],
        out_specs=[pl.BlockSpec((1, tm, hq), lambda i, t: (i, t, 0)),
                   pl.BlockSpec((1, tm, hq), lambda i, t: (i, t, 0)),
                   pl.BlockSpec((1, tm, MLA_HEADS * MLA_DV), lambda i, t: (i, t, 0))],
        out_shape=[jax.ShapeDtypeStruct((b, l, hq), BF16),
                   jax.ShapeDtypeStruct((b, l, hq), BF16),
                   jax.ShapeDtypeStruct((b, l, MLA_HEADS * MLA_DV), BF16)],
        compiler_params=_params(("parallel", "parallel")),
        name="mla_prep",
    )(p_mla, cos_t, sin_t, qnw, qup, kvnw, kvup, wqn, wqr, wkn, wkr)


def _mla_attn_kernel(q_ref, k_ref, v_ref, o_ref, *, lc, tq, q_off):
    qi = pl.program_id(2) + q_off
    q = q_ref[0]

    def attend(nk):
        s = lax.dot_general(q, k_ref[0, :nk, :], (((1,), (1,)), ((), ())),
                            preferred_element_type=F32) * MLA_SCALE
        m = jnp.max(s, axis=-1, keepdims=True)
        e = jnp.exp(s - m)
        den = jnp.sum(e, axis=-1, keepdims=True)
        o = jnp.dot(e.astype(BF16), v_ref[0, :nk, :], preferred_element_type=F32)
        o_ref[0] = (o / den).astype(o_ref.dtype)

    if q_off == 0:
        @pl.when(qi * tq < lc)
        def _():
            attend(lc)

        @pl.when(qi * tq >= lc)
        def _():
            attend(k_ref.shape[1])
    else:
        attend(k_ref.shape[1])


def _mla_attn_call(q, k, v, lc, with_ctx):
    b, l, _ = q.shape
    tq = _pick(lc, (256, 128))
    q_off = 0 if with_ctx else lc // tq
    nq = l // tq - q_off
    return pl.pallas_call(
        functools.partial(_mla_attn_kernel, lc=lc, tq=tq, q_off=q_off),
        grid=(b, MLA_HEADS, nq),
        in_specs=[pl.BlockSpec((1, tq, 256), lambda i, h, t: (i, t + q_off, h)),
                  pl.BlockSpec((1, l, 256), lambda i, h, t: (i, 0, h)),
                  pl.BlockSpec((1, l, MLA_DV), lambda i, h, t: (i, 0, h))],
        out_specs=pl.BlockSpec((1, tq, MLA_DV), lambda i, h, t: (i, t, h)),
        out_shape=jax.ShapeDtypeStruct((b, nq * tq, MLA_HEADS * MLA_DV), F32),
        compiler_params=_params(("parallel", "parallel", "arbitrary")),
        name="mla_attn",
    )(q, k, v)


def _merge_kernel(h_ref, oa_ref, ob_ref, oc_ref, za_ref, zb_ref, zc_ref, g0_ref, g1_ref, g2_ref,
                  modx_ref, modc_ref, wa_ref, wb_ref, wbr_ref, wout_ref, o_ref, *, lc, tm, t_off):
    t = pl.program_id(1) + t_off

    def head_norm(o, width):
        parts = []
        for h in range(BRANCH_W // width):
            x = o[:, h * width:(h + 1) * width]
            parts.append(x * lax.rsqrt(jnp.mean(x * x, axis=-1, keepdims=True) + EPS))
        return jnp.concatenate(parts, axis=-1)

    ya = head_norm(oa_ref[0], GLA_DV) * wa_ref[...] * _silu(za_ref[0].astype(F32))
    yb = head_norm(ob_ref[0], GDN_DV) * wb_ref[...] * _silu(zb_ref[0].astype(F32))
    yc = oc_ref[0] * _silu(zc_ref[0].astype(F32))
    acc = jax.nn.sigmoid(g0_ref[0].astype(F32)) * _mm(ya, wbr_ref[0])
    acc = acc + jax.nn.sigmoid(g1_ref[0].astype(F32)) * _mm(yb, wbr_ref[1])
    acc = acc + jax.nn.sigmoid(g2_ref[0].astype(F32)) * _mm(yc, wbr_ref[2])
    d = D_MODEL
    gate = _row_select(t, tm, lc, modc_ref[:, 2 * d:], modx_ref[0][:, 2 * d:])
    o_ref[0] = h_ref[0] + gate * _mm(acc, wout_ref[...])


def _merge_call(h, oa, ob, oc, p_zg, modx, modc, wa, wb, w_branch, w_out, lc, with_ctx):
    b, l, d = h.shape
    tm = _pick(lc, (256, 128))
    t_off = 0 if with_ctx else lc // tm
    nt = l // tm - t_off
    oc_off = 0 if oc.shape[1] == l else lc // tm
    row = lambda i, t: (i, t + t_off, 0)
    zg = lambda n: pl.BlockSpec((1, tm, d), lambda i, t: (i, t + t_off, n))
    return pl.pallas_call(
        functools.partial(_merge_kernel, lc=lc, tm=tm, t_off=t_off),
        grid=(b, nt),
        in_specs=[pl.BlockSpec((1, tm, d), row), pl.BlockSpec((1, tm, d), row), pl.BlockSpec((1, tm, d), row),
                  pl.BlockSpec((1, tm, d), lambda i, t: (i, t + t_off - oc_off, 0)),
                  zg(0), zg(1), zg(2), zg(3), zg(4), zg(5),
                  pl.BlockSpec((1, 1, 3 * d), lambda i, t: (i, 0, 0)),
                  pl.BlockSpec((1, 3 * d), lambda i, t: (0, 0)),
                  pl.BlockSpec((1, d), lambda i, t: (0, 0)),
                  pl.BlockSpec((1, d), lambda i, t: (0, 0)),
                  pl.BlockSpec((3, d, d), lambda i, t: (0, 0, 0)),
                  pl.BlockSpec((d, d), lambda i, t: (0, 0))],
        out_specs=pl.BlockSpec((1, tm, d), lambda i, t: (i, t, 0)),
        out_shape=jax.ShapeDtypeStruct((b, nt * tm, d), F32),
        compiler_params=_params(("parallel", "parallel")),
        name="merge",
    )(h, oa, ob, oc, p_zg, p_zg, p_zg, p_zg, p_zg, p_zg, modx, modc, wa, wb, w_branch, w_out)


def _cols(w, *names):
    return jnp.concatenate([w[:, _IN_OFF[n]:_IN_OFF[n] + _IN_LEN[n]] for n in names], axis=1)


def _rope_tables(t_lat, lc):
    rows = t_lat // GRID_W
    row = jnp.repeat(jnp.arange(rows, dtype=F32), GRID_W)
    col = jnp.tile(jnp.arange(GRID_W, dtype=F32), rows)
    inv_freq = jnp.power(ROPE_BASE, -jnp.arange(ROPE_FREQ, dtype=F32) / ROPE_FREQ)
    ar = row[:, None] * inv_freq
    ac = col[:, None] * inv_freq
    zeros = jnp.zeros((t_lat, 128 - MLA_ROPE), F32)
    cos_t = jnp.concatenate([jnp.cos(ar), jnp.cos(ar), jnp.cos(ac), jnp.cos(ac), zeros], axis=1)
    sin_t = jnp.concatenate([-jnp.sin(ar), jnp.sin(ar), -jnp.sin(ac), jnp.sin(ac), zeros], axis=1)
    cos_c = jnp.concatenate([jnp.ones((lc, MLA_ROPE), F32), jnp.zeros((lc, 128 - MLA_ROPE), F32)], axis=1)
    return (jnp.concatenate([cos_c, cos_t], axis=0),
            jnp.concatenate([jnp.zeros((lc, 128), F32), sin_t], axis=0))


def _pad_lanes(w, n=128):
    return jnp.pad(w, (0, n - w.shape[0])).reshape(1, n)


def kernel(x, c, ctx, c_ctx, norm_w, ada_w, ada_b, w_in, gla_gate_w2, gla_gate_b, gla_norm_w, gdn_conv_w,
           gdn_a_log, gdn_dt_bias, gdn_norm_w, mla_q_norm_w, mla_q_up, mla_kv_norm_w, mla_kv_up,
           mla_qn_nope, mla_qn_rope, mla_kn_nope, mla_kn_rope, w_branch, w_out):
    b, t_lat, d = x.shape
    lc = ctx.shape[1]
    l = lc + t_lat
    depth = w_in.shape[0]
    assert d == D_MODEL and lc % 128 == 0 and t_lat % 128 == 0

    h = jnp.concatenate([ctx, x], axis=1)
    rows = ((b + 1 + 7) // 8) * 8
    c_all = jnp.concatenate([c, c_ctx[None, :], jnp.zeros((rows - b - 1, d), F32)], axis=0)
    mod = _ada_call(c_all, ada_w, ada_b)
    cos_t, sin_t = _rope_tables(t_lat, lc)

    for li in range(depth):
        last = li == depth - 1
        modx = mod[li, :b].reshape(b, 1, 3 * d)
        modc = mod[li, b:b + 1]
        w = w_in[li]
        w_mla = _cols(w, 'mla_q_down', 'mla_kv_down', 'mla_k_rope', 'gla_gate_lr', 'gdn_a', 'gdn_b').astype(BF16)
        w_gla = _cols(w, 'gla_q', 'gla_k', 'gla_v').astype(BF16)
        w_gdn = _cols(w, 'gdn_qkv').astype(BF16)
        w_zg = _cols(w, 'gla_z', 'gdn_z', 'mla_z', 'merge_gate').astype(BF16)

        hn = _prenorm_call(h, modx, modc, norm_w[li], lc).reshape(b * l, d)
        p_mla = _proj_call(hn, w_mla, F32, "proj_mla").reshape(b, l, -1)
        p_gla = _proj_call(hn, w_gla, F32, "proj_gla").reshape(b, l, -1)
        p_gdn = _proj_call(hn, w_gdn, F32, "proj_gdn").reshape(b, l, -1)
        p_zg = _proj_call(hn, w_zg, F32, "proj_zg").reshape(b, l, -1)

        w2e = jnp.zeros((2, 128, GLA_HEADS * GLA_DK), F32).at[0, SM_LR:SM_LR + GLA_RANK].set(
            gla_gate_w2[li, 0]).at[1, SM_LR + GLA_RANK:SM_LR + 2 * GLA_RANK].set(gla_gate_w2[li, 1])
        oa = _gla_call(p_gla, p_mla, w2e, gla_gate_b[li].reshape(2, 1, -1), lc)

        qkvn = _gdn_prep_call(p_gdn, gdn_conv_w[li], lc)
        abc, abr = _gdn_gate_layouts(p_mla, lc)
        ob = _gdn_call(qkvn, abc, abr, gdn_a_log[li], gdn_dt_bias[li], lc)

        qup = mla_q_up[li].reshape(MLA_Q_LORA, MLA_HEADS, MLA_NOPE + MLA_ROPE)
        qup = jnp.pad(qup, ((0, 0), (0, 0), (0, 256 - MLA_NOPE - MLA_ROPE))).reshape(MLA_Q_LORA, -1).astype(BF16)
        kvup = mla_kv_up[li].reshape(MLA_KV_LORA, MLA_HEADS, MLA_NOPE + MLA_DV)
        kvup = jnp.concatenate([kvup[:, :, :MLA_NOPE].reshape(MLA_KV_LORA, -1),
                                kvup[:, :, MLA_NOPE:].reshape(MLA_KV_LORA, -1)], axis=1).astype(BF16)
        qm, km, vm = _mla_prep_call(
            p_mla, cos_t, sin_t, mla_q_norm_w[li].reshape(1, -1), qup, mla_kv_norm_w[li].reshape(1, -1), kvup,
            mla_qn_nope[li].reshape(1, -1), _pad_lanes(mla_qn_rope[li]),
            mla_kn_nope[li].reshape(1, -1), _pad_lanes(mla_kn_rope[li]))
        oc = _mla_attn_call(qm, km, vm, lc, with_ctx=not last)

        h = _merge_call(h, oa, ob, oc, p_zg, modx, modc,
                        jnp.tile(gla_norm_w[li], GLA_HEADS).reshape(1, -1),
                        jnp.tile(gdn_norm_w[li], GDN_HEADS).reshape(1, -1),
                        w_branch[li].astype(BF16), w_out[li].astype(BF16), lc, with_ctx=not last)
    return h
```

```python
import functools
import math

import numpy as np
import jax
import jax.numpy as jnp
from jax import lax
from jax.experimental import pallas as pl
from jax.experimental.pallas import tpu as pltpu

F32 = jnp.float32
BF16 = jnp.bfloat16

D_MODEL = 1024
EPS = 1e-6
GRID_W = 64
CHUNK = 64

GLA_HEADS, GLA_DK, GLA_DV, GLA_RANK, GLA_NORMALIZER = 4, 128, 256, 16, 16.0
GDN_HEADS, GDN_DK, GDN_DV, GDN_CONV = 8, 128, 128, 5
GLA_HB = 2
GDN_HB = 4
MLA_HEADS, MLA_Q_LORA, MLA_KV_LORA, MLA_NOPE, MLA_ROPE, MLA_DV = 8, 384, 256, 128, 64, 128
MLA_SCALE = (MLA_NOPE + MLA_ROPE) ** -0.5
MLA_Q_PRESCALE = MLA_SCALE * math.log2(math.e)
MLA_HP = 2
ROPE_FREQ = MLA_ROPE // 4
ROPE_BASE = 10000.0
BRANCH_W = 1024

_IN_SIZES = (512, 512, 1024, 32, 1024, 3072, 16, 16, 1024, 384, 256, 64, 1024, 3072)
_IN_NAMES = ('gla_q', 'gla_k', 'gla_v', 'gla_gate_lr', 'gla_z', 'gdn_qkv', 'gdn_a', 'gdn_b', 'gdn_z',
             'mla_q_down', 'mla_kv_down', 'mla_k_rope', 'mla_z', 'merge_gate')
_IN_OFF = dict(zip(_IN_NAMES, np.concatenate([[0], np.cumsum(_IN_SIZES)[:-1]]).tolist()))
_IN_LEN = dict(zip(_IN_NAMES, _IN_SIZES))

SM_ROPE, SM_LR, SM_A, SM_B = 0, 64, 96, 112
MLA_GROUP_W = MLA_Q_LORA + MLA_KV_LORA + 128
SMALL_BLOCK = (MLA_Q_LORA + MLA_KV_LORA) // 128

VMEM_LIMIT = 48 * 1024 * 1024


def _params(sem):
    return pltpu.CompilerParams(dimension_semantics=sem, vmem_limit_bytes=VMEM_LIMIT)


def _pick(n, cands):
    for c in cands:
        if n % c == 0:
            return c
    raise ValueError(f"no tile for {n} in {cands}")


def _mm(a, b):
    return jnp.dot(a.astype(BF16), b.astype(BF16), preferred_element_type=F32)


def _mm_nt(a, b):
    return lax.dot_general(a.astype(BF16), b.astype(BF16), (((1,), (1,)), ((), ())),
                           preferred_element_type=F32)


def _mm_tn(a, b):
    return lax.dot_general(a.astype(BF16), b.astype(BF16), (((0,), (0,)), ((), ())),
                           preferred_element_type=F32)


def _mm_hp(a, b):
    return jnp.dot(a, b, precision=lax.Precision.HIGHEST, preferred_element_type=F32)


def _split(a):
    hi = a.astype(BF16)
    lo = (a - hi.astype(F32)).astype(BF16)
    return hi, lo


def _mm3(a, b):
    ah, al = _split(a)
    bh, bl = _split(b)
    d = functools.partial(jnp.dot, preferred_element_type=F32)
    return d(ah, bh) + (d(ah, bl) + d(al, bh))


_MMX = _mm
_TRI_BLOCK = 8


def _split3(a):
    hi = a.astype(BF16)
    r1 = a - hi.astype(F32)
    mid = r1.astype(BF16)
    lo = (r1 - mid.astype(F32)).astype(BF16)
    return hi, mid, lo


def _cum_rows(cum, x):
    cb = cum.astype(BF16)
    d = functools.partial(jnp.dot, preferred_element_type=F32)
    hi, mid, lo = _split3(x)
    return d(cb, hi) + (d(cb, mid) + d(cb, lo))


def _cum_cols(x, cum):
    cb = cum.astype(BF16)
    d = functools.partial(jnp.dot, preferred_element_type=F32)
    hi, mid, lo = _split3(x)
    return d(hi, cb) + (d(mid, cb) + d(lo, cb))


def _softplus(x):
    return jnp.maximum(x, 0.0) + jnp.log1p(jnp.exp(-jnp.abs(x)))


def _log_sigmoid(x):
    return jnp.minimum(x, 0.0) - jnp.log1p(jnp.exp(-jnp.abs(x)))


def _silu(x):
    return x * jax.nn.sigmoid(x)


def _tri_masks(n, reverse):
    r = lax.broadcasted_iota(jnp.int32, (n, n), 0)
    c = lax.broadcasted_iota(jnp.int32, (n, n), 1)
    incl = (c >= r) if reverse else (c <= r)
    strict = (c > r) if reverse else (c < r)
    return r, c, incl, strict


def _ada_kernel(c_ref, w_ref, b_ref, o_ref):
    o_ref[0] = _mm_hp(_silu(c_ref[...]), w_ref[0]) + b_ref[0]


def _ada_call(c_all, ada_w, ada_b):
    nl, d, n3 = ada_w.shape
    r = c_all.shape[0]
    tn = 1024
    return pl.pallas_call(
        _ada_kernel,
        grid=(nl, n3 // tn),
        in_specs=[pl.BlockSpec((r, d), lambda l, j: (0, 0)),
                  pl.BlockSpec((1, d, tn), lambda l, j: (l, 0, j)),
                  pl.BlockSpec((1, 1, tn), lambda l, j: (l, 0, j))],
        out_specs=pl.BlockSpec((1, r, tn), lambda l, j: (l, 0, j)),
        out_shape=jax.ShapeDtypeStruct((nl, r, n3), F32),
        compiler_params=_params(("arbitrary", "arbitrary")),
        name="ada_rows",
    )(c_all, ada_w, ada_b.reshape(nl, 1, n3))


def _row_select(t, tm, lc, ctx_row, lat_row):
    rows = t * tm + lax.broadcasted_iota(jnp.int32, (tm, 1), 0)
    return jnp.where(rows < lc, ctx_row, lat_row)


def _prenorm_kernel(h_ref, modx_ref, modc_ref, nw_ref, o_ref, *, lc, tm):
    t = pl.program_id(1)
    x = h_ref[0]
    y = x * lax.rsqrt(jnp.mean(x * x, axis=-1, keepdims=True) + EPS) * nw_ref[...]
    mx = modx_ref[0]
    mc = modc_ref[...]
    d = D_MODEL
    shift = _row_select(t, tm, lc, mc[:, :d], mx[:, :d])
    scale = _row_select(t, tm, lc, mc[:, d:2 * d], mx[:, d:2 * d])
    o_ref[0] = (y * (1.0 + scale) + shift).astype(o_ref.dtype)


def _prenorm_call(h, modx, modc, norm_w, lc):
    b, l, d = h.shape
    tm = _pick(l, (768, 512, 384, 256, 128))
    return pl.pallas_call(
        functools.partial(_prenorm_kernel, lc=lc, tm=tm),
        grid=(b, l // tm),
        in_specs=[pl.BlockSpec((1, tm, d), lambda i, t: (i, t, 0)),
                  pl.BlockSpec((1, 1, 3 * d), lambda i, t: (i, 0, 0)),
                  pl.BlockSpec((1, 3 * d), lambda i, t: (0, 0)),
                  pl.BlockSpec((1, d), lambda i, t: (0, 0))],
        out_specs=pl.BlockSpec((1, tm, d), lambda i, t: (i, t, 0)),
        out_shape=jax.ShapeDtypeStruct((b, l, d), BF16),
        compiler_params=_params(("parallel", "parallel")),
        name="prenorm",
    )(h, modx, modc, norm_w.reshape(1, d))


def _proj_kernel(x_ref, w_ref, o_ref):
    o_ref[...] = jnp.dot(x_ref[...], w_ref[...], preferred_element_type=F32).astype(o_ref.dtype)


def _proj_call(x, w, out_dtype, name):
    m, k = x.shape
    n = w.shape[1]
    tm = _pick(m, (1024, 768, 512, 384, 256, 128))
    tn = _pick(n, (1536, 1024, 768, 512, 256, 128))
    return pl.pallas_call(
        _proj_kernel,
        grid=(m // tm, n // tn),
        in_specs=[pl.BlockSpec((tm, k), lambda i, j: (i, 0)),
                  pl.BlockSpec((k, tn), lambda i, j: (0, j))],
        out_specs=pl.BlockSpec((tm, tn), lambda i, j: (i, j)),
        out_shape=jax.ShapeDtypeStruct((m, n), out_dtype),
        compiler_params=_params(("parallel", "parallel")),
        name=name,
    )(x, w)


def _bwd_chunk(i, ncc, nc):
    return jnp.where(i < ncc, ncc - 1 - i, ncc + nc - 1 - i)


def _chunk_rows(c):
    return pl.ds(pl.multiple_of(c * CHUNK, CHUNK), CHUNK)


def _gla_kernel(q_ref, k_ref, v_ref, sm_ref, w2_ref, gb_ref, o_ref, la_ref, st_ref, *, ncc, nc):
    sm = sm_ref[0]
    for d in range(2):
        logit = _mm3(sm, w2_ref[d]) + gb_ref[d]
        la_ref[d] = _log_sigmoid(logit) * (1.0 / GLA_NORMALIZER)
    st_ref[...] = jnp.zeros_like(st_ref)
    o_ref[...] = jnp.zeros_like(o_ref)
    masks = [_tri_masks(CHUNK, rev) for rev in (False, True)]
    cum_mats = [m[2].astype(F32) for m in masks]
    probs = [(d, hh) for d in range(2) for hh in range(GLA_HB)]

    def body(i, carry):
        rows = [_chunk_rows(i), _chunk_rows(_bwd_chunk(i, ncc, nc))]
        bcum_d = [_cum_rows(cum_mats[d], la_ref[d, rows[d], :]) for d in range(2)]
        kl = [slice(hh * GLA_DK, (hh + 1) * GLA_DK) for _, hh in probs]
        vl = [slice(hh * GLA_DV, (hh + 1) * GLA_DV) for _, hh in probs]
        bcum = [bcum_d[d][:, ln] for (d, _), ln in zip(probs, kl)]
        b_last = [bc[CHUNK - 1:CHUNK] if d == 0 else bc[0:1] for (d, _), bc in zip(probs, bcum)]
        q = [q_ref[0, rows[d], ln].astype(F32) * GLA_DK ** -0.5 for (d, _), ln in zip(probs, kl)]
        k = [k_ref[0, rows[d], ln].astype(F32) for (d, _), ln in zip(probs, kl)]
        v = [v_ref[0, rows[d], ln].astype(F32) for (d, _), ln in zip(probs, vl)]
        q_dec = [a * jnp.exp(bc) for a, bc in zip(q, bcum)]
        k_inv = [a * jnp.exp(-bc) for a, bc in zip(k, bcum)]
        k_end = [a * jnp.exp(bl - bc) for a, bl, bc in zip(k, b_last, bcum)]
        st = [st_ref[j] for j in range(len(probs))]
        qk = _each(_mm_nt, q_dec, k_inv)
        o_inter = _each(_mm_nt, q_dec, st)
        upd = _each(_mm_tn, v, k_end)
        att = [jnp.where(masks[d][2], a, 0.0) for (d, _), a in zip(probs, qk)]
        o_intra = _each(_mm, att, v)
        for j, (sx, bl, up) in enumerate(zip(st, b_last, upd)):
            st_ref[j] = sx * jnp.exp(bl) + up
        for (d, _), ln, a, b in zip(probs, vl, o_intra, o_inter):
            o_ref[0, rows[d], ln] += a + b
        return carry

    lax.fori_loop(0, nc, body, 0)


def _gla_call(p_gla, p_mla, w2e, gate_b, lc):
    b, l, _ = p_gla.shape
    nc, ncc = l // CHUNK, lc // CHUNK
    nhb = GLA_HEADS // GLA_HB
    wk, wv = GLA_HB * GLA_DK, GLA_HB * GLA_DV
    return pl.pallas_call(
        functools.partial(_gla_kernel, ncc=ncc, nc=nc),
        grid=(b, nhb),
        in_specs=[pl.BlockSpec((1, l, wk), lambda i, h: (i, 0, h)),
                  pl.BlockSpec((1, l, wk), lambda i, h: (i, 0, nhb + h)),
                  pl.BlockSpec((1, l, wv), lambda i, h: (i, 0, nhb + h)),
                  pl.BlockSpec((1, l, 128), lambda i, h: (i, 0, SMALL_BLOCK)),
                  pl.BlockSpec((2, 128, wk), lambda i, h: (0, 0, h)),
                  pl.BlockSpec((2, 1, wk), lambda i, h: (0, 0, h))],
        out_specs=pl.BlockSpec((1, l, wv), lambda i, h: (i, 0, h)),
        out_shape=jax.ShapeDtypeStruct((b, l, GLA_HEADS * GLA_DV), F32),
        scratch_shapes=[pltpu.VMEM((2, l, wk), F32), pltpu.VMEM((2 * GLA_HB, GLA_DV, GLA_DK), F32)],
        compiler_params=_params(("parallel", "parallel")),
        name="gla_scan",
    )(p_gla, p_gla, p_gla, p_mla, w2e, gate_b)


def _gdn_prep_kernel(x_ref, w_ref, o_ref, *, lc, l):
    j = pl.program_id(1)
    x = x_ref[0].astype(F32)
    w = w_ref[...]
    t = lax.broadcasted_iota(jnp.int32, (l, 128), 0)
    start = jnp.where(t < lc, 0, lc)
    end = jnp.where(t < lc, lc, l)
    half = GDN_CONV // 2
    acc = x * w[half:half + 1]
    for s in range(-half, half + 1):
        if s == 0:
            continue
        xs = pltpu.roll(x, (-s) % l, axis=0)
        ok = (t + s >= start) if s < 0 else (t + s < end)
        acc = acc + jnp.where(ok, xs, 0.0) * w[s + half:s + half + 1]
    y = _silu(acc)
    nblk_head = GDN_HEADS * GDN_DK // 128
    inv = lax.rsqrt(jnp.sum(y * y, axis=-1, keepdims=True) + EPS)
    fac = jnp.where(j < nblk_head, inv * GDN_DK ** -0.5, jnp.where(j < 2 * nblk_head, inv, 1.0))
    o_ref[0] = (y * fac).astype(o_ref.dtype)


def _gdn_prep_call(p_gdn, conv_w, lc):
    b, l, n = p_gdn.shape
    return pl.pallas_call(
        functools.partial(_gdn_prep_kernel, lc=lc, l=l),
        grid=(b, n // 128),
        in_specs=[pl.BlockSpec((1, l, 128), lambda i, j: (i, 0, j)),
                  pl.BlockSpec((GDN_CONV, 128), lambda i, j: (0, j))],
        out_specs=pl.BlockSpec((1, l, 128), lambda i, j: (i, 0, j)),
        out_shape=jax.ShapeDtypeStruct((b, l, n), F32),
        compiler_params=_params(("parallel", "parallel")),
        name="gdn_prep",
    )(p_gdn, conv_w)


def _each(fn, *lists):
    return [fn(*args) for args in zip(*lists)]


def _unit_tri_solves(lms, rhs, r, c):
    eye = (r == c).astype(F32)
    same = lambda s: (r // s) == (c // s)
    nb = _TRI_BLOCK
    y = [-jnp.where(same(nb), lm, 0.0) for lm in lms]
    t = [eye + a for a in y]
    p = y
    for _ in range(int(math.log2(nb)) - 1):
        p = _each(_MMX, p, p)
        t = _each(lambda a, b: a + _MMX(a, b), t, p)
    sizes = [nb * 2 ** j for j in range(int(math.log2(CHUNK // nb)))]
    offs = [[jnp.where(same(2 * s), jnp.where(same(s), 0.0, lm), 0.0) for lm in lms] for s in sizes]
    a = _each(_mm, t, rhs)
    bs = [_each(_mm, t, cj) for cj in offs]
    for j in range(len(bs)):
        a = _each(lambda x, m: x - _mm(m, x), a, bs[j])
        for kk in range(j + 1, len(bs)):
            bs[kk] = _each(lambda x, m: x - _mm(m, x), bs[kk], bs[j])
    return a


def _gdn_kernel(rate_ref, ratel_ref, q_ref, k_ref, v_ref, abc_ref, abr_ref, o_ref, s_ref, *, ncc, nc):
    na = 2 * GDN_HB
    s_ref[...] = jnp.zeros_like(s_ref)
    o_ref[...] = jnp.zeros_like(o_ref)
    masks = [_tri_masks(CHUNK, rev) for rev in (False, True)]
    cum_mats = [m[2].astype(F32) for m in masks]
    r, cc = masks[0][0], masks[0][1]
    probs = [(d, hh) for d in range(2) for hh in range(GDN_HB)]
    neg_rate_rows = -jnp.exp(rate_ref[0, 0])[:, :CHUNK]
    dtb_rows = rate_ref[0, 1][:, :CHUNK]
    neg_rate_lanes = -jnp.exp(ratel_ref[0, 0, 0:1, :])
    dtb_lanes = ratel_ref[0, 1, 0:1, :]

    def body(i, carry):
        dirs = (0, 1)
        chunk = [i, _bwd_chunk(i, ncc, nc)]
        rows = [_chunk_rows(c) for c in chunk]
        abc = [abc_ref[0, 0, rw, :] for rw in rows]
        abr = [abr_ref[0, 0, c] for c in chunk]
        incl = [masks[d][2] for d, _ in probs]
        strict = [masks[d][3] for d, _ in probs]
        g_cols = [neg_rate_lanes * _softplus(abc[d] + dtb_lanes) for d in dirs]
        g_rows = [neg_rate_rows * _softplus(abr[d] + dtb_rows) for d in dirs]
        gc_cols = [_cum_rows(cum_mats[d], g_cols[d]) for d in dirs]
        gc_rows = [_cum_cols(g_rows[d], cum_mats[1 - d]) for d in dirs]
        beta_cols = [jax.nn.sigmoid(abc[d]) for d in dirs]
        col = [d * GDN_HB + hh for d, hh in probs]
        gc_col = [jnp.broadcast_to(gc_cols[d][:, j:j + 1], (CHUNK, 128)) for (d, _), j in zip(probs, col)]
        gc_row = [gc_rows[d][j:j + 1, :] for (d, _), j in zip(probs, col)]
        beta_col = [beta_cols[d][:, na + j:na + j + 1] for (d, _), j in zip(probs, col)]
        g_last =[gc[CHUNK - 1:CHUNK] if d == 0 else gc[0:1] for (d, _), gc in zip(probs, gc_col)]
        decay = [jnp.where(m, jnp.exp(jnp.where(m, a[:, :CHUNK] - b, 0.0)), 0.0)
                 for m, a, b in zip(incl, gc_col, gc_row)]

        lanes = [slice(hh * GDN_DK, (hh + 1) * GDN_DK) for _, hh in probs]
        q = [q_ref[0, rows[d], ln].astype(F32) for (d, _), ln in zip(probs, lanes)]
        k = [k_ref[0, rows[d], ln].astype(F32) for (d, _), ln in zip(probs, lanes)]
        v = [v_ref[0, rows[d], ln].astype(F32) for (d, _), ln in zip(probs, lanes)]
        k_beta = _each(jnp.multiply, k, beta_col)
        v_beta = _each(jnp.multiply, v, beta_col)
        e_col = _each(jnp.exp, gc_col)
        kk = _each(_mm_nt, k_beta, k)
        qk = _each(_mm_nt, q, k)
        lm = [jnp.where(m, a * dc, 0.0) for m, a, dc in zip(strict, kk, decay)]
        rhs = [jnp.concatenate([vb, kb * e], axis=1) for vb, kb, e in zip(v_beta, k_beta, e_col)]
        uw = _unit_tri_solves(lm, rhs, r, cc)
        u = [x[:, :GDN_DV] for x in uw]
        w = [x[:, GDN_DV:] for x in uw]
        att =[jnp.where(m, a * dc, 0.0) for m, a, dc in zip(incl, qk, decay)]
        q_dec = _each(jnp.multiply, q, e_col)
        k_end = [kx * jnp.exp(gl - gc) for kx, gl, gc in zip(k, g_last, gc_col)]
        s = [s_ref[d * GDN_HB + hh] for d, hh in probs]
        ws = _each(_mm, w, s)
        o_inter = _each(_mm, q_dec, s)
        v_new = _each(jnp.subtract, u, ws)
        upd = _each(_mm_tn, k_end, v_new)
        o_intra = _each(_mm, att, v_new)
        for (d, hh), sx, gl, up in zip(probs, s, g_last, upd):
            s_ref[d * GDN_HB + hh] = sx * jnp.exp(gl) + up
        for (d, _), ln, a, b in zip(probs, lanes, o_intra, o_inter):
            o_ref[0, rows[d], ln] += a + b
        return carry

    lax.fori_loop(0, nc, body, 0)


def _gdn_call(qkvn, abc, abr, a_log, dt_bias, lc):
    b, l, _ = qkvn.shape
    nc, ncc = l // CHUNK, lc // CHUNK
    nhb = GDN_HEADS // GDN_HB
    wb = GDN_HB * GDN_DK
    na = 2 * GDN_HB
    rate = jnp.stack([a_log, dt_bias]).reshape(2, 2, nhb, GDN_HB).transpose(2, 0, 1, 3).reshape(nhb, 2, na)
    rate = jnp.pad(rate, ((0, 0), (0, 0), (0, 128 - na)))
    rate_rows = jnp.broadcast_to(rate[:, :, :8, None], (nhb, 2, 8, 128))
    rate_lanes = jnp.broadcast_to(rate[:, :, None, :], (nhb, 2, 8, 128))
    return pl.pallas_call(
        functools.partial(_gdn_kernel, ncc=ncc, nc=nc),
        grid=(b, nhb),
        in_specs=[pl.BlockSpec((1, 2, 8, 128), lambda i, h: (h, 0, 0, 0)),
                  pl.BlockSpec((1, 2, 8, 128), lambda i, h: (h, 0, 0, 0)),
                  pl.BlockSpec((1, l, wb), lambda i, h: (i, 0, h)),
                  pl.BlockSpec((1, l, wb), lambda i, h: (i, 0, nhb + h)),
                  pl.BlockSpec((1, l, wb), lambda i, h: (i, 0, 2 * nhb + h)),
                  pl.BlockSpec((1, 1, l, 128), lambda i, h: (i, h, 0, 0)),
                  pl.BlockSpec((1, 1, nc, 8, CHUNK), lambda i, h: (i, h, 0, 0, 0))],
        out_specs=pl.BlockSpec((1, l, wb), lambda i, h: (i, 0, h)),
        out_shape=jax.ShapeDtypeStruct((b, l, GDN_HEADS * GDN_DV), F32),
        scratch_shapes=[pltpu.VMEM((2 * GDN_HB, GDN_DK, GDN_DV), F32)],
        compiler_params=_params(("parallel", "parallel")),
        name="gdn_scan",
    )(rate_rows, rate_lanes, qkvn, qkvn, qkvn, abc, abr)


def _gdn_gate_layouts(p_mla, lc):
    b, l, _ = p_mla.shape
    nhb = GDN_HEADS // GDN_HB
    na = 2 * GDN_HB
    o = MLA_Q_LORA + MLA_KV_LORA
    ab = p_mla[:, :, o + SM_A:o + SM_A + 32].reshape(b, l, 2, 2, nhb, GDN_HB)
    ab = ab.transpose(0, 4, 1, 2, 3, 5).reshape(b, nhb, l, 2 * na)
    abr = ab[..., :na].reshape(b, nhb, l // CHUNK, CHUNK, na).transpose(0, 1, 2, 4, 3)
    abr = jnp.pad(abr, ((0, 0), (0, 0), (0, 0), (0, 8 - na), (0, 0)))
    abc = jnp.pad(ab, ((0, 0), (0, 0), (0, 0), (0, 128 - 2 * na)))
    return abc, abr


def _rope_swap(x):
    lane = lax.broadcasted_iota(jnp.int32, x.shape, 1)
    n = x.shape[1]
    return jnp.where(lane % 32 < 16, pltpu.roll(x, n - 16, axis=1), pltpu.roll(x, 16, axis=1))


def _mla_prep_kernel(p_ref, cos_ref, sin_ref, qnw_ref, qup_ref, kvnw_ref, kvup_ref,
                     wqn_ref, wqr_ref, wkn_ref, wkr_ref, q_ref, k_ref, v_ref):
    p = p_ref[0]
    cos_t = cos_ref[...]
    sin_t = sin_ref[...]

    def rms(x, n):
        return x * lax.rsqrt(jnp.sum(x * x, axis=-1, keepdims=True) * (1.0 / n) + EPS)

    def rope(x):
        return x * cos_t + _rope_swap(x) * sin_t

    qd = p[:, :MLA_Q_LORA]
    kvd = p[:, MLA_Q_LORA:MLA_Q_LORA + MLA_KV_LORA]
    sm = p[:, MLA_Q_LORA + MLA_KV_LORA:]
    q = _mm(rms(qd, MLA_Q_LORA) * qnw_ref[...], qup_ref[...])
    kv = _mm(rms(kvd, MLA_KV_LORA) * kvnw_ref[...], kvup_ref[...])
    lane = lax.broadcasted_iota(jnp.int32, sm.shape, 1)
    kr = jnp.where(lane < MLA_ROPE, sm, 0.0)
    kr = rope(rms(kr, MLA_ROPE) * wkr_ref[...])
    for h in range(MLA_HEADS):
        base = h * 256
        qn = rms(q[:, base:base + 128], MLA_NOPE) * wqn_ref[...]
        q_ref[0, :, base:base + 128] = (qn * MLA_Q_PRESCALE).astype(q_ref.dtype)
        qr = rms(q[:, base + 128:base + 256], MLA_ROPE) * wqr_ref[...]
        q_ref[0, :, base + 128:base + 256] = (rope(qr) * MLA_Q_PRESCALE).astype(q_ref.dtype)
        kn = kv[:, h * 128:(h + 1) * 128]
        k_ref[0, :, base:base + 128] = (rms(kn, MLA_NOPE) * wkn_ref[...]).astype(k_ref.dtype)
        k_ref[0, :, base + 128:base + 256] = kr.astype(k_ref.dtype)
    v_ref[0] = kv[:, MLA_HEADS * MLA_NOPE:].astype(v_ref.dtype)


def _mla_prep_call(p_mla, cos_t, sin_t, qnw, qup, kvnw, kvup, wqn, wqr, wkn, wkr):
    b, l, gw = p_mla.shape
    tm = _pick(l, (256, 128))
    hq = MLA_HEADS * 256
    full = lambda shape: pl.BlockSpec(shape, lambda i, t: tuple(0 for _ in shape))
    return pl.pallas_call(
        _mla_prep_kernel,
        grid=(b, l // tm),
        in_specs=[pl.BlockSpec((1, tm, gw), lambda i, t: (i, t, 0)),
                  pl.BlockSpec((tm, 128), lambda i, t: (t, 0)),
                  pl.BlockSpec((tm, 128), lambda i, t: (t, 0)),
                  full((1, MLA_Q_LORA)), full((MLA_Q_LORA, hq)),
                  full((1, MLA_KV_LORA)), full((MLA_KV_LORA, hq)),
                  full((1, 128)), full((1, 128)), full((1, 128)), full((1, 128))],
        out_specs=[pl.BlockSpec((1, tm, hq), lambda i, t: (i, t, 0)),
                   pl.BlockSpec((1, tm, hq), lambda i, t: (i, t, 0)),
                   pl.BlockSpec((1, tm, MLA_HEADS * MLA_DV), lambda i, t: (i, t, 0))],
        out_shape=[jax.ShapeDtypeStruct((b, l, hq), BF16),
                   jax.ShapeDtypeStruct((b, l, hq), BF16),
                   jax.ShapeDtypeStruct((b, l, MLA_HEADS * MLA_DV), BF16)],
        compiler_params=_params(("parallel", "parallel")),
        name="mla_prep",
    )(p_mla, cos_t, sin_t, qnw, qup, kvnw, kvup, wqn, wqr, wkn, wkr)


def _mla_attn_kernel(q_ref, k_ref, v_ref, o_ref, *, lc, tq, q_off):
    qi = pl.program_id(2) + q_off
    heads = range(MLA_HP)

    def attend(nk):
        s = [lax.dot_general(q_ref[0, :, h * 256:(h + 1) * 256], k_ref[0, :nk, h * 256:(h + 1) * 256],
                             (((1,), (1,)), ((), ())), preferred_element_type=F32) for h in heads]
        e = [jnp.exp2(x - jnp.max(x, axis=-1, keepdims=True)) for x in s]
        den = [jnp.sum(x, axis=-1, keepdims=True) for x in e]
        o = [jnp.dot(e[h].astype(BF16), v_ref[0, :nk, h * MLA_DV:(h + 1) * MLA_DV],
                     preferred_element_type=F32) for h in heads]
        for h in heads:
            o_ref[0, :, h * MLA_DV:(h + 1) * MLA_DV] = (o[h] / den[h]).astype(o_ref.dtype)

    if q_off == 0:
        @pl.when(qi * tq < lc)
        def _():
            attend(lc)

        @pl.when(qi * tq >= lc)
        def _():
            attend(k_ref.shape[1])
    else:
        attend(k_ref.shape[1])


def _mla_attn_call(q, k, v, lc, with_ctx):
    b, l, _ = q.shape
    tq = _pick(lc, (256, 128))
    q_off = 0 if with_ctx else lc // tq
    nq = l // tq - q_off
    return pl.pallas_call(
        functools.partial(_mla_attn_kernel, lc=lc, tq=tq, q_off=q_off),
        grid=(b, MLA_HEADS // MLA_HP, nq),
        in_specs=[pl.BlockSpec((1, tq, MLA_HP * 256), lambda i, h, t: (i, t + q_off, h)),
                  pl.BlockSpec((1, l, MLA_HP * 256), lambda i, h, t: (i, 0, h)),
                  pl.BlockSpec((1, l, MLA_HP * MLA_DV), lambda i, h, t: (i, 0, h))],
        out_specs=pl.BlockSpec((1, tq, MLA_HP * MLA_DV), lambda i, h, t: (i, t, h)),
        out_shape=jax.ShapeDtypeStruct((b, nq * tq, MLA_HEADS * MLA_DV), F32),
        compiler_params=_params(("parallel", "parallel", "arbitrary")),
        name="mla_attn",
    )(q, k, v)


def _merge_kernel(h_ref, oa_ref, ob_ref, oc_ref, za_ref, zb_ref, zc_ref, g0_ref, g1_ref, g2_ref,
                  modx_ref, modc_ref, wa_ref, wb_ref, wbr_ref, wout_ref, o_ref, *, lc, tm, t_off):
    t = pl.program_id(1) + t_off

    def head_norm(o, width):
        parts = []
        for h in range(BRANCH_W // width):
            x = o[:, h * width:(h + 1) * width]
            parts.append(x * lax.rsqrt(jnp.mean(x * x, axis=-1, keepdims=True) + EPS))
        return jnp.concatenate(parts, axis=-1)

    ya = head_norm(oa_ref[0], GLA_DV) * wa_ref[...] * _silu(za_ref[0].astype(F32))
    yb = head_norm(ob_ref[0], GDN_DV) * wb_ref[...] * _silu(zb_ref[0].astype(F32))
    yc = oc_ref[0] * _silu(zc_ref[0].astype(F32))
    acc = jax.nn.sigmoid(g0_ref[0].astype(F32)) * _mm(ya, wbr_ref[0])
    acc = acc + jax.nn.sigmoid(g1_ref[0].astype(F32)) * _mm(yb, wbr_ref[1])
    acc = acc + jax.nn.sigmoid(g2_ref[0].astype(F32)) * _mm(yc, wbr_ref[2])
    d = D_MODEL
    gate = _row_select(t, tm, lc, modc_ref[:, 2 * d:], modx_ref[0][:, 2 * d:])
    o_ref[0] = h_ref[0] + gate * _mm(acc, wout_ref[...])


def _merge_call(h, oa, ob, oc, p_zg, modx, modc, wa, wb, w_branch, w_out, lc, with_ctx):
    b, l, d = h.shape
    tm = _pick(lc, (256, 128))
    t_off = 0 if with_ctx else lc // tm
    nt = l // tm - t_off
    oc_off = 0 if oc.shape[1] == l else lc // tm
    row = lambda i, t: (i, t + t_off, 0)
    zg = lambda n: pl.BlockSpec((1, tm, d), lambda i, t: (i, t + t_off, n))
    return pl.pallas_call(
        functools.partial(_merge_kernel, lc=lc, tm=tm, t_off=t_off),
        grid=(b, nt),
        in_specs=[pl.BlockSpec((1, tm, d), row), pl.BlockSpec((1, tm, d), row), pl.BlockSpec((1, tm, d), row),
                  pl.BlockSpec((1, tm, d), lambda i, t: (i, t + t_off - oc_off, 0)),
                  zg(0), zg(1), zg(2), zg(3), zg(4), zg(5),
                  pl.BlockSpec((1, 1, 3 * d), lambda i, t: (i, 0, 0)),
                  pl.BlockSpec((1, 3 * d), lambda i, t: (0, 0)),
                  pl.BlockSpec((1, d), lambda i, t: (0, 0)),
                  pl.BlockSpec((1, d), lambda i, t: (0, 0)),
                  pl.BlockSpec((3, d, d), lambda i, t: (0, 0, 0)),
                  pl.BlockSpec((d, d), lambda i, t: (0, 0))],
        out_specs=pl.BlockSpec((1, tm, d), lambda i, t: (i, t, 0)),
        out_shape=jax.ShapeDtypeStruct((b, nt * tm, d), F32),
        compiler_params=_params(("parallel", "parallel")),
        name="merge",
    )(h, oa, ob, oc, p_zg, p_zg, p_zg, p_zg, p_zg, p_zg, modx, modc, wa, wb, w_branch, w_out)


def _cols(w, *names):
    return jnp.concatenate([w[:, _IN_OFF[n]:_IN_OFF[n] + _IN_LEN[n]] for n in names], axis=1)


def _rope_tables(t_lat, lc):
    rows = t_lat // GRID_W
    row = jnp.repeat(jnp.arange(rows, dtype=F32), GRID_W)
    col = jnp.tile(jnp.arange(GRID_W, dtype=F32), rows)
    inv_freq = jnp.power(ROPE_BASE, -jnp.arange(ROPE_FREQ, dtype=F32) / ROPE_FREQ)
    ar = row[:, None] * inv_freq
    ac = col[:, None] * inv_freq
    zeros = jnp.zeros((t_lat, 128 - MLA_ROPE), F32)
    cos_t = jnp.concatenate([jnp.cos(ar), jnp.cos(ar), jnp.cos(ac), jnp.cos(ac), zeros], axis=1)
    sin_t = jnp.concatenate([-jnp.sin(ar), jnp.sin(ar), -jnp.sin(ac), jnp.sin(ac), zeros], axis=1)
    cos_c = jnp.concatenate([jnp.ones((lc, MLA_ROPE), F32), jnp.zeros((lc, 128 - MLA_ROPE), F32)], axis=1)
    return (jnp.concatenate([cos_c, cos_t], axis=0),
            jnp.concatenate([jnp.zeros((lc, 128), F32), sin_t], axis=0))


def _pad_lanes(w, n=128):
    return jnp.pad(w, (0, n - w.shape[0])).reshape(1, n)


def kernel(x, c, ctx, c_ctx, norm_w, ada_w, ada_b, w_in, gla_gate_w2, gla_gate_b, gla_norm_w, gdn_conv_w,
           gdn_a_log, gdn_dt_bias, gdn_norm_w, mla_q_norm_w, mla_q_up, mla_kv_norm_w, mla_kv_up,
           mla_qn_nope, mla_qn_rope, mla_kn_nope, mla_kn_rope, w_branch, w_out):
    b, t_lat, d = x.shape
    lc = ctx.shape[1]
    l = lc + t_lat
    depth = w_in.shape[0]
    assert d == D_MODEL and lc % 128 == 0 and t_lat % 128 == 0

    h = jnp.concatenate([ctx, x], axis=1)
    rows = ((b + 1 + 7) // 8) * 8
    c_all = jnp.concatenate([c, c_ctx[None, :], jnp.zeros((rows - b - 1, d), F32)], axis=0)
    mod = _ada_call(c_all, ada_w, ada_b)
    cos_t, sin_t = _rope_tables(t_lat, lc)

    for li in range(depth):
        last = li == depth - 1
        modx = mod[li, :b].reshape(b, 1, 3 * d)
        modc = mod[li, b:b + 1]
        w = w_in[li]
        w_mla = _cols(w, 'mla_q_down', 'mla_kv_down', 'mla_k_rope', 'gla_gate_lr', 'gdn_a', 'gdn_b').astype(BF16)
        w_gla = _cols(w, 'gla_q', 'gla_k', 'gla_v').astype(BF16)
        w_gdn = _cols(w, 'gdn_qkv').astype(BF16)
        w_zg = _cols(w, 'gla_z', 'gdn_z', 'mla_z', 'merge_gate').astype(BF16)

        hn = _prenorm_call(h, modx, modc, norm_w[li], lc).reshape(b * l, d)
        p_mla = _proj_call(hn, w_mla, F32, "proj_mla").reshape(b, l, -1)
        p_gla = _proj_call(hn, w_gla, F32, "proj_gla").reshape(b, l, -1)
        p_gdn = _proj_call(hn, w_gdn, F32, "proj_gdn").reshape(b, l, -1)
        p_zg = _proj_call(hn, w_zg, BF16, "proj_zg").reshape(b, l, -1)

        w2e = jnp.zeros((2, 128, GLA_HEADS * GLA_DK), F32).at[0, SM_LR:SM_LR + GLA_RANK].set(
            gla_gate_w2[li, 0]).at[1, SM_LR + GLA_RANK:SM_LR + 2 * GLA_RANK].set(gla_gate_w2[li, 1])
        oa = _gla_call(p_gla, p_mla, w2e, gla_gate_b[li].reshape(2, 1, -1), lc)

        qkvn = _gdn_prep_call(p_gdn, gdn_conv_w[li], lc)
        abc, abr = _gdn_gate_layouts(p_mla, lc)
        ob = _gdn_call(qkvn, abc, abr, gdn_a_log[li], gdn_dt_bias[li], lc)

        qup = mla_q_up[li].reshape(MLA_Q_LORA, MLA_HEADS, MLA_NOPE + MLA_ROPE)
        qup = jnp.pad(qup, ((0, 0), (0, 0), (0, 256 - MLA_NOPE - MLA_ROPE))).reshape(MLA_Q_LORA, -1).astype(BF16)
        kvup = mla_kv_up[li].reshape(MLA_KV_LORA, MLA_HEADS, MLA_NOPE + MLA_DV)
        kvup = jnp.concatenate([kvup[:, :, :MLA_NOPE].reshape(MLA_KV_LORA, -1),
                                kvup[:, :, MLA_NOPE:].reshape(MLA_KV_LORA, -1)], axis=1).astype(BF16)
        qm, km, vm = _mla_prep_call(
            p_mla, cos_t, sin_t, mla_q_norm_w[li].reshape(1, -1), qup, mla_kv_norm_w[li].reshape(1, -1), kvup,
            mla_qn_nope[li].reshape(1, -1), _pad_lanes(mla_qn_rope[li]),
            mla_kn_nope[li].reshape(1, -1), _pad_lanes(mla_kn_rope[li]))
        oc = _mla_attn_call(qm, km, vm, lc, with_ctx=not last)

        h = _merge_call(h, oa, ob, oc, p_zg, modx, modc,
                        jnp.tile(gla_norm_w[li], GLA_HEADS).reshape(1, -1),
                        jnp.tile(gdn_norm_w[li], GDN_HEADS).reshape(1, -1),
                        w_branch[li].astype(BF16), w_out[li].astype(BF16), lc, with_ctx=not last)
    return h
```

```python
import functools
import math

import numpy as np
import jax
import jax.numpy as jnp
from jax import lax
from jax.experimental import pallas as pl
from jax.experimental.pallas import tpu as pltpu

F32 = jnp.float32
BF16 = jnp.bfloat16

D_MODEL = 1024
EPS = 1e-6
GRID_W = 64
CHUNK = 64

GLA_HEADS, GLA_DK, GLA_DV, GLA_RANK, GLA_NORMALIZER = 4, 128, 256, 16, 16.0
GDN_HEADS, GDN_DK, GDN_DV, GDN_CONV = 8, 128, 128, 5
GLA_HB = 2
GDN_HB = 8
_GDN_GATE_ROWS = max(8, 2 * GDN_HB)
MLA_HEADS, MLA_Q_LORA, MLA_KV_LORA, MLA_NOPE, MLA_ROPE, MLA_DV = 8, 384, 256, 128, 64, 128
MLA_SCALE = (MLA_NOPE + MLA_ROPE) ** -0.5
MLA_Q_PRESCALE = MLA_SCALE * math.log2(math.e)
MLA_HP = 2
ROPE_FREQ = MLA_ROPE // 4
ROPE_BASE = 10000.0
BRANCH_W = 1024

_IN_SIZES = (512, 512, 1024, 32, 1024, 3072, 16, 16, 1024, 384, 256, 64, 1024, 3072)
_IN_NAMES = ('gla_q', 'gla_k', 'gla_v', 'gla_gate_lr', 'gla_z', 'gdn_qkv', 'gdn_a', 'gdn_b', 'gdn_z',
             'mla_q_down', 'mla_kv_down', 'mla_k_rope', 'mla_z', 'merge_gate')
_IN_OFF = dict(zip(_IN_NAMES, np.concatenate([[0], np.cumsum(_IN_SIZES)[:-1]]).tolist()))
_IN_LEN = dict(zip(_IN_NAMES, _IN_SIZES))

SM_ROPE, SM_LR, SM_A, SM_B = 0, 64, 96, 112
MLA_GROUP_W = MLA_Q_LORA + MLA_KV_LORA + 128
SMALL_BLOCK = (MLA_Q_LORA + MLA_KV_LORA) // 128

VMEM_LIMIT = 56 * 1024 * 1024


def _params(sem):
    return pltpu.CompilerParams(dimension_semantics=sem, vmem_limit_bytes=VMEM_LIMIT)


def _pick(n, cands):
    for c in cands:
        if n % c == 0:
            return c
    raise ValueError(f"no tile for {n} in {cands}")


def _mm(a, b):
    return jnp.dot(a.astype(BF16), b.astype(BF16), preferred_element_type=F32)


def _mm_nt(a, b):
    return lax.dot_general(a.astype(BF16), b.astype(BF16), (((1,), (1,)), ((), ())),
                           preferred_element_type=F32)


def _mm_tn(a, b):
    return lax.dot_general(a.astype(BF16), b.astype(BF16), (((0,), (0,)), ((), ())),
                           preferred_element_type=F32)


def _mm_hp(a, b):
    return jnp.dot(a, b, precision=lax.Precision.HIGHEST, preferred_element_type=F32)


def _split(a):
    hi = a.astype(BF16)
    lo = (a - hi.astype(F32)).astype(BF16)
    return hi, lo


def _mm3(a, b):
    ah, al = _split(a)
    bh, bl = _split(b)
    d = functools.partial(jnp.dot, preferred_element_type=F32)
    return d(ah, bh) + (d(ah, bl) + d(al, bh))


_MMX = _mm
_TRI_BLOCK = 8


def _cum_rows(cum, x):
    cb = cum.astype(BF16)
    d = functools.partial(jnp.dot, preferred_element_type=F32)
    hi, lo = _split(x)
    return d(cb, hi) + d(cb, lo)


def _cum_cols(x, cum):
    cb = cum.astype(BF16)
    d = functools.partial(jnp.dot, preferred_element_type=F32)
    hi, lo = _split(x)
    return d(hi, cb) + d(lo, cb)


def _softplus(x):
    return jnp.maximum(x, 0.0) + jnp.log1p(jnp.exp(-jnp.abs(x)))


def _log_sigmoid(x):
    return jnp.minimum(x, 0.0) - jnp.log1p(jnp.exp(-jnp.abs(x)))


def _silu(x):
    return x * jax.nn.sigmoid(x)


def _tri_masks(n, reverse):
    r = lax.broadcasted_iota(jnp.int32, (n, n), 0)
    c = lax.broadcasted_iota(jnp.int32, (n, n), 1)
    incl = (c >= r) if reverse else (c <= r)
    strict = (c > r) if reverse else (c < r)
    return r, c, incl, strict


def _ada_kernel(c_ref, w_ref, b_ref, o_ref):
    o_ref[0] = _mm_hp(_silu(c_ref[...]), w_ref[0]) + b_ref[0]


def _ada_call(c_all, ada_w, ada_b):
    nl, d, n3 = ada_w.shape
    r = c_all.shape[0]
    tn = 1024
    return pl.pallas_call(
        _ada_kernel,
        grid=(nl, n3 // tn),
        in_specs=[pl.BlockSpec((r, d), lambda l, j: (0, 0)),
                  pl.BlockSpec((1, d, tn), lambda l, j: (l, 0, j)),
                  pl.BlockSpec((1, 1, tn), lambda l, j: (l, 0, j))],
        out_specs=pl.BlockSpec((1, r, tn), lambda l, j: (l, 0, j)),
        out_shape=jax.ShapeDtypeStruct((nl, r, n3), F32),
        compiler_params=_params(("arbitrary", "arbitrary")),
        name="ada_rows",
    )(c_all, ada_w, ada_b.reshape(nl, 1, n3))


def _row_select(t, tm, lc, ctx_row, lat_row):
    rows = t * tm + lax.broadcasted_iota(jnp.int32, (tm, 1), 0)
    return jnp.where(rows < lc, ctx_row, lat_row)


def _prenorm_kernel(h_ref, modx_ref, modc_ref, nw_ref, o_ref, *, lc, tm):
    t = pl.program_id(1)
    x = h_ref[0]
    y = x * lax.rsqrt(jnp.mean(x * x, axis=-1, keepdims=True) + EPS) * nw_ref[...]
    mx = modx_ref[0]
    mc = modc_ref[...]
    d = D_MODEL
    shift = _row_select(t, tm, lc, mc[:, :d], mx[:, :d])
    scale = _row_select(t, tm, lc, mc[:, d:2 * d], mx[:, d:2 * d])
    o_ref[0] = (y * (1.0 + scale) + shift).astype(o_ref.dtype)


def _prenorm_call(h, modx, modc, norm_w, lc):
    b, l, d = h.shape
    tm = _pick(l, (768, 512, 384, 256, 128))
    return pl.pallas_call(
        functools.partial(_prenorm_kernel, lc=lc, tm=tm),
        grid=(b, l // tm),
        in_specs=[pl.BlockSpec((1, tm, d), lambda i, t: (i, t, 0)),
                  pl.BlockSpec((1, 1, 3 * d), lambda i, t: (i, 0, 0)),
                  pl.BlockSpec((1, 3 * d), lambda i, t: (0, 0)),
                  pl.BlockSpec((1, d), lambda i, t: (0, 0))],
        out_specs=pl.BlockSpec((1, tm, d), lambda i, t: (i, t, 0)),
        out_shape=jax.ShapeDtypeStruct((b, l, d), BF16),
        compiler_params=_params(("parallel", "parallel")),
        name="prenorm",
    )(h, modx, modc, norm_w.reshape(1, d))


def _proj_kernel(x_ref, w_ref, o_ref):
    o_ref[...] = jnp.dot(x_ref[...], w_ref[...], preferred_element_type=F32).astype(o_ref.dtype)


def _proj_call(x, w, out_dtype, name):
    m, k = x.shape
    n = w.shape[1]
    tm = _pick(m, (1024, 768, 512, 384, 256, 128))
    tn = _pick(n, (1536, 1024, 768, 512, 256, 128))
    return pl.pallas_call(
        _proj_kernel,
        grid=(m // tm, n // tn),
        in_specs=[pl.BlockSpec((tm, k), lambda i, j: (i, 0)),
                  pl.BlockSpec((k, tn), lambda i, j: (0, j))],
        out_specs=pl.BlockSpec((tm, tn), lambda i, j: (i, j)),
        out_shape=jax.ShapeDtypeStruct((m, n), out_dtype),
        compiler_params=_params(("parallel", "parallel")),
        name=name,
    )(x, w)


def _bwd_chunk(i, ncc, nc):
    return jnp.where(i < ncc, ncc - 1 - i, ncc + nc - 1 - i)


def _chunk_rows(c):
    return pl.ds(pl.multiple_of(c * CHUNK, CHUNK), CHUNK)


def _gla_kernel(q_ref, k_ref, v_ref, sm_ref, w2_ref, gb_ref, o_ref, la_ref, st_ref, *, ncc, nc):
    sm = sm_ref[0]
    for d in range(2):
        logit = _mm3(sm, w2_ref[d]) + gb_ref[d]
        la_ref[d] = _log_sigmoid(logit) * (1.0 / GLA_NORMALIZER)
    st_ref[...] = jnp.zeros_like(st_ref)
    o_ref[...] = jnp.zeros_like(o_ref)
    masks = [_tri_masks(CHUNK, rev) for rev in (False, True)]
    cum_mats = [m[2].astype(F32) for m in masks]
    probs = [(d, hh) for d in range(2) for hh in range(GLA_HB)]

    def body(i, carry):
        rows = [_chunk_rows(i), _chunk_rows(_bwd_chunk(i, ncc, nc))]
        bcum_d = [_cum_rows(cum_mats[d], la_ref[d, rows[d], :]) for d in range(2)]
        kl = [slice(hh * GLA_DK, (hh + 1) * GLA_DK) for _, hh in probs]
        vl = [slice(hh * GLA_DV, (hh + 1) * GLA_DV) for _, hh in probs]
        bcum = [bcum_d[d][:, ln] for (d, _), ln in zip(probs, kl)]
        b_last = [bc[CHUNK - 1:CHUNK] if d == 0 else bc[0:1] for (d, _), bc in zip(probs, bcum)]
        q = [q_ref[0, rows[d], ln].astype(F32) * GLA_DK ** -0.5 for (d, _), ln in zip(probs, kl)]
        k = [k_ref[0, rows[d], ln].astype(F32) for (d, _), ln in zip(probs, kl)]
        v = [v_ref[0, rows[d], ln].astype(F32) for (d, _), ln in zip(probs, vl)]
        q_dec = [a * jnp.exp(bc) for a, bc in zip(q, bcum)]
        k_inv = [a * jnp.exp(-bc) for a, bc in zip(k, bcum)]
        k_end = [a * jnp.exp(bl - bc) for a, bl, bc in zip(k, b_last, bcum)]
        st = [st_ref[j] for j in range(len(probs))]
        qk = _each(_mm_nt, q_dec, k_inv)
        o_inter = _each(_mm_nt, q_dec, st)
        upd = _each(_mm_tn, v, k_end)
        att = [jnp.where(masks[d][2], a, 0.0) for (d, _), a in zip(probs, qk)]
        o_intra = _each(_mm, att, v)
        for j, (sx, bl, up) in enumerate(zip(st, b_last, upd)):
            st_ref[j] = sx * jnp.exp(bl) + up
        for (d, _), ln, a, b in zip(probs, vl, o_intra, o_inter):
            o_ref[0, rows[d], ln] += a + b
        return carry

    lax.fori_loop(0, nc, body, 0)


def _gla_call(p_gla, p_mla, w2e, gate_b, lc):
    b, l, _ = p_gla.shape
    nc, ncc = l // CHUNK, lc // CHUNK
    nhb = GLA_HEADS // GLA_HB
    wk, wv = GLA_HB * GLA_DK, GLA_HB * GLA_DV
    return pl.pallas_call(
        functools.partial(_gla_kernel, ncc=ncc, nc=nc),
        grid=(b, nhb),
        in_specs=[pl.BlockSpec((1, l, wk), lambda i, h: (i, 0, h)),
                  pl.BlockSpec((1, l, wk), lambda i, h: (i, 0, nhb + h)),
                  pl.BlockSpec((1, l, wv), lambda i, h: (i, 0, nhb + h)),
                  pl.BlockSpec((1, l, 128), lambda i, h: (i, 0, SMALL_BLOCK)),
                  pl.BlockSpec((2, 128, wk), lambda i, h: (0, 0, h)),
                  pl.BlockSpec((2, 1, wk), lambda i, h: (0, 0, h))],
        out_specs=pl.BlockSpec((1, l, wv), lambda i, h: (i, 0, h)),
        out_shape=jax.ShapeDtypeStruct((b, l, GLA_HEADS * GLA_DV), F32),
        scratch_shapes=[pltpu.VMEM((2, l, wk), F32), pltpu.VMEM((2 * GLA_HB, GLA_DV, GLA_DK), F32)],
        compiler_params=_params(("parallel", "parallel")),
        name="gla_scan",
    )(p_gla, p_gla, p_gla, p_mla, w2e, gate_b)


def _gdn_prep_kernel(x_ref, w_ref, o_ref, *, lc, l):
    j = pl.program_id(1)
    x = x_ref[0].astype(F32)
    w = w_ref[...]
    t = lax.broadcasted_iota(jnp.int32, (l, 128), 0)
    start = jnp.where(t < lc, 0, lc)
    end = jnp.where(t < lc, lc, l)
    half = GDN_CONV // 2
    acc = x * w[half:half + 1]
    for s in range(-half, half + 1):
        if s == 0:
            continue
        xs = pltpu.roll(x, (-s) % l, axis=0)
        ok = (t + s >= start) if s < 0 else (t + s < end)
        acc = acc + jnp.where(ok, xs, 0.0) * w[s + half:s + half + 1]
    y = _silu(acc)
    nblk_head = GDN_HEADS * GDN_DK // 128
    inv = lax.rsqrt(jnp.sum(y * y, axis=-1, keepdims=True) + EPS)
    fac = jnp.where(j < nblk_head, inv * GDN_DK ** -0.5, jnp.where(j < 2 * nblk_head, inv, 1.0))
    o_ref[0] = (y * fac).astype(o_ref.dtype)


def _gdn_prep_call(p_gdn, conv_w, lc):
    b, l, n = p_gdn.shape
    return pl.pallas_call(
        functools.partial(_gdn_prep_kernel, lc=lc, l=l),
        grid=(b, n // 128),
        in_specs=[pl.BlockSpec((1, l, 128), lambda i, j: (i, 0, j)),
                  pl.BlockSpec((GDN_CONV, 128), lambda i, j: (0, j))],
        out_specs=pl.BlockSpec((1, l, 128), lambda i, j: (i, 0, j)),
        out_shape=jax.ShapeDtypeStruct((b, l, n), BF16),
        compiler_params=_params(("parallel", "parallel")),
        name="gdn_prep",
    )(p_gdn, conv_w)


def _each(fn, *lists):
    return [fn(*args) for args in zip(*lists)]


def _unit_tri_solves(lms, rhs, r, c):
    eye = (r == c).astype(F32)
    same = lambda s: (r // s) == (c // s)
    nb = _TRI_BLOCK
    y = [-jnp.where(same(nb), lm, 0.0) for lm in lms]
    t = [eye + a for a in y]
    p = y
    for _ in range(int(math.log2(nb)) - 1):
        p = _each(_MMX, p, p)
        t = _each(lambda a, b: a + _MMX(a, b), t, p)
    sizes = [nb * 2 ** j for j in range(int(math.log2(CHUNK // nb)))]
    nlev = len(sizes)
    w = rhs[0].shape[1]
    pad = jnp.zeros((CHUNK, 128 - CHUNK), F32)
    offs = [[jnp.where(same(2 * s), jnp.where(same(s), 0.0, lm), 0.0) for s in reversed(sizes)] for lm in lms]
    wide = [jnp.concatenate([x] + [part for o in off for part in (o, pad)], axis=1) for x, off in zip(rhs, offs)]
    z = _each(_mm, t, wide)
    for j in range(nlev):
        keep = w + 128 * (nlev - 1 - j)
        m = [x[:, keep:keep + CHUNK] for x in z]
        z = [x[:, :keep] for x in z]
        z = _each(lambda x, mj: x - _mm(mj, x), z, m)
    return z


def _gdn_kernel(rate_ref, ratel_ref, q_ref, k_ref, v_ref, abc_ref, abr_ref, o_ref, s_ref, *, ncc, nc):
    na = 2 * GDN_HB
    s_ref[...] = jnp.zeros_like(s_ref)
    o_ref[...] = jnp.zeros_like(o_ref)
    masks = [_tri_masks(CHUNK, rev) for rev in (False, True)]
    cum_mats = [m[2].astype(F32) for m in masks]
    r, cc = masks[0][0], masks[0][1]
    probs = [(d, hh) for d in range(2) for hh in range(GDN_HB)]
    neg_rate_rows = -jnp.exp(rate_ref[0, 0])[:, :CHUNK]
    dtb_rows = rate_ref[0, 1][:, :CHUNK]
    neg_rate_lanes = -jnp.exp(ratel_ref[0, 0, 0:1, :])
    dtb_lanes = ratel_ref[0, 1, 0:1, :]

    def body(i, carry):
        dirs = (0, 1)
        chunk = [i, _bwd_chunk(i, ncc, nc)]
        rows = [_chunk_rows(c) for c in chunk]
        abc = [abc_ref[0, 0, rw, :] for rw in rows]
        abr = [abr_ref[0, 0, c] for c in chunk]
        incl = [masks[d][2] for d, _ in probs]
        strict = [masks[d][3] for d, _ in probs]
        g_cols = [neg_rate_lanes * _softplus(abc[d] + dtb_lanes) for d in dirs]
        g_rows = [neg_rate_rows * _softplus(abr[d] + dtb_rows) for d in dirs]
        gc_cols = [_cum_rows(cum_mats[d], g_cols[d]) for d in dirs]
        gc_rows = [_cum_cols(g_rows[d], cum_mats[1 - d]) for d in dirs]
        beta_cols = [jax.nn.sigmoid(abc[d]) for d in dirs]
        col = [d * GDN_HB + hh for d, hh in probs]
        gc_col = [jnp.broadcast_to(gc_cols[d][:, j:j + 1], (CHUNK, 128)) for (d, _), j in zip(probs, col)]
        gc_row = [gc_rows[d][j:j + 1, :] for (d, _), j in zip(probs, col)]
        beta_col = [beta_cols[d][:, na + j:na + j + 1] for (d, _), j in zip(probs, col)]
        g_last =[gc[CHUNK - 1:CHUNK] if d == 0 else gc[0:1] for (d, _), gc in zip(probs, gc_col)]
        decay = [jnp.where(m, jnp.exp(jnp.where(m, a[:, :CHUNK] - b, 0.0)), 0.0)
                 for m, a, b in zip(incl, gc_col, gc_row)]

        lanes = [slice(hh * GDN_DK, (hh + 1) * GDN_DK) for _, hh in probs]
        q = [q_ref[0, rows[d], ln].astype(F32) for (d, _), ln in zip(probs, lanes)]
        k = [k_ref[0, rows[d], ln].astype(F32) for (d, _), ln in zip(probs, lanes)]
        v = [v_ref[0, rows[d], ln].astype(F32) for (d, _), ln in zip(probs, lanes)]
        k_beta = _each(jnp.multiply, k, beta_col)
        v_beta = _each(jnp.multiply, v, beta_col)
        e_col = _each(jnp.exp, gc_col)
        kq = _each(_mm_nt, [jnp.concatenate([a, b], axis=0) for a, b in zip(k_beta, q)], k)
        kk = [x[:CHUNK] for x in kq]
        qk = [x[CHUNK:] for x in kq]
        lm =[jnp.where(m, a * dc, 0.0) for m, a, dc in zip(strict, kk, decay)]
        rhs = [jnp.concatenate([vb, kb * e], axis=1) for vb, kb, e in zip(v_beta, k_beta, e_col)]
        uw = _unit_tri_solves(lm, rhs, r, cc)
        u = [x[:, :GDN_DV] for x in uw]
        w = [x[:, GDN_DV:] for x in uw]
        att =[jnp.where(m, a * dc, 0.0) for m, a, dc in zip(incl, qk, decay)]
        q_dec = _each(jnp.multiply, q, e_col)
        k_end = [kx * jnp.exp(gl - gc) for kx, gl, gc in zip(k, g_last, gc_col)]
        s = [s_ref[d * GDN_HB + hh] for d, hh in probs]
        wq = _each(_mm, [jnp.concatenate([a, b], axis=0) for a, b in zip(w, q_dec)], s)
        ws = [x[:CHUNK] for x in wq]
        o_inter = [x[CHUNK:] for x in wq]
        v_new =_each(jnp.subtract, u, ws)
        upd = _each(_mm_tn, k_end, v_new)
        o_intra = _each(_mm, att, v_new)
        for (d, hh), sx, gl, up in zip(probs, s, g_last, upd):
            s_ref[d * GDN_HB + hh] = sx * jnp.exp(gl) + up
        for (d, _), ln, a, b in zip(probs, lanes, o_intra, o_inter):
            o_ref[0, rows[d], ln] += a + b
        return carry

    lax.fori_loop(0, nc, body, 0)


def _gdn_call(qkvn, abc, abr, a_log, dt_bias, lc):
    b, l, _ = qkvn.shape
    nc, ncc = l // CHUNK, lc // CHUNK
    nhb = GDN_HEADS // GDN_HB
    wb = GDN_HB * GDN_DK
    na = 2 * GDN_HB
    rate = jnp.stack([a_log, dt_bias]).reshape(2, 2, nhb, GDN_HB).transpose(2, 0, 1, 3).reshape(nhb, 2, na)
    rate = jnp.pad(rate, ((0, 0), (0, 0), (0, 128 - na)))
    nr = _GDN_GATE_ROWS
    rate_rows = jnp.broadcast_to(rate[:, :, :nr, None], (nhb, 2, nr, 128))
    rate_lanes = jnp.broadcast_to(rate[:, :, None, :], (nhb, 2, 8, 128))
    return pl.pallas_call(
        functools.partial(_gdn_kernel, ncc=ncc, nc=nc),
        grid=(b, nhb),
        in_specs=[pl.BlockSpec((1, 2, nr, 128), lambda i, h: (h, 0, 0, 0)),
                  pl.BlockSpec((1, 2, 8, 128), lambda i, h: (h, 0, 0, 0)),
                  pl.BlockSpec((1, l, wb), lambda i, h: (i, 0, h)),
                  pl.BlockSpec((1, l, wb), lambda i, h: (i, 0, nhb + h)),
                  pl.BlockSpec((1, l, wb), lambda i, h: (i, 0, 2 * nhb + h)),
                  pl.BlockSpec((1, 1, l, 128), lambda i, h: (i, h, 0, 0)),
                  pl.BlockSpec((1, 1, nc, nr, CHUNK), lambda i, h: (i, h, 0, 0, 0))],
        out_specs=pl.BlockSpec((1, l, wb), lambda i, h: (i, 0, h)),
        out_shape=jax.ShapeDtypeStruct((b, l, GDN_HEADS * GDN_DV), F32),
        scratch_shapes=[pltpu.VMEM((2 * GDN_HB, GDN_DK, GDN_DV), F32)],
        compiler_params=_params(("parallel", "parallel")),
        name="gdn_scan",
    )(rate_rows, rate_lanes, qkvn, qkvn, qkvn, abc, abr)


def _gdn_gate_layouts(p_mla, lc):
    b, l, _ = p_mla.shape
    nhb = GDN_HEADS // GDN_HB
    na = 2 * GDN_HB
    o = MLA_Q_LORA + MLA_KV_LORA
    ab = p_mla[:, :, o + SM_A:o + SM_A + 32].reshape(b, l, 2, 2, nhb, GDN_HB)
    ab = ab.transpose(0, 4, 1, 2, 3, 5).reshape(b, nhb, l, 2 * na)
    abr = ab[..., :na].reshape(b, nhb, l // CHUNK, CHUNK, na).transpose(0, 1, 2, 4, 3)
    abr = jnp.pad(abr, ((0, 0), (0, 0), (0, 0), (0, _GDN_GATE_ROWS - na), (0, 0)))
    abc = jnp.pad(ab, ((0, 0), (0, 0), (0, 0), (0, 128 - 2 * na)))
    return abc, abr


def _rope_swap(x):
    lane = lax.broadcasted_iota(jnp.int32, x.shape, 1)
    n = x.shape[1]
    return jnp.where(lane % 32 < 16, pltpu.roll(x, n - 16, axis=1), pltpu.roll(x, 16, axis=1))


def _mla_prep_kernel(p_ref, cos_ref, sin_ref, qnw_ref, qup_ref, kvnw_ref, kvup_ref,
                     wqn_ref, wqr_ref, wkn_ref, wkr_ref, q_ref, k_ref, v_ref):
    p = p_ref[0]
    cos_t = cos_ref[...]
    sin_t = sin_ref[...]

    def rms(x, n):
        return x * lax.rsqrt(jnp.sum(x * x, axis=-1, keepdims=True) * (1.0 / n) + EPS)

    def rope(x):
        return x * cos_t + _rope_swap(x) * sin_t

    qd = p[:, :MLA_Q_LORA]
    kvd = p[:, MLA_Q_LORA:MLA_Q_LORA + MLA_KV_LORA]
    sm = p[:, MLA_Q_LORA + MLA_KV_LORA:]
    q = _mm(rms(qd, MLA_Q_LORA) * qnw_ref[...], qup_ref[...])
    kv = _mm(rms(kvd, MLA_KV_LORA) * kvnw_ref[...], kvup_ref[...])
    lane = lax.broadcasted_iota(jnp.int32, sm.shape, 1)
    kr = jnp.where(lane < MLA_ROPE, sm, 0.0)
    kr = rope(rms(kr, MLA_ROPE) * wkr_ref[...])
    for h in range(MLA_HEADS):
        base = h * 256
        qn = rms(q[:, base:base + 128], MLA_NOPE) * wqn_ref[...]
        q_ref[0, :, base:base + 128] = (qn * MLA_Q_PRESCALE).astype(q_ref.dtype)
        qr = rms(q[:, base + 128:base + 256], MLA_ROPE) * wqr_ref[...]
        q_ref[0, :, base + 128:base + 256] = (rope(qr) * MLA_Q_PRESCALE).astype(q_ref.dtype)
        kn = kv[:, h * 128:(h + 1) * 128]
        k_ref[0, :, base:base + 128] = (rms(kn, MLA_NOPE) * wkn_ref[...]).astype(k_ref.dtype)
        k_ref[0, :, base + 128:base + 256] = kr.astype(k_ref.dtype)
    v_ref[0] = kv[:, MLA_HEADS * MLA_NOPE:].astype(v_ref.dtype)


def _mla_prep_call(p_mla, cos_t, sin_t, qnw, qup, kvnw, kvup, wqn, wqr, wkn, wkr):
    b, l, gw = p_mla.shape
    tm = _pick(l, (256, 128))
    hq = MLA_HEADS * 256
    full = lambda shape: pl.BlockSpec(shape, lambda i, t: tuple(0 for _ in shape))
    return pl.pallas_call(
        _mla_prep_kernel,
        grid=(b, l // tm),
        in_specs=[pl.BlockSpec((1, tm, gw), lambda i, t: (i, t, 0)),
                  pl.BlockSpec((tm, 128), lambda i, t: (t, 0)),
                  pl.BlockSpec((tm, 128), lambda i, t: (t, 0)),
                  full((1, MLA_Q_LORA)), full((MLA_Q_LORA, hq)),
                  full((1, MLA_KV_LORA)), full((MLA_KV_LORA, hq)),
                  full((1, 128)), full((1, 128)), full((1, 128)), full((1, 128))],
        out_specs=[pl.BlockSpec((1, tm, hq), lambda i, t: (i, t, 0)),
                   pl.BlockSpec((1, tm, hq), lambda i, t: (i, t, 0)),
                   pl.BlockSpec((1, tm, MLA_HEADS * MLA_DV), lambda i, t: (i, t, 0))],
        out_shape=[jax.ShapeDtypeStruct((b, l, hq), BF16),
                   jax.ShapeDtypeStruct((b, l, hq), BF16),
                   jax.ShapeDtypeStruct((b, l, MLA_HEADS * MLA_DV), BF16)],
        compiler_params=_params(("parallel", "parallel")),
        name="mla_prep",
    )(p_mla, cos_t, sin_t, qnw, qup, kvnw, kvup, wqn, wqr, wkn, wkr)


def _mla_attn_kernel(q_ref, k_ref, v_ref, o_ref, *, lc, tq, q_off):
    qi = pl.program_id(2) + q_off
    heads = range(MLA_HP)

    def attend(nk):
        s = [lax.dot_general(q_ref[0, :, h * 256:(h + 1) * 256], k_ref[0, :nk, h * 256:(h + 1) * 256],
                             (((1,), (1,)), ((), ())), preferred_element_type=F32) for h in heads]
        e = [jnp.exp2(x - jnp.max(x, axis=-1, keepdims=True)) for x in s]
        den = [jnp.sum(x, axis=-1, keepdims=True) for x in e]
        o = [jnp.dot(e[h].astype(BF16), v_ref[0, :nk, h * MLA_DV:(h + 1) * MLA_DV],
                     preferred_element_type=F32) for h in heads]
        for h in heads:
            o_ref[0, :, h * MLA_DV:(h + 1) * MLA_DV] = (o[h] / den[h]).astype(o_ref.dtype)

    if q_off == 0:
        @pl.when(qi * tq < lc)
        def _():
            attend(lc)

        @pl.when(qi * tq >= lc)
        def _():
            attend(k_ref.shape[1])
    else:
        attend(k_ref.shape[1])


def _mla_attn_call(q, k, v, lc, with_ctx):
    b, l, _ = q.shape
    tq = _pick(lc, (256, 128))
    q_off = 0 if with_ctx else lc // tq
    nq = l // tq - q_off
    return pl.pallas_call(
        functools.partial(_mla_attn_kernel, lc=lc, tq=tq, q_off=q_off),
        grid=(b, MLA_HEADS // MLA_HP, nq),
        in_specs=[pl.BlockSpec((1, tq, MLA_HP * 256), lambda i, h, t: (i, t + q_off, h)),
                  pl.BlockSpec((1, l, MLA_HP * 256), lambda i, h, t: (i, 0, h)),
                  pl.BlockSpec((1, l, MLA_HP * MLA_DV), lambda i, h, t: (i, 0, h))],
        out_specs=pl.BlockSpec((1, tq, MLA_HP * MLA_DV), lambda i, h, t: (i, t, h)),
        out_shape=jax.ShapeDtypeStruct((b, nq * tq, MLA_HEADS * MLA_DV), F32),
        compiler_params=_params(("parallel", "parallel", "arbitrary")),
        name="mla_attn",
    )(q, k, v)


def _merge_kernel(h_ref, oa_ref, ob_ref, oc_ref, za_ref, zb_ref, zc_ref, g0_ref, g1_ref, g2_ref,
                  modx_ref, modc_ref, wa_ref, wb_ref, wbr_ref, wout_ref, o_ref, *, lc, tm, t_off):
    t = pl.program_id(1) + t_off

    def head_norm(o, width):
        parts = []
        for h in range(BRANCH_W // width):
            x = o[:, h * width:(h + 1) * width]
            parts.append(x * lax.rsqrt(jnp.mean(x * x, axis=-1, keepdims=True) + EPS))
        return jnp.concatenate(parts, axis=-1)

    ya = head_norm(oa_ref[0], GLA_DV) * wa_ref[...] * _silu(za_ref[0].astype(F32))
    yb = head_norm(ob_ref[0], GDN_DV) * wb_ref[...] * _silu(zb_ref[0].astype(F32))
    yc = oc_ref[0] * _silu(zc_ref[0].astype(F32))
    acc = jax.nn.sigmoid(g0_ref[0].astype(F32)) * _mm(ya, wbr_ref[0])
    acc = acc + jax.nn.sigmoid(g1_ref[0].astype(F32)) * _mm(yb, wbr_ref[1])
    acc = acc + jax.nn.sigmoid(g2_ref[0].astype(F32)) * _mm(yc, wbr_ref[2])
    d = D_MODEL
    gate = _row_select(t, tm, lc, modc_ref[:, 2 * d:], modx_ref[0][:, 2 * d:])
    o_ref[0] = h_ref[0] + gate * _mm(acc, wout_ref[...])


def _merge_call(h, oa, ob, oc, p_zg, modx, modc, wa, wb, w_branch, w_out, lc, with_ctx):
    b, l, d = h.shape
    tm = _pick(lc, (256, 128))
    t_off = 0 if with_ctx else lc // tm
    nt = l // tm - t_off
    oc_off = 0 if oc.shape[1] == l else lc // tm
    row = lambda i, t: (i, t + t_off, 0)
    zg = lambda n: pl.BlockSpec((1, tm, d), lambda i, t: (i, t + t_off, n))
    return pl.pallas_call(
        functools.partial(_merge_kernel, lc=lc, tm=tm, t_off=t_off),
        grid=(b, nt),
        in_specs=[pl.BlockSpec((1, tm, d), row), pl.BlockSpec((1, tm, d), row), pl.BlockSpec((1, tm, d), row),
                  pl.BlockSpec((1, tm, d), lambda i, t: (i, t + t_off - oc_off, 0)),
                  zg(0), zg(1), zg(2), zg(3), zg(4), zg(5),
                  pl.BlockSpec((1, 1, 3 * d), lambda i, t: (i, 0, 0)),
                  pl.BlockSpec((1, 3 * d), lambda i, t: (0, 0)),
                  pl.BlockSpec((1, d), lambda i, t: (0, 0)),
                  pl.BlockSpec((1, d), lambda i, t: (0, 0)),
                  pl.BlockSpec((3, d, d), lambda i, t: (0, 0, 0)),
                  pl.BlockSpec((d, d), lambda i, t: (0, 0))],
        out_specs=pl.BlockSpec((1, tm, d), lambda i, t: (i, t, 0)),
        out_shape=jax.ShapeDtypeStruct((b, nt * tm, d), F32),
        compiler_params=_params(("parallel", "parallel")),
        name="merge",
    )(h, oa, ob, oc, p_zg, p_zg, p_zg, p_zg, p_zg, p_zg, modx, modc, wa, wb, w_branch, w_out)


def _cols(w, *names):
    return jnp.concatenate([w[:, _IN_OFF[n]:_IN_OFF[n] + _IN_LEN[n]] for n in names], axis=1)


def _rope_tables(t_lat, lc):
    rows = t_lat // GRID_W
    row = jnp.repeat(jnp.arange(rows, dtype=F32), GRID_W)
    col = jnp.tile(jnp.arange(GRID_W, dtype=F32), rows)
    inv_freq = jnp.power(ROPE_BASE, -jnp.arange(ROPE_FREQ, dtype=F32) / ROPE_FREQ)
    ar = row[:, None] * inv_freq
    ac = col[:, None] * inv_freq
    zeros = jnp.zeros((t_lat, 128 - MLA_ROPE), F32)
    cos_t = jnp.concatenate([jnp.cos(ar), jnp.cos(ar), jnp.cos(ac), jnp.cos(ac), zeros], axis=1)
    sin_t = jnp.concatenate([-jnp.sin(ar), jnp.sin(ar), -jnp.sin(ac), jnp.sin(ac), zeros], axis=1)
    cos_c = jnp.concatenate([jnp.ones((lc, MLA_ROPE), F32), jnp.zeros((lc, 128 - MLA_ROPE), F32)], axis=1)
    return (jnp.concatenate([cos_c, cos_t], axis=0),
            jnp.concatenate([jnp.zeros((lc, 128), F32), sin_t], axis=0))


def _pad_lanes(w, n=128):
    return jnp.pad(w, (0, n - w.shape[0])).reshape(1, n)


def kernel(x, c, ctx, c_ctx, norm_w, ada_w, ada_b, w_in, gla_gate_w2, gla_gate_b, gla_norm_w, gdn_conv_w,
           gdn_a_log, gdn_dt_bias, gdn_norm_w, mla_q_norm_w, mla_q_up, mla_kv_norm_w, mla_kv_up,
           mla_qn_nope, mla_qn_rope, mla_kn_nope, mla_kn_rope, w_branch, w_out):
    b, t_lat, d = x.shape
    lc = ctx.shape[1]
    l = lc + t_lat
    depth = w_in.shape[0]
    assert d == D_MODEL and lc % 128 == 0 and t_lat % 128 == 0

    h = jnp.concatenate([ctx, x], axis=1)
    rows = ((b + 1 + 7) // 8) * 8
    c_all = jnp.concatenate([c, c_ctx[None, :], jnp.zeros((rows - b - 1, d), F32)], axis=0)
    mod = _ada_call(c_all, ada_w, ada_b)
    cos_t, sin_t = _rope_tables(t_lat, lc)

    for li in range(depth):
        last = li == depth - 1
        modx = mod[li, :b].reshape(b, 1, 3 * d)
        modc = mod[li, b:b + 1]
        w = w_in[li]
        w_mla = _cols(w, 'mla_q_down', 'mla_kv_down', 'mla_k_rope', 'gla_gate_lr', 'gdn_a', 'gdn_b').astype(BF16)
        w_gla = _cols(w, 'gla_q', 'gla_k', 'gla_v').astype(BF16)
        w_gdn = _cols(w, 'gdn_qkv').astype(BF16)
        w_zg = _cols(w, 'gla_z', 'gdn_z', 'mla_z', 'merge_gate').astype(BF16)

        hn = _prenorm_call(h, modx, modc, norm_w[li], lc).reshape(b * l, d)
        p_mla = _proj_call(hn, w_mla, F32, "proj_mla").reshape(b, l, -1)
        p_gla = _proj_call(hn, w_gla, F32, "proj_gla").reshape(b, l, -1)
        p_gdn = _proj_call(hn, w_gdn, F32, "proj_gdn").reshape(b, l, -1)
        p_zg = _proj_call(hn, w_zg, BF16, "proj_zg").reshape(b, l, -1)

        w2e = jnp.stack([jnp.pad(gla_gate_w2[li, dd], ((SM_LR + dd * GLA_RANK, 128 - SM_LR - (dd + 1) * GLA_RANK),
                                                      (0, 0))) for dd in range(2)])
        oa = _gla_call(p_gla, p_mla, w2e, gla_gate_b[li].reshape(2, 1, -1), lc)

        qkvn = _gdn_prep_call(p_gdn, gdn_conv_w[li], lc)
        abc, abr = _gdn_gate_layouts(p_mla, lc)
        ob = _gdn_call(qkvn, abc, abr, gdn_a_log[li], gdn_dt_bias[li], lc)

        qup = mla_q_up[li].reshape(MLA_Q_LORA, MLA_HEADS, MLA_NOPE + MLA_ROPE)
        qup = jnp.pad(qup, ((0, 0), (0, 0), (0, 256 - MLA_NOPE - MLA_ROPE))).reshape(MLA_Q_LORA, -1).astype(BF16)
        kvup = mla_kv_up[li].reshape(MLA_KV_LORA, MLA_HEADS, MLA_NOPE + MLA_DV)
        kvup = jnp.concatenate([kvup[:, :, :MLA_NOPE].reshape(MLA_KV_LORA, -1),
                                kvup[:, :, MLA_NOPE:].reshape(MLA_KV_LORA, -1)], axis=1).astype(BF16)
        qm, km, vm = _mla_prep_call(
            p_mla, cos_t, sin_t, mla_q_norm_w[li].reshape(1, -1), qup, mla_kv_norm_w[li].reshape(1, -1), kvup,
            mla_qn_nope[li].reshape(1, -1), _pad_lanes(mla_qn_rope[li]),
            mla_kn_nope[li].reshape(1, -1), _pad_lanes(mla_kn_rope[li]))
        oc = _mla_attn_call(qm, km, vm, lc, with_ctx=not last)

        h = _merge_call(h, oa, ob, oc, p_zg, modx, modc,
                        jnp.tile(gla_norm_w[li], GLA_HEADS).reshape(1, -1),
                        jnp.tile(gdn_norm_w[li], GDN_HEADS).reshape(1, -1),
                        w_branch[li].astype(BF16), w_out[li].astype(BF16), lc, with_ctx=not last)
    return h
```

```python
import functools
import math

import numpy as np
import jax
import jax.numpy as jnp
from jax import lax
from jax.experimental import pallas as pl
from jax.experimental.pallas import tpu as pltpu

F32 = jnp.float32
BF16 = jnp.bfloat16

D_MODEL = 1024
EPS = 1e-6
GRID_W = 64
CHUNK = 64

GLA_HEADS, GLA_DK, GLA_DV, GLA_RANK, GLA_NORMALIZER = 4, 128, 256, 16, 16.0
GDN_HEADS, GDN_DK, GDN_DV, GDN_CONV = 8, 128, 128, 5
GLA_HB = 4
GDN_HB = 8
_GDN_GATE_ROWS = max(8, 2 * GDN_HB)
MLA_HEADS, MLA_Q_LORA, MLA_KV_LORA, MLA_NOPE, MLA_ROPE, MLA_DV = 8, 384, 256, 128, 64, 128
MLA_SCALE = (MLA_NOPE + MLA_ROPE) ** -0.5
MLA_Q_PRESCALE = MLA_SCALE * math.log2(math.e)
MLA_HP = 8
ROPE_FREQ = MLA_ROPE // 4
ROPE_BASE = 10000.0
BRANCH_W = 1024

_IN_SIZES = (512, 512, 1024, 32, 1024, 3072, 16, 16, 1024, 384, 256, 64, 1024, 3072)
_IN_NAMES = ('gla_q', 'gla_k', 'gla_v', 'gla_gate_lr', 'gla_z', 'gdn_qkv', 'gdn_a', 'gdn_b', 'gdn_z',
             'mla_q_down', 'mla_kv_down', 'mla_k_rope', 'mla_z', 'merge_gate')
_IN_OFF = dict(zip(_IN_NAMES, np.concatenate([[0], np.cumsum(_IN_SIZES)[:-1]]).tolist()))
_IN_LEN = dict(zip(_IN_NAMES, _IN_SIZES))

SM_ROPE, SM_LR, SM_A, SM_B = 0, 64, 96, 112
MLA_GROUP_W = MLA_Q_LORA + MLA_KV_LORA + 128
SMALL_BLOCK = (MLA_Q_LORA + MLA_KV_LORA) // 128

VMEM_LIMIT = 56 * 1024 * 1024


def _params(sem):
    return pltpu.CompilerParams(dimension_semantics=sem, vmem_limit_bytes=VMEM_LIMIT)


def _pick(n, cands):
    for c in cands:
        if n % c == 0:
            return c
    raise ValueError(f"no tile for {n} in {cands}")


def _mm(a, b):
    return jnp.dot(a.astype(BF16), b.astype(BF16), preferred_element_type=F32)


def _mm_nt(a, b):
    return lax.dot_general(a.astype(BF16), b.astype(BF16), (((1,), (1,)), ((), ())),
                           preferred_element_type=F32)


def _mm_tn(a, b):
    return lax.dot_general(a.astype(BF16), b.astype(BF16), (((0,), (0,)), ((), ())),
                           preferred_element_type=F32)


def _mm_hp(a, b):
    return jnp.dot(a, b, precision=lax.Precision.HIGHEST, preferred_element_type=F32)


def _split(a):
    hi = a.astype(BF16)
    lo = (a - hi.astype(F32)).astype(BF16)
    return hi, lo


def _mm3(a, b):
    ah, al = _split(a)
    bh, bl = _split(b)
    d = functools.partial(jnp.dot, preferred_element_type=F32)
    return d(ah, bh) + (d(ah, bl) + d(al, bh))


_MMX = _mm
_TRI_BLOCK = 8


def _cum_rows(cum, x):
    cb = cum.astype(BF16)
    d = functools.partial(jnp.dot, preferred_element_type=F32)
    hi, lo = _split(x)
    return d(cb, hi) + d(cb, lo)


def _cum_cols(x, cum):
    cb = cum.astype(BF16)
    d = functools.partial(jnp.dot, preferred_element_type=F32)
    hi, lo = _split(x)
    return d(hi, cb) + d(lo, cb)


def _softplus(x):
    return jnp.maximum(x, 0.0) + jnp.log(1.0 + jnp.exp(-jnp.abs(x)))


def _log_sigmoid(x):
    return jnp.minimum(x, 0.0) - jnp.log(1.0 + jnp.exp(-jnp.abs(x)))


def _silu(x):
    return x * jax.nn.sigmoid(x)


def _tri_masks(n, reverse):
    r = lax.broadcasted_iota(jnp.int32, (n, n), 0)
    c = lax.broadcasted_iota(jnp.int32, (n, n), 1)
    incl = (c >= r) if reverse else (c <= r)
    strict = (c > r) if reverse else (c < r)
    return r, c, incl, strict


def _ada_kernel(c_ref, w_ref, b_ref, o_ref):
    o_ref[0] = _mm_hp(_silu(c_ref[...]), w_ref[0]) + b_ref[0]


def _ada_call(c_all, ada_w, ada_b):
    nl, d, n3 = ada_w.shape
    r = c_all.shape[0]
    tn = 1024
    return pl.pallas_call(
        _ada_kernel,
        grid=(nl, n3 // tn),
        in_specs=[pl.BlockSpec((r, d), lambda l, j: (0, 0)),
                  pl.BlockSpec((1, d, tn), lambda l, j: (l, 0, j)),
                  pl.BlockSpec((1, 1, tn), lambda l, j: (l, 0, j))],
        out_specs=pl.BlockSpec((1, r, tn), lambda l, j: (l, 0, j)),
        out_shape=jax.ShapeDtypeStruct((nl, r, n3), F32),
        compiler_params=_params(("arbitrary", "arbitrary")),
        name="ada_rows",
    )(c_all, ada_w, ada_b.reshape(nl, 1, n3))


def _row_select(t, tm, lc, ctx_row, lat_row):
    rows = t * tm + lax.broadcasted_iota(jnp.int32, (tm, 1), 0)
    return jnp.where(rows < lc, ctx_row, lat_row)


def _prenorm_kernel(h_ref, modx_ref, modc_ref, nw_ref, o_ref, *, lc, tm):
    t = pl.program_id(1)
    x = h_ref[0]
    y = x * lax.rsqrt(jnp.mean(x * x, axis=-1, keepdims=True) + EPS) * nw_ref[...]
    mx = modx_ref[0]
    mc = modc_ref[...]
    d = D_MODEL
    shift = _row_select(t, tm, lc, mc[:, :d], mx[:, :d])
    scale = _row_select(t, tm, lc, mc[:, d:2 * d], mx[:, d:2 * d])
    o_ref[0] = (y * (1.0 + scale) + shift).astype(o_ref.dtype)


def _prenorm_call(h, modx, modc, norm_w, lc):
    b, l, d = h.shape
    tm = _pick(l, (768, 512, 384, 256, 128))
    return pl.pallas_call(
        functools.partial(_prenorm_kernel, lc=lc, tm=tm),
        grid=(b, l // tm),
        in_specs=[pl.BlockSpec((1, tm, d), lambda i, t: (i, t, 0)),
                  pl.BlockSpec((1, 1, 3 * d), lambda i, t: (i, 0, 0)),
                  pl.BlockSpec((1, 3 * d), lambda i, t: (0, 0)),
                  pl.BlockSpec((1, d), lambda i, t: (0, 0))],
        out_specs=pl.BlockSpec((1, tm, d), lambda i, t: (i, t, 0)),
        out_shape=jax.ShapeDtypeStruct((b, l, d), BF16),
        compiler_params=_params(("parallel", "parallel")),
        name="prenorm",
    )(h, modx, modc, norm_w.reshape(1, d))


def _proj_kernel(x_ref, w_ref, o_ref):
    o_ref[...] = jnp.dot(x_ref[...], w_ref[...], preferred_element_type=F32).astype(o_ref.dtype)


def _proj_call(x, w, out_dtype, name):
    m, k = x.shape
    n = w.shape[1]
    tm = _pick(m, (1024, 768, 512, 384, 256, 128))
    tn = _pick(n, (1536, 1024, 768, 512, 256, 128))
    return pl.pallas_call(
        _proj_kernel,
        grid=(m // tm, n // tn),
        in_specs=[pl.BlockSpec((tm, k), lambda i, j: (i, 0)),
                  pl.BlockSpec((k, tn), lambda i, j: (0, j))],
        out_specs=pl.BlockSpec((tm, tn), lambda i, j: (i, j)),
        out_shape=jax.ShapeDtypeStruct((m, n), out_dtype),
        compiler_params=_params(("parallel", "parallel")),
        name=name,
    )(x, w)


def _bwd_chunk(i, ncc, nc):
    return jnp.where(i < ncc, ncc - 1 - i, ncc + nc - 1 - i)


def _chunk_rows(c):
    return pl.ds(pl.multiple_of(c * CHUNK, CHUNK), CHUNK)


def _gla_kernel(q_ref, k_ref, v_ref, sm_ref, w2_ref, gb_ref, o_ref, la_ref, st_ref, *, ncc, nc):
    sm = sm_ref[0]
    for d in range(2):
        logit = _mm3(sm, w2_ref[d]) + gb_ref[d]
        la_ref[d] = _log_sigmoid(logit) * (1.0 / GLA_NORMALIZER)
    st_ref[...] = jnp.zeros_like(st_ref)
    o_ref[...] = jnp.zeros_like(o_ref)
    masks = [_tri_masks(CHUNK, rev) for rev in (False, True)]
    cum_mats = [m[2].astype(F32) for m in masks]
    probs = [(d, hh) for d in range(2) for hh in range(GLA_HB)]

    def body(i, carry):
        rows = [_chunk_rows(i), _chunk_rows(_bwd_chunk(i, ncc, nc))]
        bcum_d = [_cum_rows(cum_mats[d], la_ref[d, rows[d], :]) for d in range(2)]
        kl = [slice(hh * GLA_DK, (hh + 1) * GLA_DK) for _, hh in probs]
        vl = [slice(hh * GLA_DV, (hh + 1) * GLA_DV) for _, hh in probs]
        bcum = [bcum_d[d][:, ln] for (d, _), ln in zip(probs, kl)]
        b_last = [bc[CHUNK - 1:CHUNK] if d == 0 else bc[0:1] for (d, _), bc in zip(probs, bcum)]
        q = [q_ref[0, rows[d], ln].astype(F32) * GLA_DK ** -0.5 for (d, _), ln in zip(probs, kl)]
        k = [k_ref[0, rows[d], ln].astype(F32) for (d, _), ln in zip(probs, kl)]
        v = [v_ref[0, rows[d], ln].astype(F32) for (d, _), ln in zip(probs, vl)]
        q_dec = [a * jnp.exp(bc) for a, bc in zip(q, bcum)]
        k_inv = [a * jnp.exp(-bc) for a, bc in zip(k, bcum)]
        k_end = [a * jnp.exp(bl - bc) for a, bl, bc in zip(k, b_last, bcum)]
        st = [st_ref[j] for j in range(len(probs))]
        qk = _each(_mm_nt, q_dec, k_inv)
        o_inter = _each(_mm_nt, q_dec, st)
        upd = _each(_mm_tn, v, k_end)
        att = [jnp.where(masks[d][2], a, 0.0) for (d, _), a in zip(probs, qk)]
        o_intra = _each(_mm, att, v)
        for j, (sx, bl, up) in enumerate(zip(st, b_last, upd)):
            st_ref[j] = sx * jnp.exp(bl) + up
        for (d, _), ln, a, b in zip(probs, vl, o_intra, o_inter):
            o_ref[0, rows[d], ln] += a + b
        return carry

    lax.fori_loop(0, nc, body, 0)


def _gla_call(p_gla, p_mla, w2e, gate_b, lc):
    b, l, _ = p_gla.shape
    nc, ncc = l // CHUNK, lc // CHUNK
    nhb = GLA_HEADS // GLA_HB
    wk, wv = GLA_HB * GLA_DK, GLA_HB * GLA_DV
    return pl.pallas_call(
        functools.partial(_gla_kernel, ncc=ncc, nc=nc),
        grid=(b, nhb),
        in_specs=[pl.BlockSpec((1, l, wk), lambda i, h: (i, 0, h)),
                  pl.BlockSpec((1, l, wk), lambda i, h: (i, 0, nhb + h)),
                  pl.BlockSpec((1, l, wv), lambda i, h: (i, 0, nhb + h)),
                  pl.BlockSpec((1, l, 128), lambda i, h: (i, 0, SMALL_BLOCK)),
                  pl.BlockSpec((2, 128, wk), lambda i, h: (0, 0, h)),
                  pl.BlockSpec((2, 1, wk), lambda i, h: (0, 0, h))],
        out_specs=pl.BlockSpec((1, l, wv), lambda i, h: (i, 0, h)),
        out_shape=jax.ShapeDtypeStruct((b, l, GLA_HEADS * GLA_DV), F32),
        scratch_shapes=[pltpu.VMEM((2, l, wk), F32), pltpu.VMEM((2 * GLA_HB, GLA_DV, GLA_DK), F32)],
        compiler_params=_params(("parallel", "parallel")),
        name="gla_scan",
    )(p_gla, p_gla, p_gla, p_mla, w2e, gate_b)


def _gdn_prep_kernel(x_ref, w_ref, o_ref, *, lc, l):
    j = pl.program_id(1)
    x = x_ref[0].astype(F32)
    w = w_ref[...]
    t = lax.broadcasted_iota(jnp.int32, (l, 128), 0)
    start = jnp.where(t < lc, 0, lc)
    end = jnp.where(t < lc, lc, l)
    half = GDN_CONV // 2
    acc = x * w[half:half + 1]
    for s in range(-half, half + 1):
        if s == 0:
            continue
        xs = pltpu.roll(x, (-s) % l, axis=0)
        ok = (t + s >= start) if s < 0 else (t + s < end)
        acc = acc + jnp.where(ok, xs, 0.0) * w[s + half:s + half + 1]
    y = _silu(acc)
    nblk_head = GDN_HEADS * GDN_DK // 128
    inv = lax.rsqrt(jnp.sum(y * y, axis=-1, keepdims=True) + EPS)
    fac = jnp.where(j < nblk_head, inv * GDN_DK ** -0.5, jnp.where(j < 2 * nblk_head, inv, 1.0))
    o_ref[0] = (y * fac).astype(o_ref.dtype)


def _gdn_prep_call(p_gdn, conv_w, lc):
    b, l, n = p_gdn.shape
    return pl.pallas_call(
        functools.partial(_gdn_prep_kernel, lc=lc, l=l),
        grid=(b, n // 128),
        in_specs=[pl.BlockSpec((1, l, 128), lambda i, j: (i, 0, j)),
                  pl.BlockSpec((GDN_CONV, 128), lambda i, j: (0, j))],
        out_specs=pl.BlockSpec((1, l, 128), lambda i, j: (i, 0, j)),
        out_shape=jax.ShapeDtypeStruct((b, l, n), BF16),
        compiler_params=_params(("parallel", "parallel")),
        name="gdn_prep",
    )(p_gdn, conv_w)


def _each(fn, *lists):
    return [fn(*args) for args in zip(*lists)]


def _unit_tri_solves(lms, rhs, r, c):
    eye = (r == c).astype(F32)
    same = lambda s: (r // s) == (c // s)
    nb = _TRI_BLOCK
    y = [-jnp.where(same(nb), lm, 0.0) for lm in lms]
    t = [eye + a for a in y]
    p = y
    for _ in range(int(math.log2(nb)) - 1):
        p = _each(_MMX, p, p)
        t = _each(lambda a, b: a + _MMX(a, b), t, p)
    sizes = [nb * 2 ** j for j in range(int(math.log2(CHUNK // nb)))]
    nlev = len(sizes)
    w = rhs[0].shape[1]
    pad = jnp.zeros((CHUNK, 128 - CHUNK), F32)
    offs = [[jnp.where(same(2 * s), jnp.where(same(s), 0.0, lm), 0.0) for s in reversed(sizes)] for lm in lms]
    wide = [jnp.concatenate([x] + [part for o in off for part in (o, pad)], axis=1) for x, off in zip(rhs, offs)]
    z = _each(_mm, t, wide)
    for j in range(nlev):
        keep = w + 128 * (nlev - 1 - j)
        m = [x[:, keep:keep + CHUNK] for x in z]
        z = [x[:, :keep] for x in z]
        z = _each(lambda x, mj: x - _mm(mj, x), z, m)
    return z


def _gdn_kernel(rate_ref, ratel_ref, q_ref, k_ref, v_ref, abc_ref, abr_ref, o_ref, s_ref, *, ncc, nc):
    na = 2 * GDN_HB
    s_ref[...] = jnp.zeros_like(s_ref)
    o_ref[...] = jnp.zeros_like(o_ref)
    masks = [_tri_masks(CHUNK, rev) for rev in (False, True)]
    cum_mats = [m[2].astype(F32) for m in masks]
    r, cc = masks[0][0], masks[0][1]
    probs = [(d, hh) for d in range(2) for hh in range(GDN_HB)]
    neg_rate_rows = -jnp.exp(rate_ref[0, 0])[:, :CHUNK]
    dtb_rows = rate_ref[0, 1][:, :CHUNK]
    neg_rate_lanes = -jnp.exp(ratel_ref[0, 0, 0:1, :])
    dtb_lanes = ratel_ref[0, 1, 0:1, :]

    def body(i, carry):
        dirs = (0, 1)
        chunk = [i, _bwd_chunk(i, ncc, nc)]
        rows = [_chunk_rows(c) for c in chunk]
        abc = [abc_ref[0, 0, rw, :] for rw in rows]
        abr = [abr_ref[0, 0, c] for c in chunk]
        incl = [masks[d][2] for d, _ in probs]
        strict = [masks[d][3] for d, _ in probs]
        g_cols = [neg_rate_lanes * _softplus(abc[d] + dtb_lanes) for d in dirs]
        g_rows = [neg_rate_rows * _softplus(abr[d] + dtb_rows) for d in dirs]
        gc_cols = [_cum_rows(cum_mats[d], g_cols[d]) for d in dirs]
        gc_rows = [_cum_cols(g_rows[d], cum_mats[1 - d]) for d in dirs]
        beta_cols = [jax.nn.sigmoid(abc[d]) for d in dirs]
        col = [d * GDN_HB + hh for d, hh in probs]
        gc_col = [jnp.broadcast_to(gc_cols[d][:, j:j + 1], (CHUNK, 128)) for (d, _), j in zip(probs, col)]
        gc_row = [gc_rows[d][j:j + 1, :] for (d, _), j in zip(probs, col)]
        beta_col = [beta_cols[d][:, na + j:na + j + 1] for (d, _), j in zip(probs, col)]
        g_last =[gc[CHUNK - 1:CHUNK] if d == 0 else gc[0:1] for (d, _), gc in zip(probs, gc_col)]
        decay = [jnp.where(m, jnp.exp(jnp.where(m, a[:, :CHUNK] - b, 0.0)), 0.0)
                 for m, a, b in zip(incl, gc_col, gc_row)]

        lanes = [slice(hh * GDN_DK, (hh + 1) * GDN_DK) for _, hh in probs]
        q = [q_ref[0, rows[d], ln].astype(F32) for (d, _), ln in zip(probs, lanes)]
        k = [k_ref[0, rows[d], ln].astype(F32) for (d, _), ln in zip(probs, lanes)]
        v = [v_ref[0, rows[d], ln].astype(F32) for (d, _), ln in zip(probs, lanes)]
        k_beta = _each(jnp.multiply, k, beta_col)
        v_beta = _each(jnp.multiply, v, beta_col)
        e_col = _each(jnp.exp, gc_col)
        kq = _each(_mm_nt, [jnp.concatenate([a, b], axis=0) for a, b in zip(k_beta, q)], k)
        kk = [x[:CHUNK] for x in kq]
        qk = [x[CHUNK:] for x in kq]
        lm =[jnp.where(m, a * dc, 0.0) for m, a, dc in zip(strict, kk, decay)]
        rhs = [jnp.concatenate([vb, kb * e], axis=1) for vb, kb, e in zip(v_beta, k_beta, e_col)]
        uw = _unit_tri_solves(lm, rhs, r, cc)
        u = [x[:, :GDN_DV] for x in uw]
        w = [x[:, GDN_DV:] for x in uw]
        att =[jnp.where(m, a * dc, 0.0) for m, a, dc in zip(incl, qk, decay)]
        q_dec = _each(jnp.multiply, q, e_col)
        k_end = [kx * jnp.exp(gl - gc) for kx, gl, gc in zip(k, g_last, gc_col)]
        s = [s_ref[d * GDN_HB + hh] for d, hh in probs]
        wq = _each(_mm, [jnp.concatenate([a, b], axis=0) for a, b in zip(w, q_dec)], s)
        ws = [x[:CHUNK] for x in wq]
        o_inter = [x[CHUNK:] for x in wq]
        v_new =_each(jnp.subtract, u, ws)
        upd = _each(_mm_tn, k_end, v_new)
        o_intra = _each(_mm, att, v_new)
        for (d, hh), sx, gl, up in zip(probs, s, g_last, upd):
            s_ref[d * GDN_HB + hh] = sx * jnp.exp(gl) + up
        for (d, _), ln, a, b in zip(probs, lanes, o_intra, o_inter):
            o_ref[0, rows[d], ln] += a + b
        return carry

    lax.fori_loop(0, nc, body, 0)


def _gdn_call(qkvn, abc, abr, a_log, dt_bias, lc):
    b, l, _ = qkvn.shape
    nc, ncc = l // CHUNK, lc // CHUNK
    nhb = GDN_HEADS // GDN_HB
    wb = GDN_HB * GDN_DK
    na = 2 * GDN_HB
    rate = jnp.stack([a_log, dt_bias]).reshape(2, 2, nhb, GDN_HB).transpose(2, 0, 1, 3).reshape(nhb, 2, na)
    rate = jnp.pad(rate, ((0, 0), (0, 0), (0, 128 - na)))
    nr = _GDN_GATE_ROWS
    rate_rows = jnp.broadcast_to(rate[:, :, :nr, None], (nhb, 2, nr, 128))
    rate_lanes = jnp.broadcast_to(rate[:, :, None, :], (nhb, 2, 8, 128))
    return pl.pallas_call(
        functools.partial(_gdn_kernel, ncc=ncc, nc=nc),
        grid=(b, nhb),
        in_specs=[pl.BlockSpec((1, 2, nr, 128), lambda i, h: (h, 0, 0, 0)),
                  pl.BlockSpec((1, 2, 8, 128), lambda i, h: (h, 0, 0, 0)),
                  pl.BlockSpec((1, l, wb), lambda i, h: (i, 0, h)),
                  pl.BlockSpec((1, l, wb), lambda i, h: (i, 0, nhb + h)),
                  pl.BlockSpec((1, l, wb), lambda i, h: (i, 0, 2 * nhb + h)),
                  pl.BlockSpec((1, 1, l, 128), lambda i, h: (i, h, 0, 0)),
                  pl.BlockSpec((1, 1, nc, nr, CHUNK), lambda i, h: (i, h, 0, 0, 0))],
        out_specs=pl.BlockSpec((1, l, wb), lambda i, h: (i, 0, h)),
        out_shape=jax.ShapeDtypeStruct((b, l, GDN_HEADS * GDN_DV), F32),
        scratch_shapes=[pltpu.VMEM((2 * GDN_HB, GDN_DK, GDN_DV), F32)],
        compiler_params=_params(("parallel", "parallel")),
        name="gdn_scan",
    )(rate_rows, rate_lanes, qkvn, qkvn, qkvn, abc, abr)


def _gdn_gate_layouts(p_mla, lc):
    b, l, _ = p_mla.shape
    nhb = GDN_HEADS // GDN_HB
    na = 2 * GDN_HB
    o = MLA_Q_LORA + MLA_KV_LORA
    ab = p_mla[:, :, o + SM_A:o + SM_A + 32].reshape(b, l, 2, 2, nhb, GDN_HB)
    ab = ab.transpose(0, 4, 1, 2, 3, 5).reshape(b, nhb, l, 2 * na)
    abr = ab[..., :na].reshape(b, nhb, l // CHUNK, CHUNK, na).transpose(0, 1, 2, 4, 3)
    abr = jnp.pad(abr, ((0, 0), (0, 0), (0, 0), (0, _GDN_GATE_ROWS - na), (0, 0)))
    abc = jnp.pad(ab, ((0, 0), (0, 0), (0, 0), (0, 128 - 2 * na)))
    return abc, abr


def _rope_swap(x):
    lane = lax.broadcasted_iota(jnp.int32, x.shape, 1)
    n = x.shape[1]
    return jnp.where(lane % 32 < 16, pltpu.roll(x, n - 16, axis=1), pltpu.roll(x, 16, axis=1))


def _mla_prep_kernel(p_ref, cos_ref, sin_ref, qnw_ref, qup_ref, kvnw_ref, kvup_ref,
                     wqn_ref, wqr_ref, wkn_ref, wkr_ref, q_ref, k_ref, v_ref):
    p = p_ref[0]
    cos_t = cos_ref[...]
    sin_t = sin_ref[...]

    def rms(x, n):
        return x * lax.rsqrt(jnp.sum(x * x, axis=-1, keepdims=True) * (1.0 / n) + EPS)

    def rope(x):
        return x * cos_t + _rope_swap(x) * sin_t

    qd = p[:, :MLA_Q_LORA]
    kvd = p[:, MLA_Q_LORA:MLA_Q_LORA + MLA_KV_LORA]
    sm = p[:, MLA_Q_LORA + MLA_KV_LORA:]
    q = _mm(rms(qd, MLA_Q_LORA) * qnw_ref[...], qup_ref[...])
    kv = _mm(rms(kvd, MLA_KV_LORA) * kvnw_ref[...], kvup_ref[...])
    lane = lax.broadcasted_iota(jnp.int32, sm.shape, 1)
    kr = jnp.where(lane < MLA_ROPE, sm, 0.0)
    kr = rope(rms(kr, MLA_ROPE) * wkr_ref[...])
    for h in range(MLA_HEADS):
        base = h * 256
        qn = rms(q[:, base:base + 128], MLA_NOPE) * wqn_ref[...]
        q_ref[0, :, base:base + 128] = (qn * MLA_Q_PRESCALE).astype(q_ref.dtype)
        qr = rms(q[:, base + 128:base + 256], MLA_ROPE) * wqr_ref[...]
        q_ref[0, :, base + 128:base + 256] = (rope(qr) * MLA_Q_PRESCALE).astype(q_ref.dtype)
        kn = kv[:, h * 128:(h + 1) * 128]
        k_ref[0, :, base:base + 128] = (rms(kn, MLA_NOPE) * wkn_ref[...]).astype(k_ref.dtype)
        k_ref[0, :, base + 128:base + 256] = kr.astype(k_ref.dtype)
    v_ref[0] = kv[:, MLA_HEADS * MLA_NOPE:].T.astype(v_ref.dtype)


def _mla_prep_call(p_mla, cos_t, sin_t, qnw, qup, kvnw, kvup, wqn, wqr, wkn, wkr):
    b, l, gw = p_mla.shape
    tm = _pick(l, (256, 128))
    hq = MLA_HEADS * 256
    full = lambda shape: pl.BlockSpec(shape, lambda i, t: tuple(0 for _ in shape))
    return pl.pallas_call(
        _mla_prep_kernel,
        grid=(b, l // tm),
        in_specs=[pl.BlockSpec((1, tm, gw), lambda i, t: (i, t, 0)),
                  pl.BlockSpec((tm, 128), lambda i, t: (t, 0)),
                  pl.BlockSpec((tm, 128), lambda i, t: (t, 0)),
                  full((1, MLA_Q_LORA)), full((MLA_Q_LORA, hq)),
                  full((1, MLA_KV_LORA)), full((MLA_KV_LORA, hq)),
                  full((1, 128)), full((1, 128)), full((1, 128)), full((1, 128))],
        out_specs=[pl.BlockSpec((1, tm, hq), lambda i, t: (i, t, 0)),
                   pl.BlockSpec((1, tm, hq), lambda i, t: (i, t, 0)),
                   pl.BlockSpec((1, MLA_HEADS * MLA_DV, tm), lambda i, t: (i, 0, t))],
        out_shape=[jax.ShapeDtypeStruct((b, l, hq), BF16),
                   jax.ShapeDtypeStruct((b, l, hq), BF16),
                   jax.ShapeDtypeStruct((b, MLA_HEADS * MLA_DV, l), BF16)],
        compiler_params=_params(("parallel", "parallel")),
        name="mla_prep",
    )(p_mla, cos_t, sin_t, qnw, qup, kvnw, kvup, wqn, wqr, wkn, wkr)


def _mla_attn_kernel(q_ref, k_ref, vt_ref, o_ref, *, lc, tq, q_off):
    qi = pl.program_id(2) + q_off
    heads = range(MLA_HP)

    def attend(nk):
        st = [lax.dot_general(k_ref[0, :nk, h * 256:(h + 1) * 256], q_ref[0, :, h * 256:(h + 1) * 256],
                              (((1,), (1,)), ((), ())), preferred_element_type=F32) for h in heads]
        e = [jnp.exp2(x - jnp.max(x, axis=0, keepdims=True)) for x in st]
        den = [jnp.sum(x, axis=0, keepdims=True) for x in e]
        ot = [jnp.dot(vt_ref[0, h * MLA_DV:(h + 1) * MLA_DV, :nk], e[h].astype(BF16),
                      preferred_element_type=F32) for h in heads]
        for h in heads:
            o_ref[0, :, h * MLA_DV:(h + 1) * MLA_DV] = (ot[h] / den[h]).T.astype(o_ref.dtype)

    if q_off == 0:
        @pl.when(qi * tq < lc)
        def _():
            attend(lc)

        @pl.when(qi * tq >= lc)
        def _():
            attend(k_ref.shape[1])
    else:
        attend(k_ref.shape[1])


def _mla_attn_call(q, k, vt, lc, with_ctx):
    b, l, _ = q.shape
    tq = _pick(lc, (256, 128))
    q_off = 0 if with_ctx else lc // tq
    nq = l // tq - q_off
    return pl.pallas_call(
        functools.partial(_mla_attn_kernel, lc=lc, tq=tq, q_off=q_off),
        grid=(b, MLA_HEADS // MLA_HP, nq),
        in_specs=[pl.BlockSpec((1, tq, MLA_HP * 256), lambda i, h, t: (i, t + q_off, h)),
                  pl.BlockSpec((1, l, MLA_HP * 256), lambda i, h, t: (i, 0, h)),
                  pl.BlockSpec((1, MLA_HP * MLA_DV, l), lambda i, h, t: (i, h, 0))],
        out_specs=pl.BlockSpec((1, tq, MLA_HP * MLA_DV), lambda i, h, t: (i, t, h)),
        out_shape=jax.ShapeDtypeStruct((b, nq * tq, MLA_HEADS * MLA_DV), F32),
        compiler_params=_params(("parallel", "parallel", "arbitrary")),
        name="mla_attn",
    )(q, k, vt)


def _merge_kernel(h_ref, oa_ref, ob_ref, oc_ref, za_ref, zb_ref, zc_ref, g0_ref, g1_ref, g2_ref,
                  modx_ref, modc_ref, wa_ref, wb_ref, wbr_ref, wout_ref, o_ref, *, lc, tm, t_off):
    t = pl.program_id(1) + t_off

    def head_norm(o, width):
        parts = []
        for h in range(BRANCH_W // width):
            x = o[:, h * width:(h + 1) * width]
            parts.append(x * lax.rsqrt(jnp.mean(x * x, axis=-1, keepdims=True) + EPS))
        return jnp.concatenate(parts, axis=-1)

    ya = head_norm(oa_ref[0], GLA_DV) * wa_ref[...] * _silu(za_ref[0].astype(F32))
    yb = head_norm(ob_ref[0], GDN_DV) * wb_ref[...] * _silu(zb_ref[0].astype(F32))
    yc = oc_ref[0] * _silu(zc_ref[0].astype(F32))
    acc = jax.nn.sigmoid(g0_ref[0].astype(F32)) * _mm(ya, wbr_ref[0])
    acc = acc + jax.nn.sigmoid(g1_ref[0].astype(F32)) * _mm(yb, wbr_ref[1])
    acc = acc + jax.nn.sigmoid(g2_ref[0].astype(F32)) * _mm(yc, wbr_ref[2])
    d = D_MODEL
    gate = _row_select(t, tm, lc, modc_ref[:, 2 * d:], modx_ref[0][:, 2 * d:])
    o_ref[0] = h_ref[0] + gate * _mm(acc, wout_ref[...])


def _merge_call(h, oa, ob, oc, p_zg, modx, modc, wa, wb, w_branch, w_out, lc, with_ctx):
    b, l, d = h.shape
    tm = _pick(lc, (256, 128))
    t_off = 0 if with_ctx else lc // tm
    nt = l // tm - t_off
    oc_off = 0 if oc.shape[1] == l else lc // tm
    row = lambda i, t: (i, t + t_off, 0)
    zg = lambda n: pl.BlockSpec((1, tm, d), lambda i, t: (i, t + t_off, n))
    return pl.pallas_call(
        functools.partial(_merge_kernel, lc=lc, tm=tm, t_off=t_off),
        grid=(b, nt),
        in_specs=[pl.BlockSpec((1, tm, d), row), pl.BlockSpec((1, tm, d), row), pl.BlockSpec((1, tm, d), row),
                  pl.BlockSpec((1, tm, d), lambda i, t: (i, t + t_off - oc_off, 0)),
                  zg(0), zg(1), zg(2), zg(3), zg(4), zg(5),
                  pl.BlockSpec((1, 1, 3 * d), lambda i, t: (i, 0, 0)),
                  pl.BlockSpec((1, 3 * d), lambda i, t: (0, 0)),
                  pl.BlockSpec((1, d), lambda i, t: (0, 0)),
                  pl.BlockSpec((1, d), lambda i, t: (0, 0)),
                  pl.BlockSpec((3, d, d), lambda i, t: (0, 0, 0)),
                  pl.BlockSpec((d, d), lambda i, t: (0, 0))],
        out_specs=pl.BlockSpec((1, tm, d), lambda i, t: (i, t, 0)),
        out_shape=jax.ShapeDtypeStruct((b, nt * tm, d), F32),
        compiler_params=_params(("parallel", "parallel")),
        name="merge",
    )(h, oa, ob, oc, p_zg, p_zg, p_zg, p_zg, p_zg, p_zg, modx, modc, wa, wb, w_branch, w_out)


def _cols(w, *names):
    return jnp.concatenate([w[:, _IN_OFF[n]:_IN_OFF[n] + _IN_LEN[n]] for n in names], axis=1)


def _rope_tables(t_lat, lc):
    rows = t_lat // GRID_W
    row = jnp.repeat(jnp.arange(rows, dtype=F32), GRID_W)
    col = jnp.tile(jnp.arange(GRID_W, dtype=F32), rows)
    inv_freq = jnp.power(ROPE_BASE, -jnp.arange(ROPE_FREQ, dtype=F32) / ROPE_FREQ)
    ar = row[:, None] * inv_freq
    ac = col[:, None] * inv_freq
    zeros = jnp.zeros((t_lat, 128 - MLA_ROPE), F32)
    cos_t = jnp.concatenate([jnp.cos(ar), jnp.cos(ar), jnp.cos(ac), jnp.cos(ac), zeros], axis=1)
    sin_t = jnp.concatenate([-jnp.sin(ar), jnp.sin(ar), -jnp.sin(ac), jnp.sin(ac), zeros], axis=1)
    cos_c = jnp.concatenate([jnp.ones((lc, MLA_ROPE), F32), jnp.zeros((lc, 128 - MLA_ROPE), F32)], axis=1)
    return (jnp.concatenate([cos_c, cos_t], axis=0),
            jnp.concatenate([jnp.zeros((lc, 128), F32), sin_t], axis=0))


def _pad_lanes(w, n=128):
    return jnp.pad(w, (0, n - w.shape[0])).reshape(1, n)


def kernel(x, c, ctx, c_ctx, norm_w, ada_w, ada_b, w_in, gla_gate_w2, gla_gate_b, gla_norm_w, gdn_conv_w,
           gdn_a_log, gdn_dt_bias, gdn_norm_w, mla_q_norm_w, mla_q_up, mla_kv_norm_w, mla_kv_up,
           mla_qn_nope, mla_qn_rope, mla_kn_nope, mla_kn_rope, w_branch, w_out):
    b, t_lat, d = x.shape
    lc = ctx.shape[1]
    l = lc + t_lat
    depth = w_in.shape[0]
    assert d == D_MODEL and lc % 128 == 0 and t_lat % 128 == 0

    h = jnp.concatenate([ctx, x], axis=1)
    rows = ((b + 1 + 7) // 8) * 8
    c_all = jnp.concatenate([c, c_ctx[None, :], jnp.zeros((rows - b - 1, d), F32)], axis=0)
    mod = _ada_call(c_all, ada_w, ada_b)
    cos_t, sin_t = _rope_tables(t_lat, lc)

    for li in range(depth):
        last = li == depth - 1
        modx = mod[li, :b].reshape(b, 1, 3 * d)
        modc = mod[li, b:b + 1]
        w = w_in[li]
        w_mla = _cols(w, 'mla_q_down', 'mla_kv_down', 'mla_k_rope', 'gla_gate_lr', 'gdn_a', 'gdn_b').astype(BF16)
        w_gla = _cols(w, 'gla_q', 'gla_k', 'gla_v').astype(BF16)
        w_gdn = _cols(w, 'gdn_qkv').astype(BF16)
        w_zg = _cols(w, 'gla_z', 'gdn_z', 'mla_z', 'merge_gate').astype(BF16)

        hn = _prenorm_call(h, modx, modc, norm_w[li], lc).reshape(b * l, d)
        p_mla = _proj_call(hn, w_mla, F32, "proj_mla").reshape(b, l, -1)
        p_gla = _proj_call(hn, w_gla, BF16, "proj_gla").reshape(b, l, -1)
        p_gdn = _proj_call(hn, w_gdn, F32, "proj_gdn").reshape(b, l, -1)
        p_zg = _proj_call(hn, w_zg, BF16, "proj_zg").reshape(b, l, -1)

        w2e = jnp.stack([jnp.pad(gla_gate_w2[li, dd], ((SM_LR + dd * GLA_RANK, 128 - SM_LR - (dd + 1) * GLA_RANK),
                                                      (0, 0))) for dd in range(2)])
        oa = _gla_call(p_gla, p_mla, w2e, gla_gate_b[li].reshape(2, 1, -1), lc)

        qkvn = _gdn_prep_call(p_gdn, gdn_conv_w[li], lc)
        abc, abr = _gdn_gate_layouts(p_mla, lc)
        ob = _gdn_call(qkvn, abc, abr, gdn_a_log[li], gdn_dt_bias[li], lc)

        qup = mla_q_up[li].reshape(MLA_Q_LORA, MLA_HEADS, MLA_NOPE + MLA_ROPE)
        qup = jnp.pad(qup, ((0, 0), (0, 0), (0, 256 - MLA_NOPE - MLA_ROPE))).reshape(MLA_Q_LORA, -1).astype(BF16)
        kvup = mla_kv_up[li].reshape(MLA_KV_LORA, MLA_HEADS, MLA_NOPE + MLA_DV)
        kvup = jnp.concatenate([kvup[:, :, :MLA_NOPE].reshape(MLA_KV_LORA, -1),
                                kvup[:, :, MLA_NOPE:].reshape(MLA_KV_LORA, -1)], axis=1).astype(BF16)
        qm, km, vm = _mla_prep_call(
            p_mla, cos_t, sin_t, mla_q_norm_w[li].reshape(1, -1), qup, mla_kv_norm_w[li].reshape(1, -1), kvup,
            mla_qn_nope[li].reshape(1, -1), _pad_lanes(mla_qn_rope[li]),
            mla_kn_nope[li].reshape(1, -1), _pad_lanes(mla_kn_rope[li]))
        oc = _mla_attn_call(qm, km, vm, lc, with_ctx=not last)

        h = _merge_call(h, oa, ob, oc, p_zg, modx, modc,
                        jnp.tile(gla_norm_w[li], GLA_HEADS).reshape(1, -1),
                        jnp.tile(gdn_norm_w[li], GDN_HEADS).reshape(1, -1),
                        w_branch[li].astype(BF16), w_out[li].astype(BF16), lc, with_ctx=not last)
    return h
```

```python
import functools
import math

import numpy as np
import jax
import jax.numpy as jnp
from jax import lax
from jax.experimental import pallas as pl
from jax.experimental.pallas import tpu as pltpu

F32 = jnp.float32
BF16 = jnp.bfloat16

D_MODEL = 1024
EPS = 1e-6
GRID_W = 64
CHUNK = 64

GLA_HEADS, GLA_DK, GLA_DV, GLA_RANK, GLA_NORMALIZER = 4, 128, 256, 16, 16.0
GDN_HEADS, GDN_DK, GDN_DV, GDN_CONV = 8, 128, 128, 5
GLA_HB = 4
GDN_HB = 8
_GDN_GATE_ROWS = max(8, 2 * GDN_HB)
MLA_HEADS, MLA_Q_LORA, MLA_KV_LORA, MLA_NOPE, MLA_ROPE, MLA_DV = 8, 384, 256, 128, 64, 128
MLA_SCALE = (MLA_NOPE + MLA_ROPE) ** -0.5
MLA_Q_PRESCALE = MLA_SCALE * math.log2(math.e)
MLA_HP = 8
ROPE_FREQ = MLA_ROPE // 4
ROPE_BASE = 10000.0
BRANCH_W = 1024

_IN_SIZES = (512, 512, 1024, 32, 1024, 3072, 16, 16, 1024, 384, 256, 64, 1024, 3072)
_IN_NAMES = ('gla_q', 'gla_k', 'gla_v', 'gla_gate_lr', 'gla_z', 'gdn_qkv', 'gdn_a', 'gdn_b', 'gdn_z',
             'mla_q_down', 'mla_kv_down', 'mla_k_rope', 'mla_z', 'merge_gate')
_IN_OFF = dict(zip(_IN_NAMES, np.concatenate([[0], np.cumsum(_IN_SIZES)[:-1]]).tolist()))
_IN_LEN = dict(zip(_IN_NAMES, _IN_SIZES))

SM_ROPE, SM_LR, SM_A, SM_B = 0, 64, 96, 112
MLA_GROUP_W = MLA_Q_LORA + MLA_KV_LORA + 128
SMALL_BLOCK = (MLA_Q_LORA + MLA_KV_LORA) // 128

VMEM_LIMIT = 56 * 1024 * 1024


def _params(sem):
    return pltpu.CompilerParams(dimension_semantics=sem, vmem_limit_bytes=VMEM_LIMIT)


def _pick(n, cands):
    for c in cands:
        if n % c == 0:
            return c
    raise ValueError(f"no tile for {n} in {cands}")


def _mm(a, b):
    return jnp.dot(a.astype(BF16), b.astype(BF16), preferred_element_type=F32)


def _mm_nt(a, b):
    return lax.dot_general(a.astype(BF16), b.astype(BF16), (((1,), (1,)), ((), ())),
                           preferred_element_type=F32)


def _mm_tn(a, b):
    return lax.dot_general(a.astype(BF16), b.astype(BF16), (((0,), (0,)), ((), ())),
                           preferred_element_type=F32)


def _mm_hp(a, b):
    return jnp.dot(a, b, precision=lax.Precision.HIGHEST, preferred_element_type=F32)


def _split(a):
    hi = a.astype(BF16)
    lo = (a - hi.astype(F32)).astype(BF16)
    return hi, lo


def _mm3(a, b):
    ah, al = _split(a)
    bh, bl = _split(b)
    d = functools.partial(jnp.dot, preferred_element_type=F32)
    return d(ah, bh) + (d(ah, bl) + d(al, bh))


_MMX = _mm
_TRI_BLOCK = 8


def _cum_rows(cum, x):
    cb = cum.astype(BF16)
    d = functools.partial(jnp.dot, preferred_element_type=F32)
    hi, lo = _split(x)
    return d(cb, hi) + d(cb, lo)


def _cum_cols(x, cum):
    cb = cum.astype(BF16)
    d = functools.partial(jnp.dot, preferred_element_type=F32)
    hi, lo = _split(x)
    return d(hi, cb) + d(lo, cb)


def _softplus(x):
    return jnp.maximum(x, 0.0) + jnp.log(1.0 + jnp.exp(-jnp.abs(x)))


def _log_sigmoid(x):
    return jnp.minimum(x, 0.0) - jnp.log(1.0 + jnp.exp(-jnp.abs(x)))


_sigmoid = jax.nn.sigmoid


def _silu(x):
    return x * _sigmoid(x)


def _tri_masks(n, reverse):
    r = lax.broadcasted_iota(jnp.int32, (n, n), 0)
    c = lax.broadcasted_iota(jnp.int32, (n, n), 1)
    incl = (c >= r) if reverse else (c <= r)
    strict = (c > r) if reverse else (c < r)
    return r, c, incl, strict


def _ada_kernel(c_ref, w_ref, b_ref, o_ref):
    o_ref[0] = _mm_hp(_silu(c_ref[...]), w_ref[0]) + b_ref[0]


def _ada_call(c_all, ada_w, ada_b):
    nl, d, n3 = ada_w.shape
    r = c_all.shape[0]
    tn = 1024
    return pl.pallas_call(
        _ada_kernel,
        grid=(nl, n3 // tn),
        in_specs=[pl.BlockSpec((r, d), lambda l, j: (0, 0)),
                  pl.BlockSpec((1, d, tn), lambda l, j: (l, 0, j)),
                  pl.BlockSpec((1, 1, tn), lambda l, j: (l, 0, j))],
        out_specs=pl.BlockSpec((1, r, tn), lambda l, j: (l, 0, j)),
        out_shape=jax.ShapeDtypeStruct((nl, r, n3), F32),
        compiler_params=_params(("arbitrary", "arbitrary")),
        name="ada_rows",
    )(c_all, ada_w, ada_b.reshape(nl, 1, n3))


def _row_select(t, tm, lc, ctx_row, lat_row):
    rows = t * tm + lax.broadcasted_iota(jnp.int32, (tm, 1), 0)
    return jnp.where(rows < lc, ctx_row, lat_row)


def _prenorm_kernel(h_ref, modx_ref, modc_ref, nw_ref, o_ref, *, lc, tm):
    t = pl.program_id(1)
    x = h_ref[0]
    y = x * lax.rsqrt(jnp.mean(x * x, axis=-1, keepdims=True) + EPS) * nw_ref[...]
    mx = modx_ref[0]
    mc = modc_ref[...]
    d = D_MODEL
    shift = _row_select(t, tm, lc, mc[:, :d], mx[:, :d])
    scale = _row_select(t, tm, lc, mc[:, d:2 * d], mx[:, d:2 * d])
    o_ref[0] = (y * (1.0 + scale) + shift).astype(o_ref.dtype)


def _prenorm_call(h, modx, modc, norm_w, lc):
    b, l, d = h.shape
    tm = _pick(l, (768, 512, 384, 256, 128))
    return pl.pallas_call(
        functools.partial(_prenorm_kernel, lc=lc, tm=tm),
        grid=(b, l // tm),
        in_specs=[pl.BlockSpec((1, tm, d), lambda i, t: (i, t, 0)),
                  pl.BlockSpec((1, 1, 3 * d), lambda i, t: (i, 0, 0)),
                  pl.BlockSpec((1, 3 * d), lambda i, t: (0, 0)),
                  pl.BlockSpec((1, d), lambda i, t: (0, 0))],
        out_specs=pl.BlockSpec((1, tm, d), lambda i, t: (i, t, 0)),
        out_shape=jax.ShapeDtypeStruct((b, l, d), BF16),
        compiler_params=_params(("parallel", "parallel")),
        name="prenorm",
    )(h, modx, modc, norm_w.reshape(1, d))


def _proj_kernel(x_ref, w_ref, o_ref):
    o_ref[...] = jnp.dot(x_ref[...], w_ref[...], preferred_element_type=F32).astype(o_ref.dtype)


def _proj_call(x, w, out_dtype, name):
    m, k = x.shape
    n = w.shape[1]
    tm = _pick(m, (1024, 768, 512, 384, 256, 128))
    tn = _pick(n, (1536, 1024, 768, 512, 256, 128))
    return pl.pallas_call(
        _proj_kernel,
        grid=(m // tm, n // tn),
        in_specs=[pl.BlockSpec((tm, k), lambda i, j: (i, 0)),
                  pl.BlockSpec((k, tn), lambda i, j: (0, j))],
        out_specs=pl.BlockSpec((tm, tn), lambda i, j: (i, j)),
        out_shape=jax.ShapeDtypeStruct((m, n), out_dtype),
        compiler_params=_params(("parallel", "parallel")),
        name=name,
    )(x, w)


def _bwd_chunk(i, ncc, nc):
    return jnp.where(i < ncc, ncc - 1 - i, ncc + nc - 1 - i)


def _chunk_rows(c):
    return pl.ds(pl.multiple_of(c * CHUNK, CHUNK), CHUNK)


def _gla_kernel(q_ref, k_ref, v_ref, sm_ref, w2_ref, gb_ref, o_ref, la_ref, st_ref, *, ncc, nc):
    sm = sm_ref[0]
    for d in range(2):
        logit = _mm3(sm, w2_ref[d]) + gb_ref[d]
        la_ref[d] = _log_sigmoid(logit) * (1.0 / GLA_NORMALIZER)
    st_ref[...] = jnp.zeros_like(st_ref)
    o_ref[...] = jnp.zeros_like(o_ref)
    masks = [_tri_masks(CHUNK, rev) for rev in (False, True)]
    cum_mats = [m[2].astype(F32) for m in masks]
    probs = [(d, hh) for d in range(2) for hh in range(GLA_HB)]

    def body(i, carry):
        rows = [_chunk_rows(i), _chunk_rows(_bwd_chunk(i, ncc, nc))]
        bcum_d = [_cum_rows(cum_mats[d], la_ref[d, rows[d], :]) for d in range(2)]
        kl = [slice(hh * GLA_DK, (hh + 1) * GLA_DK) for _, hh in probs]
        vl = [slice(hh * GLA_DV, (hh + 1) * GLA_DV) for _, hh in probs]
        bcum = [bcum_d[d][:, ln] for (d, _), ln in zip(probs, kl)]
        b_last = [bc[CHUNK - 1:CHUNK] if d == 0 else bc[0:1] for (d, _), bc in zip(probs, bcum)]
        q = [q_ref[0, rows[d], ln].astype(F32) * GLA_DK ** -0.5 for (d, _), ln in zip(probs, kl)]
        k = [k_ref[0, rows[d], ln].astype(F32) for (d, _), ln in zip(probs, kl)]
        v = [v_ref[0, rows[d], ln].astype(F32) for (d, _), ln in zip(probs, vl)]
        q_dec = [a * jnp.exp(bc) for a, bc in zip(q, bcum)]
        k_inv = [a * jnp.exp(-bc) for a, bc in zip(k, bcum)]
        k_end = [a * jnp.exp(bl - bc) for a, bl, bc in zip(k, b_last, bcum)]
        st = [st_ref[j] for j in range(len(probs))]
        qk = _each(_mm_nt, q_dec, k_inv)
        o_inter = _each(_mm_nt, q_dec, st)
        upd = _each(_mm_tn, v, k_end)
        att = [jnp.where(masks[d][2], a, 0.0) for (d, _), a in zip(probs, qk)]
        o_intra = _each(_mm, att, v)
        for j, (sx, bl, up) in enumerate(zip(st, b_last, upd)):
            st_ref[j] = sx * jnp.exp(bl) + up
        for (d, _), ln, a, b in zip(probs, vl, o_intra, o_inter):
            o_ref[0, rows[d], ln] += a + b
        return carry

    lax.fori_loop(0, nc, body, 0)


def _gla_call(p_gla, p_mla, w2e, gate_b, lc):
    b, l, _ = p_gla.shape
    nc, ncc = l // CHUNK, lc // CHUNK
    nhb = GLA_HEADS // GLA_HB
    wk, wv = GLA_HB * GLA_DK, GLA_HB * GLA_DV
    return pl.pallas_call(
        functools.partial(_gla_kernel, ncc=ncc, nc=nc),
        grid=(b, nhb),
        in_specs=[pl.BlockSpec((1, l, wk), lambda i, h: (i, 0, h)),
                  pl.BlockSpec((1, l, wk), lambda i, h: (i, 0, nhb + h)),
                  pl.BlockSpec((1, l, wv), lambda i, h: (i, 0, nhb + h)),
                  pl.BlockSpec((1, l, 128), lambda i, h: (i, 0, SMALL_BLOCK)),
                  pl.BlockSpec((2, 128, wk), lambda i, h: (0, 0, h)),
                  pl.BlockSpec((2, 1, wk), lambda i, h: (0, 0, h))],
        out_specs=pl.BlockSpec((1, l, wv), lambda i, h: (i, 0, h)),
        out_shape=jax.ShapeDtypeStruct((b, l, GLA_HEADS * GLA_DV), F32),
        scratch_shapes=[pltpu.VMEM((2, l, wk), F32), pltpu.VMEM((2 * GLA_HB, GLA_DV, GLA_DK), F32)],
        compiler_params=_params(("parallel", "parallel")),
        name="gla_scan",
    )(p_gla, p_gla, p_gla, p_mla, w2e, gate_b)


def _gdn_prep_kernel(x_ref, w_ref, o_ref, *, lc, l):
    j = pl.program_id(1)
    w = w_ref[...]
    half = GDN_CONV // 2
    nblk_head = GDN_HEADS * GDN_DK // 128

    def finish(acc):
        y = _silu(acc)
        inv = lax.rsqrt(jnp.sum(y * y, axis=-1, keepdims=True) + EPS)
        fac = jnp.where(j < nblk_head, inv * GDN_DK ** -0.5, jnp.where(j < 2 * nblk_head, inv, 1.0))
        return (y * fac).astype(o_ref.dtype)

    grp, halo = 16, 8
    acc = None
    for s in range(-half, half + 1):
        term = x_ref[0, grp + s:l - grp + s, :].astype(F32) * w[s + half:s + half + 1]
        acc = term if acc is None else acc + term
    o_ref[0, grp:l - grp, :] = finish(acc)

    zeros = jnp.zeros((halo, 128), F32)
    for r0 in sorted({0, lc - grp, lc, l - grp}):
        lo, hi = max(r0 - halo, 0), min(r0 + grp + halo, l)
        slab = x_ref[0, lo:hi, :].astype(F32)
        if r0 - halo < 0:
            slab = jnp.concatenate([zeros, slab], axis=0)
        if r0 + grp + halo > l:
            slab = jnp.concatenate([slab, zeros], axis=0)
        n = grp + 2 * halo
        t = r0 + lax.broadcasted_iota(jnp.int32, (grp, 128), 0)
        start = jnp.where(t < lc, 0, lc)
        end = jnp.where(t < lc, lc, l)
        acc = slab[halo:halo + grp] * w[half:half + 1]
        for s in range(-half, half + 1):
            if s != 0:
                xs = pltpu.roll(slab, (-s) % n, axis=0)[halo:halo + grp]
                ok = (t + s >= start) if s < 0 else (t + s < end)
                acc = acc + jnp.where(ok, xs, 0.0) * w[s + half:s + half + 1]
        o_ref[0, r0:r0 + grp, :] = finish(acc)


def _gdn_prep_call(p_gdn, conv_w, lc):
    b, l, n = p_gdn.shape
    return pl.pallas_call(
        functools.partial(_gdn_prep_kernel, lc=lc, l=l),
        grid=(b, n // 128),
        in_specs=[pl.BlockSpec((1, l, 128), lambda i, j: (i, 0, j)),
                  pl.BlockSpec((GDN_CONV, 128), lambda i, j: (0, j))],
        out_specs=pl.BlockSpec((1, l, 128), lambda i, j: (i, 0, j)),
        out_shape=jax.ShapeDtypeStruct((b, l, n), BF16),
        compiler_params=_params(("parallel", "parallel")),
        name="gdn_prep",
    )(p_gdn, conv_w)


def _each(fn, *lists):
    return [fn(*args) for args in zip(*lists)]


class _TriConsts:
    def __init__(self):
        r = lax.broadcasted_iota(jnp.int32, (CHUNK, 2 * CHUNK), 0)
        lane = lax.broadcasted_iota(jnp.int32, (CHUNK, 2 * CHUNK), 1)
        c = lane % CHUNK
        same = lambda s: (r // s) == (c // s)
        nb = _TRI_BLOCK
        self.left = lane < CHUNK
        self.diag_blocks = same(nb)[:, :CHUNK]
        self.eye = (r == c).astype(F32)[:, :CHUNK]
        right = jnp.logical_not(self.left)
        self.first_is_eye = self.left & (r == c)
        self.first_is_lm = right & jnp.logical_not(same(nb))
        sizes = [nb * 2 ** j for j in range(int(math.log2(CHUNK // nb)))]
        self.level = [right & same(2 * s) & jnp.logical_not(same(s)) for s in sizes]


def _unit_tri_solves(lms, rhs, tc):
    nb = _TRI_BLOCK
    y = [-jnp.where(tc.diag_blocks, lm[:, :CHUNK], 0.0) for lm in lms]
    t = [tc.eye + a for a in y]
    p = y
    for _ in range(int(math.log2(nb)) - 1):
        p = _each(_MMX, p, p)
        t = _each(lambda a, b: a + _MMX(a, b), t, p)
    first = [jnp.where(tc.first_is_eye, 1.0, jnp.where(tc.first_is_lm, lm, 0.0)) for lm in lms]
    x = _each(_mm, t, first)
    zero = jnp.zeros((CHUNK, 2 * CHUNK), F32)
    for mask in tc.level:
        m = [jnp.where(mask, a, 0.0) for a in x]
        x = _each(lambda a, mj: a - _mm(mj, jnp.concatenate([zero, a], axis=0)), x, m)
    t_full = [jnp.where(tc.left, a, 0.0) for a in x]
    zero_rhs = jnp.zeros_like(rhs[0])
    return _each(lambda a, b: _mm(a, jnp.concatenate([b, zero_rhs], axis=0)), t_full, rhs)


def _gdn_kernel(rate_ref, ratel_ref, q_ref, k_ref, v_ref, abc_ref, abr_ref, o_ref, s_ref, *, ncc, nc):
    na = 2 * GDN_HB
    s_ref[...] = jnp.zeros_like(s_ref)
    o_ref[...] = jnp.zeros_like(o_ref)
    masks = [_tri_masks(CHUNK, rev) for rev in (False, True)]
    cum_mats = [m[2].astype(F32) for m in masks]
    r2 = lax.broadcasted_iota(jnp.int32, (CHUNK, 2 * CHUNK), 0)
    c2 = lax.broadcasted_iota(jnp.int32, (CHUNK, 2 * CHUNK), 1) % CHUNK
    incl2 = [c2 <= r2, c2 >= r2]
    strict2 = [c2 < r2, c2 > r2]
    cum_twice = [m.astype(F32) for m in incl2]
    tri = _TriConsts()
    probs = [(d, hh) for d in range(2) for hh in range(GDN_HB)]
    neg_rate_rows = -jnp.exp(rate_ref[0, 0])[:, :CHUNK]
    dtb_rows = rate_ref[0, 1][:, :CHUNK]
    neg_rate_lanes = -jnp.exp(ratel_ref[0, 0, 0:1, :])
    dtb_lanes = ratel_ref[0, 1, 0:1, :]

    def body(i, carry):
        dirs = (0, 1)
        chunk = [i, _bwd_chunk(i, ncc, nc)]
        rows = [_chunk_rows(c) for c in chunk]
        abc = [abc_ref[0, 0, rw, :] for rw in rows]
        abr = [abr_ref[0, 0, c] for c in chunk]
        incl = [incl2[d] for d, _ in probs]
        strict = [strict2[d] for d, _ in probs]
        g_cols = [neg_rate_lanes * _softplus(abc[d] + dtb_lanes) for d in dirs]
        g_rows = [neg_rate_rows * _softplus(abr[d] + dtb_rows) for d in dirs]
        gc_cols = [_cum_rows(cum_mats[d], g_cols[d]) for d in dirs]
        gc_rows = [_cum_cols(g_rows[d], cum_twice[1 - d]) for d in dirs]
        beta_cols = [jax.nn.sigmoid(abc[d]) for d in dirs]
        col = [d * GDN_HB + hh for d, hh in probs]
        gc_col = [jnp.broadcast_to(gc_cols[d][:, j:j + 1], (CHUNK, 128)) for (d, _), j in zip(probs, col)]
        gc_row = [gc_rows[d][j:j + 1, :] for (d, _), j in zip(probs, col)]
        beta_col = [beta_cols[d][:, na + j:na + j + 1] for (d, _), j in zip(probs, col)]
        g_last =[gc[CHUNK - 1:CHUNK] if d == 0 else gc[0:1] for (d, _), gc in zip(probs, gc_col)]
        decay = [jnp.where(m, jnp.exp(jnp.where(m, a - b, 0.0)), 0.0)
                 for m, a, b in zip(incl, gc_col, gc_row)]

        lanes = [slice(hh * GDN_DK, (hh + 1) * GDN_DK) for _, hh in probs]
        q = [q_ref[0, rows[d], ln].astype(F32) for (d, _), ln in zip(probs, lanes)]
        k = [k_ref[0, rows[d], ln].astype(F32) for (d, _), ln in zip(probs, lanes)]
        v = [v_ref[0, rows[d], ln].astype(F32) for (d, _), ln in zip(probs, lanes)]
        k_beta = _each(jnp.multiply, k, beta_col)
        v_beta = _each(jnp.multiply, v, beta_col)
        e_col = _each(jnp.exp, gc_col)
        kq = _each(_mm_nt, [jnp.concatenate([a, b], axis=0) for a, b in zip(k_beta, q)],
                   [jnp.concatenate([a, a], axis=0) for a in k])
        kk = [x[:CHUNK] for x in kq]
        qk = [x[CHUNK:] for x in kq]
        lm = [jnp.where(m, a * dc, 0.0) for m, a, dc in zip(strict, kk, decay)]
        rhs = [jnp.concatenate([vb, kb * e], axis=1) for vb, kb, e in zip(v_beta, k_beta, e_col)]
        uw = _unit_tri_solves(lm, rhs, tri)
        u = [x[:, :GDN_DV] for x in uw]
        w = [x[:, GDN_DV:] for x in uw]
        att = [jnp.where(m, a * dc, 0.0)[:, :CHUNK] for m, a, dc in zip(incl, qk, decay)]
        q_dec = _each(jnp.multiply, q, e_col)
        k_end = [kx * jnp.exp(gl - gc) for kx, gl, gc in zip(k, g_last, gc_col)]
        s = [s_ref[d * GDN_HB + hh] for d, hh in probs]
        wq = _each(_mm, [jnp.concatenate([a, b], axis=0) for a, b in zip(w, q_dec)], s)
        ws = [x[:CHUNK] for x in wq]
        o_inter = [x[CHUNK:] for x in wq]
        v_new =_each(jnp.subtract, u, ws)
        upd = _each(_mm_tn, k_end, v_new)
        o_intra = _each(_mm, att, v_new)
        for (d, hh), sx, gl, up in zip(probs, s, g_last, upd):
            s_ref[d * GDN_HB + hh] = sx * jnp.exp(gl) + up
        for (d, _), ln, a, b in zip(probs, lanes, o_intra, o_inter):
            o_ref[0, rows[d], ln] += a + b
        return carry

    lax.fori_loop(0, nc, body, 0)


def _gdn_call(qkvn, abc, abr, a_log, dt_bias, lc):
    b, l, _ = qkvn.shape
    nc, ncc = l // CHUNK, lc // CHUNK
    nhb = GDN_HEADS // GDN_HB
    wb = GDN_HB * GDN_DK
    na = 2 * GDN_HB
    rate = jnp.stack([a_log, dt_bias]).reshape(2, 2, nhb, GDN_HB).transpose(2, 0, 1, 3).reshape(nhb, 2, na)
    rate = jnp.pad(rate, ((0, 0), (0, 0), (0, 128 - na)))
    nr = _GDN_GATE_ROWS
    rate_rows = jnp.broadcast_to(rate[:, :, :nr, None], (nhb, 2, nr, 128))
    rate_lanes = jnp.broadcast_to(rate[:, :, None, :], (nhb, 2, 8, 128))
    return pl.pallas_call(
        functools.partial(_gdn_kernel, ncc=ncc, nc=nc),
        grid=(b, nhb),
        in_specs=[pl.BlockSpec((1, 2, nr, 128), lambda i, h: (h, 0, 0, 0)),
                  pl.BlockSpec((1, 2, 8, 128), lambda i, h: (h, 0, 0, 0)),
                  pl.BlockSpec((1, l, wb), lambda i, h: (i, 0, h)),
                  pl.BlockSpec((1, l, wb), lambda i, h: (i, 0, nhb + h)),
                  pl.BlockSpec((1, l, wb), lambda i, h: (i, 0, 2 * nhb + h)),
                  pl.BlockSpec((1, 1, l, 128), lambda i, h: (i, h, 0, 0)),
                  pl.BlockSpec((1, 1, nc, nr, CHUNK), lambda i, h: (i, h, 0, 0, 0))],
        out_specs=pl.BlockSpec((1, l, wb), lambda i, h: (i, 0, h)),
        out_shape=jax.ShapeDtypeStruct((b, l, GDN_HEADS * GDN_DV), F32),
        scratch_shapes=[pltpu.VMEM((2 * GDN_HB, GDN_DK, GDN_DV), F32)],
        compiler_params=_params(("parallel", "parallel")),
        name="gdn_scan",
    )(rate_rows, rate_lanes, qkvn, qkvn, qkvn, abc, abr)


def _gdn_gate_layouts(p_mla, lc):
    b, l, _ = p_mla.shape
    nhb = GDN_HEADS // GDN_HB
    na = 2 * GDN_HB
    o = MLA_Q_LORA + MLA_KV_LORA
    ab = p_mla[:, :, o + SM_A:o + SM_A + 32].reshape(b, l, 2, 2, nhb, GDN_HB)
    ab = ab.transpose(0, 4, 1, 2, 3, 5).reshape(b, nhb, l, 2 * na)
    abr = ab[..., :na].reshape(b, nhb, l // CHUNK, CHUNK, na).transpose(0, 1, 2, 4, 3)
    abr = jnp.pad(abr, ((0, 0), (0, 0), (0, 0), (0, _GDN_GATE_ROWS - na), (0, 0)))
    abc = jnp.pad(ab, ((0, 0), (0, 0), (0, 0), (0, 128 - 2 * na)))
    return abc, abr


def _rope_swap(x):
    lane = lax.broadcasted_iota(jnp.int32, x.shape, 1)
    n = x.shape[1]
    return jnp.where(lane % 32 < 16, pltpu.roll(x, n - 16, axis=1), pltpu.roll(x, 16, axis=1))


def _mla_prep_kernel(p_ref, cos_ref, sin_ref, qnw_ref, qup_ref, kvnw_ref, kvup_ref,
                     wqn_ref, wqr_ref, wkn_ref, wkr_ref, q_ref, k_ref, v_ref):
    p = p_ref[0]
    cos_t = cos_ref[...]
    sin_t = sin_ref[...]

    def rms(x, n):
        return x * lax.rsqrt(jnp.sum(x * x, axis=-1, keepdims=True) * (1.0 / n) + EPS)

    def rope(x):
        return x * cos_t + _rope_swap(x) * sin_t

    qd = p[:, :MLA_Q_LORA]
    kvd = p[:, MLA_Q_LORA:MLA_Q_LORA + MLA_KV_LORA]
    sm = p[:, MLA_Q_LORA + MLA_KV_LORA:]
    q = _mm(rms(qd, MLA_Q_LORA) * qnw_ref[...], qup_ref[...])
    kv = _mm(rms(kvd, MLA_KV_LORA) * kvnw_ref[...], kvup_ref[...])
    lane = lax.broadcasted_iota(jnp.int32, sm.shape, 1)
    kr = jnp.where(lane < MLA_ROPE, sm, 0.0)
    kr = rope(rms(kr, MLA_ROPE) * wkr_ref[...])
    for h in range(MLA_HEADS):
        base = h * 256
        qn = rms(q[:, base:base + 128], MLA_NOPE) * wqn_ref[...]
        q_ref[0, :, base:base + 128] = (qn * MLA_Q_PRESCALE).astype(q_ref.dtype)
        qr = rms(q[:, base + 128:base + 256], MLA_ROPE) * wqr_ref[...]
        q_ref[0, :, base + 128:base + 256] = (rope(qr) * MLA_Q_PRESCALE).astype(q_ref.dtype)
        kn = kv[:, h * 128:(h + 1) * 128]
        k_ref[0, :, base:base + 128] = (rms(kn, MLA_NOPE) * wkn_ref[...]).astype(k_ref.dtype)
        k_ref[0, :, base + 128:base + 256] = kr.astype(k_ref.dtype)
    v_ref[0] = kv[:, MLA_HEADS * MLA_NOPE:].T.astype(v_ref.dtype)


def _mla_prep_call(p_mla, cos_t, sin_t, qnw, qup, kvnw, kvup, wqn, wqr, wkn, wkr):
    b, l, gw = p_mla.shape
    tm = _pick(l, (256, 128))
    hq = MLA_HEADS * 256
    full = lambda shape: pl.BlockSpec(shape, lambda i, t: tuple(0 for _ in shape))
    return pl.pallas_call(
        _mla_prep_kernel,
        grid=(b, l // tm),
        in_specs=[pl.BlockSpec((1, tm, gw), lambda i, t: (i, t, 0)),
                  pl.BlockSpec((tm, 128), lambda i, t: (t, 0)),
                  pl.BlockSpec((tm, 128), lambda i, t: (t, 0)),
                  full((1, MLA_Q_LORA)), full((MLA_Q_LORA, hq)),
                  full((1, MLA_KV_LORA)), full((MLA_KV_LORA, hq)),
                  full((1, 128)), full((1, 128)), full((1, 128)), full((1, 128))],
        out_specs=[pl.BlockSpec((1, tm, hq), lambda i, t: (i, t, 0)),
                   pl.BlockSpec((1, tm, hq), lambda i, t: (i, t, 0)),
                   pl.BlockSpec((1, MLA_HEADS * MLA_DV, tm), lambda i, t: (i, 0, t))],
        out_shape=[jax.ShapeDtypeStruct((b, l, hq), BF16),
                   jax.ShapeDtypeStruct((b, l, hq), BF16),
                   jax.ShapeDtypeStruct((b, MLA_HEADS * MLA_DV, l), BF16)],
        compiler_params=_params(("parallel", "parallel")),
        name="mla_prep",
    )(p_mla, cos_t, sin_t, qnw, qup, kvnw, kvup, wqn, wqr, wkn, wkr)


def _mla_attn_kernel(q_ref, k_ref, vt_ref, o_ref, *, lc, tq, q_off):
    qi = pl.program_id(2) + q_off
    heads = range(MLA_HP)

    def attend(nk):
        st = [lax.dot_general(k_ref[0, :nk, h * 256:(h + 1) * 256], q_ref[0, :, h * 256:(h + 1) * 256],
                              (((1,), (1,)), ((), ())), preferred_element_type=F32) for h in heads]
        e = [jnp.exp2(x - jnp.max(x, axis=0, keepdims=True)) for x in st]
        den = [jnp.sum(x, axis=0, keepdims=True) for x in e]
        ot = [jnp.dot(vt_ref[0, h * MLA_DV:(h + 1) * MLA_DV, :nk], e[h].astype(BF16),
                      preferred_element_type=F32) for h in heads]
        for h in heads:
            o_ref[0, :, h * MLA_DV:(h + 1) * MLA_DV] = (ot[h] / den[h]).T.astype(o_ref.dtype)

    if q_off == 0:
        @pl.when(qi * tq < lc)
        def _():
            attend(lc)

        @pl.when(qi * tq >= lc)
        def _():
            attend(k_ref.shape[1])
    else:
        attend(k_ref.shape[1])


def _mla_attn_call(q, k, vt, lc, with_ctx):
    b, l, _ = q.shape
    tq = _pick(lc, (256, 128))
    q_off = 0 if with_ctx else lc // tq
    nq = l // tq - q_off
    return pl.pallas_call(
        functools.partial(_mla_attn_kernel, lc=lc, tq=tq, q_off=q_off),
        grid=(b, MLA_HEADS // MLA_HP, nq),
        in_specs=[pl.BlockSpec((1, tq, MLA_HP * 256), lambda i, h, t: (i, t + q_off, h)),
                  pl.BlockSpec((1, l, MLA_HP * 256), lambda i, h, t: (i, 0, h)),
                  pl.BlockSpec((1, MLA_HP * MLA_DV, l), lambda i, h, t: (i, h, 0))],
        out_specs=pl.BlockSpec((1, tq, MLA_HP * MLA_DV), lambda i, h, t: (i, t, h)),
        out_shape=jax.ShapeDtypeStruct((b, nq * tq, MLA_HEADS * MLA_DV), F32),
        compiler_params=_params(("parallel", "parallel", "arbitrary")),
        name="mla_attn",
    )(q, k, vt)


def _merge_kernel(h_ref, oa_ref, ob_ref, oc_ref, za_ref, zb_ref, zc_ref, g0_ref, g1_ref, g2_ref,
                  modx_ref, modc_ref, wa_ref, wb_ref, wbr_ref, wout_ref, o_ref, *, lc, tm, t_off):
    t = pl.program_id(1) + t_off

    def head_norm(o, width):
        parts = []
        for h in range(BRANCH_W // width):
            x = o[:, h * width:(h + 1) * width]
            parts.append(x * lax.rsqrt(jnp.mean(x * x, axis=-1, keepdims=True) + EPS))
        return jnp.concatenate(parts, axis=-1)

    ya = head_norm(oa_ref[0], GLA_DV) * wa_ref[...] * _silu(za_ref[0].astype(F32))
    yb = head_norm(ob_ref[0], GDN_DV) * wb_ref[...] * _silu(zb_ref[0].astype(F32))
    yc = oc_ref[0] * _silu(zc_ref[0].astype(F32))
    acc = _sigmoid(g0_ref[0].astype(F32)) * _mm(ya, wbr_ref[0])
    acc = acc + _sigmoid(g1_ref[0].astype(F32)) * _mm(yb, wbr_ref[1])
    acc = acc + _sigmoid(g2_ref[0].astype(F32)) * _mm(yc, wbr_ref[2])
    d = D_MODEL
    gate = _row_select(t, tm, lc, modc_ref[:, 2 * d:], modx_ref[0][:, 2 * d:])
    o_ref[0] = h_ref[0] + gate * _mm(acc, wout_ref[...])


def _merge_call(h, oa, ob, oc, p_zg, modx, modc, wa, wb, w_branch, w_out, lc, with_ctx):
    b, l, d = h.shape
    tm = _pick(lc, (256, 128))
    t_off = 0 if with_ctx else lc // tm
    nt = l // tm - t_off
    oc_off = 0 if oc.shape[1] == l else lc // tm
    row = lambda i, t: (i, t + t_off, 0)
    zg = lambda n: pl.BlockSpec((1, tm, d), lambda i, t: (i, t + t_off, n))
    return pl.pallas_call(
        functools.partial(_merge_kernel, lc=lc, tm=tm, t_off=t_off),
        grid=(b, nt),
        in_specs=[pl.BlockSpec((1, tm, d), row), pl.BlockSpec((1, tm, d), row), pl.BlockSpec((1, tm, d), row),
                  pl.BlockSpec((1, tm, d), lambda i, t: (i, t + t_off - oc_off, 0)),
                  zg(0), zg(1), zg(2), zg(3), zg(4), zg(5),
                  pl.BlockSpec((1, 1, 3 * d), lambda i, t: (i, 0, 0)),
                  pl.BlockSpec((1, 3 * d), lambda i, t: (0, 0)),
                  pl.BlockSpec((1, d), lambda i, t: (0, 0)),
                  pl.BlockSpec((1, d), lambda i, t: (0, 0)),
                  pl.BlockSpec((3, d, d), lambda i, t: (0, 0, 0)),
                  pl.BlockSpec((d, d), lambda i, t: (0, 0))],
        out_specs=pl.BlockSpec((1, tm, d), lambda i, t: (i, t, 0)),
        out_shape=jax.ShapeDtypeStruct((b, nt * tm, d), F32),
        compiler_params=_params(("parallel", "parallel")),
        name="merge",
    )(h, oa, ob, oc, p_zg, p_zg, p_zg, p_zg, p_zg, p_zg, modx, modc, wa, wb, w_branch, w_out)


def _cols(w, *names):
    return jnp.concatenate([w[:, _IN_OFF[n]:_IN_OFF[n] + _IN_LEN[n]] for n in names], axis=1)


def _rope_tables(t_lat, lc):
    rows = t_lat // GRID_W
    row = jnp.repeat(jnp.arange(rows, dtype=F32), GRID_W)
    col = jnp.tile(jnp.arange(GRID_W, dtype=F32), rows)
    inv_freq = jnp.power(ROPE_BASE, -jnp.arange(ROPE_FREQ, dtype=F32) / ROPE_FREQ)
    ar = row[:, None] * inv_freq
    ac = col[:, None] * inv_freq
    zeros = jnp.zeros((t_lat, 128 - MLA_ROPE), F32)
    cos_t = jnp.concatenate([jnp.cos(ar), jnp.cos(ar), jnp.cos(ac), jnp.cos(ac), zeros], axis=1)
    sin_t = jnp.concatenate([-jnp.sin(ar), jnp.sin(ar), -jnp.sin(ac), jnp.sin(ac), zeros], axis=1)
    cos_c = jnp.concatenate([jnp.ones((lc, MLA_ROPE), F32), jnp.zeros((lc, 128 - MLA_ROPE), F32)], axis=1)
    return (jnp.concatenate([cos_c, cos_t], axis=0),
            jnp.concatenate([jnp.zeros((lc, 128), F32), sin_t], axis=0))


def _pad_lanes(w, n=128):
    return jnp.pad(w, (0, n - w.shape[0])).reshape(1, n)


def kernel(x, c, ctx, c_ctx, norm_w, ada_w, ada_b, w_in, gla_gate_w2, gla_gate_b, gla_norm_w, gdn_conv_w,
           gdn_a_log, gdn_dt_bias, gdn_norm_w, mla_q_norm_w, mla_q_up, mla_kv_norm_w, mla_kv_up,
           mla_qn_nope, mla_qn_rope, mla_kn_nope, mla_kn_rope, w_branch, w_out):
    b, t_lat, d = x.shape
    lc = ctx.shape[1]
    l = lc + t_lat
    depth = w_in.shape[0]
    assert d == D_MODEL and lc % 128 == 0 and t_lat % 128 == 0

    h = jnp.concatenate([ctx, x], axis=1)
    rows = ((b + 1 + 7) // 8) * 8
    c_all = jnp.concatenate([c, c_ctx[None, :], jnp.zeros((rows - b - 1, d), F32)], axis=0)
    mod = _ada_call(c_all, ada_w, ada_b)
    cos_t, sin_t = _rope_tables(t_lat, lc)

    for li in range(depth):
        last = li == depth - 1
        modx = mod[li, :b].reshape(b, 1, 3 * d)
        modc = mod[li, b:b + 1]
        w = w_in[li]
        w_mla = _cols(w, 'mla_q_down', 'mla_kv_down', 'mla_k_rope', 'gla_gate_lr', 'gdn_a', 'gdn_b').astype(BF16)
        w_gla = _cols(w, 'gla_q', 'gla_k', 'gla_v').astype(BF16)
        w_gdn = _cols(w, 'gdn_qkv').astype(BF16)
        w_zg = _cols(w, 'gla_z', 'gdn_z', 'mla_z', 'merge_gate').astype(BF16)

        hn = _prenorm_call(h, modx, modc, norm_w[li], lc).reshape(b * l, d)
        p_mla = _proj_call(hn, w_mla, F32, "proj_mla").reshape(b, l, -1)
        p_gla = _proj_call(hn, w_gla, BF16, "proj_gla").reshape(b, l, -1)
        p_gdn = _proj_call(hn, w_gdn, F32, "proj_gdn").reshape(b, l, -1)
        p_zg = _proj_call(hn, w_zg, BF16, "proj_zg").reshape(b, l, -1)

        w2e = jnp.stack([jnp.pad(gla_gate_w2[li, dd], ((SM_LR + dd * GLA_RANK, 128 - SM_LR - (dd + 1) * GLA_RANK),
                                                      (0, 0))) for dd in range(2)])
        oa = _gla_call(p_gla, p_mla, w2e, gla_gate_b[li].reshape(2, 1, -1), lc)

        qkvn = _gdn_prep_call(p_gdn, gdn_conv_w[li], lc)
        abc, abr = _gdn_gate_layouts(p_mla, lc)
        ob = _gdn_call(qkvn, abc, abr, gdn_a_log[li], gdn_dt_bias[li], lc)

        qup = mla_q_up[li].reshape(MLA_Q_LORA, MLA_HEADS, MLA_NOPE + MLA_ROPE)
        qup = jnp.pad(qup, ((0, 0), (0, 0), (0, 256 - MLA_NOPE - MLA_ROPE))).reshape(MLA_Q_LORA, -1).astype(BF16)
        kvup = mla_kv_up[li].reshape(MLA_KV_LORA, MLA_HEADS, MLA_NOPE + MLA_DV)
        kvup = jnp.concatenate([kvup[:, :, :MLA_NOPE].reshape(MLA_KV_LORA, -1),
                                kvup[:, :, MLA_NOPE:].reshape(MLA_KV_LORA, -1)], axis=1).astype(BF16)
        qm, km, vm = _mla_prep_call(
            p_mla, cos_t, sin_t, mla_q_norm_w[li].reshape(1, -1), qup, mla_kv_norm_w[li].reshape(1, -1), kvup,
            mla_qn_nope[li].reshape(1, -1), _pad_lanes(mla_qn_rope[li]),
            mla_kn_nope[li].reshape(1, -1), _pad_lanes(mla_kn_rope[li]))
        oc = _mla_attn_call(qm, km, vm, lc, with_ctx=not last)

        h = _merge_call(h, oa, ob, oc, p_zg, modx, modc,
                        jnp.tile(gla_norm_w[li], GLA_HEADS).reshape(1, -1),
                        jnp.tile(gdn_norm_w[li], GDN_HEADS).reshape(1, -1),
                        w_branch[li].astype(BF16), w_out[li].astype(BF16), lc, with_ctx=not last)
    return h
```

```python
import functools
import math

import numpy as np
import jax
import jax.numpy as jnp
from jax import lax
from jax.experimental import pallas as pl
from jax.experimental.pallas import tpu as pltpu

F32 = jnp.float32
BF16 = jnp.bfloat16

D_MODEL = 1024
EPS = 1e-6
GRID_W = 64
CHUNK = 64

GLA_HEADS, GLA_DK, GLA_DV, GLA_RANK, GLA_NORMALIZER = 4, 128, 256, 16, 16.0
GDN_HEADS, GDN_DK, GDN_DV, GDN_CONV = 8, 128, 128, 5
GLA_HB = 4
GDN_HB = 8
_GDN_GATE_ROWS = max(8, 2 * GDN_HB)
GDN_PREP_W = 128
GDN_UNROLL = 2
MLA_HEADS, MLA_Q_LORA, MLA_KV_LORA, MLA_NOPE, MLA_ROPE, MLA_DV = 8, 384, 256, 128, 64, 128
MLA_SCALE = (MLA_NOPE + MLA_ROPE) ** -0.5
MLA_Q_PRESCALE = MLA_SCALE * math.log2(math.e)
MLA_HP = 8
ROPE_FREQ = MLA_ROPE // 4
ROPE_BASE = 10000.0
BRANCH_W = 1024

_IN_SIZES = (512, 512, 1024, 32, 1024, 3072, 16, 16, 1024, 384, 256, 64, 1024, 3072)
_IN_NAMES = ('gla_q', 'gla_k', 'gla_v', 'gla_gate_lr', 'gla_z', 'gdn_qkv', 'gdn_a', 'gdn_b', 'gdn_z',
             'mla_q_down', 'mla_kv_down', 'mla_k_rope', 'mla_z', 'merge_gate')
_IN_OFF = dict(zip(_IN_NAMES, np.concatenate([[0], np.cumsum(_IN_SIZES)[:-1]]).tolist()))
_IN_LEN = dict(zip(_IN_NAMES, _IN_SIZES))

SM_ROPE, SM_LR, SM_A, SM_B = 0, 64, 96, 112
MLA_GROUP_W = MLA_Q_LORA + MLA_KV_LORA + 128
SMALL_BLOCK = (MLA_Q_LORA + MLA_KV_LORA) // 128

VMEM_LIMIT = 60 * 1024 * 1024


def _params(sem):
    return pltpu.CompilerParams(dimension_semantics=sem, vmem_limit_bytes=VMEM_LIMIT)


def _pick(n, cands):
    for c in cands:
        if n % c == 0:
            return c
    raise ValueError(f"no tile for {n} in {cands}")


def _mm(a, b):
    return jnp.dot(a.astype(BF16), b.astype(BF16), preferred_element_type=F32)


def _mm_nt(a, b):
    return lax.dot_general(a.astype(BF16), b.astype(BF16), (((1,), (1,)), ((), ())),
                           preferred_element_type=F32)


def _mm_tn(a, b):
    return lax.dot_general(a.astype(BF16), b.astype(BF16), (((0,), (0,)), ((), ())),
                           preferred_element_type=F32)


def _mm_hp(a, b):
    return jnp.dot(a, b, precision=lax.Precision.HIGHEST, preferred_element_type=F32)


def _split(a):
    hi = a.astype(BF16)
    lo = (a - hi.astype(F32)).astype(BF16)
    return hi, lo


def _mm3(a, b):
    ah, al = _split(a)
    bh, bl = _split(b)
    d = functools.partial(jnp.dot, preferred_element_type=F32)
    return d(ah, bh) + (d(ah, bl) + d(al, bh))


_MMX = _mm
_TRI_BLOCK = 8


def _cum_rows(cum, x):
    cb = cum.astype(BF16)
    d = functools.partial(jnp.dot, preferred_element_type=F32)
    hi, lo = _split(x)
    return d(cb, hi) + d(cb, lo)


def _cum_cols(x, cum):
    cb = cum.astype(BF16)
    d = functools.partial(jnp.dot, preferred_element_type=F32)
    hi, lo = _split(x)
    return d(hi, cb) + d(lo, cb)


def _softplus(x):
    return jnp.maximum(x, 0.0) + jnp.log(1.0 + jnp.exp(-jnp.abs(x)))


def _log_sigmoid(x):
    return jnp.minimum(x, 0.0) - jnp.log(1.0 + jnp.exp(-jnp.abs(x)))


_sigmoid = jax.nn.sigmoid


def _silu(x):
    return x * _sigmoid(x)


def _tri_masks(n, reverse):
    r = lax.broadcasted_iota(jnp.int32, (n, n), 0)
    c = lax.broadcasted_iota(jnp.int32, (n, n), 1)
    incl = (c >= r) if reverse else (c <= r)
    strict = (c > r) if reverse else (c < r)
    return r, c, incl, strict


def _ada_kernel(c_ref, w_ref, b_ref, o_ref):
    o_ref[0] = _mm_hp(_silu(c_ref[...]), w_ref[0]) + b_ref[0]


def _ada_call(c_all, ada_w, ada_b):
    nl, d, n3 = ada_w.shape
    r = c_all.shape[0]
    tn = 1024
    return pl.pallas_call(
        _ada_kernel,
        grid=(nl, n3 // tn),
        in_specs=[pl.BlockSpec((r, d), lambda l, j: (0, 0)),
                  pl.BlockSpec((1, d, tn), lambda l, j: (l, 0, j)),
                  pl.BlockSpec((1, 1, tn), lambda l, j: (l, 0, j))],
        out_specs=pl.BlockSpec((1, r, tn), lambda l, j: (l, 0, j)),
        out_shape=jax.ShapeDtypeStruct((nl, r, n3), F32),
        compiler_params=_params(("arbitrary", "arbitrary")),
        name="ada_rows",
    )(c_all, ada_w, ada_b.reshape(nl, 1, n3))


def _row_select(t, tm, lc, ctx_row, lat_row):
    rows = t * tm + lax.broadcasted_iota(jnp.int32, (tm, 1), 0)
    return jnp.where(rows < lc, ctx_row, lat_row)


def _prenorm_kernel(h_ref, modx_ref, modc_ref, nw_ref, o_ref, *, lc, tm):
    t = pl.program_id(1)
    x = h_ref[0]
    y = x * lax.rsqrt(jnp.mean(x * x, axis=-1, keepdims=True) + EPS) * nw_ref[...]
    mx = modx_ref[0]
    mc = modc_ref[...]
    d = D_MODEL
    shift = _row_select(t, tm, lc, mc[:, :d], mx[:, :d])
    scale = _row_select(t, tm, lc, mc[:, d:2 * d], mx[:, d:2 * d])
    o_ref[0] = (y * (1.0 + scale) + shift).astype(o_ref.dtype)


def _prenorm_call(h, modx, modc, norm_w, lc):
    b, l, d = h.shape
    tm = _pick(l, (768, 512, 384, 256, 128))
    return pl.pallas_call(
        functools.partial(_prenorm_kernel, lc=lc, tm=tm),
        grid=(b, l // tm),
        in_specs=[pl.BlockSpec((1, tm, d), lambda i, t: (i, t, 0)),
                  pl.BlockSpec((1, 1, 3 * d), lambda i, t: (i, 0, 0)),
                  pl.BlockSpec((1, 3 * d), lambda i, t: (0, 0)),
                  pl.BlockSpec((1, d), lambda i, t: (0, 0))],
        out_specs=pl.BlockSpec((1, tm, d), lambda i, t: (i, t, 0)),
        out_shape=jax.ShapeDtypeStruct((b, l, d), BF16),
        compiler_params=_params(("parallel", "parallel")),
        name="prenorm",
    )(h, modx, modc, norm_w.reshape(1, d))


def _proj_kernel(x_ref, w_ref, o_ref):
    o_ref[...] = jnp.dot(x_ref[...], w_ref[...], preferred_element_type=F32).astype(o_ref.dtype)


def _proj_call(x, w, out_dtype, name):
    m, k = x.shape
    n = w.shape[1]
    tm = _pick(m, (1024, 768, 512, 384, 256, 128))
    tn = _pick(n, (1536, 1024, 768, 512, 256, 128))
    return pl.pallas_call(
        _proj_kernel,
        grid=(m // tm, n // tn),
        in_specs=[pl.BlockSpec((tm, k), lambda i, j: (i, 0)),
                  pl.BlockSpec((k, tn), lambda i, j: (0, j))],
        out_specs=pl.BlockSpec((tm, tn), lambda i, j: (i, j)),
        out_shape=jax.ShapeDtypeStruct((m, n), out_dtype),
        compiler_params=_params(("parallel", "parallel")),
        name=name,
    )(x, w)


def _bwd_chunk(i, ncc, nc):
    return jnp.where(i < ncc, ncc - 1 - i, ncc + nc - 1 - i)


def _chunk_rows(c):
    return pl.ds(pl.multiple_of(c * CHUNK, CHUNK), CHUNK)


def _gla_kernel(q_ref, k_ref, v_ref, sm_ref, w2_ref, gb_ref, o_ref, la_ref, st_ref, *, ncc, nc):
    sm = sm_ref[0]
    for d in range(2):
        logit = _mm3(sm, w2_ref[d]) + gb_ref[d]
        la_ref[d] = _log_sigmoid(logit) * (1.0 / GLA_NORMALIZER)
    st_ref[...] = jnp.zeros_like(st_ref)
    o_ref[...] = jnp.zeros_like(o_ref)
    masks = [_tri_masks(CHUNK, rev) for rev in (False, True)]
    cum_mats = [m[2].astype(F32) for m in masks]
    probs = [(d, hh) for d in range(2) for hh in range(GLA_HB)]

    kl = [slice(hh * GLA_DK, (hh + 1) * GLA_DK) for _, hh in probs]
    vl = [slice(hh * GLA_DV, (hh + 1) * GLA_DV) for _, hh in probs]
    unroll = 2 if nc % 2 == 0 else 1

    def body(i, carry):
        steps = range(unroll)
        rows = [[_chunk_rows(i * unroll + u), _chunk_rows(_bwd_chunk(i * unroll + u, ncc, nc))] for u in steps]
        bcum_d = [[_cum_rows(cum_mats[d], la_ref[d, rows[u][d], :]) for d in range(2)] for u in steps]
        bcum = [[bcum_d[u][d][:, ln] for (d, _), ln in zip(probs, kl)] for u in steps]
        b_last = [[bc[CHUNK - 1:CHUNK] if d == 0 else bc[0:1] for (d, _), bc in zip(probs, bcum[u])] for u in steps]
        q = [[q_ref[0, rows[u][d], ln].astype(F32) * GLA_DK ** -0.5 for (d, _), ln in zip(probs, kl)] for u in steps]
        k = [[k_ref[0, rows[u][d], ln].astype(F32) for (d, _), ln in zip(probs, kl)] for u in steps]
        v = [[v_ref[0, rows[u][d], ln].astype(F32) for (d, _), ln in zip(probs, vl)] for u in steps]
        q_dec = [[a * jnp.exp(bc) for a, bc in zip(q[u], bcum[u])] for u in steps]
        k_inv = [[a * jnp.exp(-bc) for a, bc in zip(k[u], bcum[u])] for u in steps]
        k_end = [[a * jnp.exp(bl - bc) for a, bl, bc in zip(k[u], b_last[u], bcum[u])] for u in steps]
        st = [st_ref[j] for j in range(len(probs))]
        qk = [_each(_mm_nt, q_dec[u], k_inv[u]) for u in steps]
        o_inter = [_each(_mm_nt, q_dec[0], st)]
        upd = [_each(_mm_tn, v[u], k_end[u]) for u in steps]
        att = [[jnp.where(masks[d][2], a, 0.0) for (d, _), a in zip(probs, qk[u])] for u in steps]
        o_intra = [_each(_mm, att[u], v[u]) for u in steps]
        for u in steps:
            st = [sx * jnp.exp(bl) + up for sx, bl, up in zip(st, b_last[u], upd[u])]
            if u + 1 < unroll:
                o_inter.append(_each(_mm_nt, q_dec[u + 1], st))
        for j, sx in enumerate(st):
            st_ref[j] = sx
        for u in steps:
            for (d, _), ln, a, b in zip(probs, vl, o_intra[u], o_inter[u]):
                o_ref[0, rows[u][d], ln] += a + b
        return carry

    lax.fori_loop(0, nc // unroll, body, 0)


def _gla_call(p_gla, p_mla, w2e, gate_b, lc):
    b, l, _ = p_gla.shape
    nc, ncc = l // CHUNK, lc // CHUNK
    nhb = GLA_HEADS // GLA_HB
    wk, wv = GLA_HB * GLA_DK, GLA_HB * GLA_DV
    return pl.pallas_call(
        functools.partial(_gla_kernel, ncc=ncc, nc=nc),
        grid=(b, nhb),
        in_specs=[pl.BlockSpec((1, l, wk), lambda i, h: (i, 0, h)),
                  pl.BlockSpec((1, l, wk), lambda i, h: (i, 0, nhb + h)),
                  pl.BlockSpec((1, l, wv), lambda i, h: (i, 0, nhb + h)),
                  pl.BlockSpec((1, l, 128), lambda i, h: (i, 0, SMALL_BLOCK)),
                  pl.BlockSpec((2, 128, wk), lambda i, h: (0, 0, h)),
                  pl.BlockSpec((2, 1, wk), lambda i, h: (0, 0, h))],
        out_specs=pl.BlockSpec((1, l, wv), lambda i, h: (i, 0, h)),
        out_shape=jax.ShapeDtypeStruct((b, l, GLA_HEADS * GLA_DV), F32),
        scratch_shapes=[pltpu.VMEM((2, l, wk), F32), pltpu.VMEM((2 * GLA_HB, GLA_DV, GLA_DK), F32)],
        compiler_params=_params(("parallel", "parallel")),
        name="gla_scan",
    )(p_gla, p_gla, p_gla, p_mla, w2e, gate_b)


def _gdn_prep_kernel(x_ref, w_ref, o_ref, *, lc, l):
    half = GDN_CONV // 2
    nblk_head = GDN_HEADS * GDN_DK // 128
    grp, halo = 16, 8
    zeros = jnp.zeros((halo, 128), F32)
    for g in range(GDN_PREP_W // 128):
        j = pl.program_id(1) * (GDN_PREP_W // 128) + g
        ln = slice(g * 128, (g + 1) * 128)
        w = w_ref[:, ln]

        def finish(acc):
            y = _silu(acc)
            inv = lax.rsqrt(jnp.sum(y * y, axis=-1, keepdims=True) + EPS)
            fac = jnp.where(j < nblk_head, inv * GDN_DK ** -0.5, jnp.where(j < 2 * nblk_head, inv, 1.0))
            return (y * fac).astype(o_ref.dtype)

        acc = None
        for s in range(-half, half + 1):
            term = x_ref[0, grp + s:l - grp + s, ln].astype(F32) * w[s + half:s + half + 1]
            acc = term if acc is None else acc + term
        o_ref[0, grp:l - grp, ln] = finish(acc)

        for r0 in sorted({0, lc - grp, lc, l - grp}):
            lo, hi = max(r0 - halo, 0), min(r0 + grp + halo, l)
            slab = x_ref[0, lo:hi, ln].astype(F32)
            if r0 - halo < 0:
                slab = jnp.concatenate([zeros, slab], axis=0)
            if r0 + grp + halo > l:
                slab = jnp.concatenate([slab, zeros], axis=0)
            n = grp + 2 * halo
            t = r0 + lax.broadcasted_iota(jnp.int32, (grp, 128), 0)
            start = jnp.where(t < lc, 0, lc)
            end = jnp.where(t < lc, lc, l)
            acc = slab[halo:halo + grp] * w[half:half + 1]
            for s in range(-half, half + 1):
                if s != 0:
                    xs = pltpu.roll(slab, (-s) % n, axis=0)[halo:halo + grp]
                    ok = (t + s >= start) if s < 0 else (t + s < end)
                    acc = acc + jnp.where(ok, xs, 0.0) * w[s + half:s + half + 1]
            o_ref[0, r0:r0 + grp, ln] = finish(acc)


def _gdn_prep_call(p_gdn, conv_w, lc):
    b, l, n = p_gdn.shape
    wd = GDN_PREP_W
    return pl.pallas_call(
        functools.partial(_gdn_prep_kernel, lc=lc, l=l),
        grid=(b, n // wd),
        in_specs=[pl.BlockSpec((1, l, wd), lambda i, j: (i, 0, j)),
                  pl.BlockSpec((GDN_CONV, wd), lambda i, j: (0, j))],
        out_specs=pl.BlockSpec((1, l, wd), lambda i, j: (i, 0, j)),
        out_shape=jax.ShapeDtypeStruct((b, l, n), BF16),
        compiler_params=_params(("parallel", "parallel")),
        name="gdn_prep",
    )(p_gdn, conv_w)


def _each(fn, *lists):
    return [fn(*args) for args in zip(*lists)]


class _TriConsts:
    def __init__(self):
        r = lax.broadcasted_iota(jnp.int32, (CHUNK, 2 * CHUNK), 0)
        lane = lax.broadcasted_iota(jnp.int32, (CHUNK, 2 * CHUNK), 1)
        c = lane % CHUNK
        same = lambda s: (r // s) == (c // s)
        nb = _TRI_BLOCK
        self.left = lane < CHUNK
        self.diag_blocks = same(nb)[:, :CHUNK]
        self.eye = (r == c).astype(F32)[:, :CHUNK]
        right = jnp.logical_not(self.left)
        self.first_is_eye = self.left & (r == c)
        self.first_is_lm = right & jnp.logical_not(same(nb))
        sizes = [nb * 2 ** j for j in range(int(math.log2(CHUNK // nb)))]
        self.level = [right & same(2 * s) & jnp.logical_not(same(s)) for s in sizes]


def _unit_tri_solves(lms, rhs, tc):
    nb = _TRI_BLOCK
    y = [-jnp.where(tc.diag_blocks, lm[:, :CHUNK], 0.0) for lm in lms]
    t = [tc.eye + a for a in y]
    p = y
    for _ in range(int(math.log2(nb)) - 1):
        p = _each(_MMX, p, p)
        t = _each(lambda a, b: a + _MMX(a, b), t, p)
    first = [jnp.where(tc.first_is_eye, 1.0, jnp.where(tc.first_is_lm, lm, 0.0)) for lm in lms]
    x = _each(_mm, t, first)
    zero = jnp.zeros((CHUNK, 2 * CHUNK), F32)
    for mask in tc.level:
        m = [jnp.where(mask, a, 0.0) for a in x]
        x = _each(lambda a, mj: a - _mm(mj, jnp.concatenate([zero, a], axis=0)), x, m)
    t_full = [jnp.where(tc.left, a, 0.0) for a in x]
    zero_rhs = jnp.zeros_like(rhs[0])
    return _each(lambda a, b: _mm(a, jnp.concatenate([b, zero_rhs], axis=0)), t_full, rhs)


def _gdn_kernel(rate_ref, ratel_ref, q_ref, k_ref, v_ref, abc_ref, abr_ref, o_ref, s_ref, *, ncc, nc):
    na = 2 * GDN_HB
    s_ref[...] = jnp.zeros_like(s_ref)
    o_ref[...] = jnp.zeros_like(o_ref)
    masks = [_tri_masks(CHUNK, rev) for rev in (False, True)]
    cum_mats = [m[2].astype(F32) for m in masks]
    r2 = lax.broadcasted_iota(jnp.int32, (CHUNK, 2 * CHUNK), 0)
    c2 = lax.broadcasted_iota(jnp.int32, (CHUNK, 2 * CHUNK), 1) % CHUNK
    incl2 = [c2 <= r2, c2 >= r2]
    strict2 = [c2 < r2, c2 > r2]
    cum_twice = [m.astype(F32) for m in incl2]
    tri = _TriConsts()
    unroll = GDN_UNROLL if nc % GDN_UNROLL == 0 else 1
    streams = [(u, d) for u in range(unroll) for d in range(2)]
    probs = [(si, hh) for si in range(len(streams)) for hh in range(GDN_HB)]
    pdir = [streams[si][1] for si, _ in probs]
    neg_rate_rows = -jnp.exp(rate_ref[0, 0])[:, :CHUNK]
    dtb_rows = rate_ref[0, 1][:, :CHUNK]
    neg_rate_lanes = -jnp.exp(ratel_ref[0, 0, 0:1, :])
    dtb_lanes = ratel_ref[0, 1, 0:1, :]

    def body(i, carry):
        dirs = range(len(streams))
        sdir = [d for _, d in streams]
        chunk = [i * unroll + u if d == 0 else _bwd_chunk(i * unroll + u, ncc, nc) for u, d in streams]
        rows = [_chunk_rows(c) for c in chunk]
        abc = [abc_ref[0, 0, rw, :] for rw in rows]
        abr = [abr_ref[0, 0, c] for c in chunk]
        incl = [incl2[d] for d in pdir]
        strict = [strict2[d] for d in pdir]
        g_cols = [neg_rate_lanes * _softplus(abc[si] + dtb_lanes) for si in dirs]
        g_rows = [neg_rate_rows * _softplus(abr[si] + dtb_rows) for si in dirs]
        gc_cols = [_cum_rows(cum_mats[sdir[si]], g_cols[si]) for si in dirs]
        gc_rows = [_cum_cols(g_rows[si], cum_twice[1 - sdir[si]]) for si in dirs]
        beta_cols = [jax.nn.sigmoid(abc[si]) for si in dirs]
        col = [d * GDN_HB + hh for d, (_, hh) in zip(pdir, probs)]
        gc_col = [jnp.broadcast_to(gc_cols[si][:, j:j + 1], (CHUNK, 128)) for (si, _), j in zip(probs, col)]
        gc_row = [gc_rows[si][j:j + 1, :] for (si, _), j in zip(probs, col)]
        beta_col = [beta_cols[si][:, na + j:na + j + 1] for (si, _), j in zip(probs, col)]
        g_last = [gc[CHUNK - 1:CHUNK] if d == 0 else gc[0:1] for d, gc in zip(pdir, gc_col)]
        decay = [jnp.where(m, jnp.exp(jnp.where(m, a - b, 0.0)), 0.0)
                 for m, a, b in zip(incl, gc_col, gc_row)]

        lanes = [slice(hh * GDN_DK, (hh + 1) * GDN_DK) for _, hh in probs]
        q = [q_ref[0, rows[si], ln].astype(F32) for (si, _), ln in zip(probs, lanes)]
        k = [k_ref[0, rows[si], ln].astype(F32) for (si, _), ln in zip(probs, lanes)]
        v = [v_ref[0, rows[si], ln].astype(F32) for (si, _), ln in zip(probs, lanes)]
        k_beta = _each(jnp.multiply, k, beta_col)
        v_beta = _each(jnp.multiply, v, beta_col)
        e_col = _each(jnp.exp, gc_col)
        kq = _each(_mm_nt, [jnp.concatenate([a, b], axis=0) for a, b in zip(k_beta, q)],
                   [jnp.concatenate([a, a], axis=0) for a in k])
        kk = [x[:CHUNK] for x in kq]
        qk = [x[CHUNK:] for x in kq]
        lm = [jnp.where(m, a * dc, 0.0) for m, a, dc in zip(strict, kk, decay)]
        rhs = [jnp.concatenate([vb, kb * e], axis=1) for vb, kb, e in zip(v_beta, k_beta, e_col)]
        uw = _unit_tri_solves(lm, rhs, tri)
        u = [x[:, :GDN_DV] for x in uw]
        w = [x[:, GDN_DV:] for x in uw]
        att = [jnp.where(m, a * dc, 0.0)[:, :CHUNK] for m, a, dc in zip(incl, qk, decay)]
        q_dec = _each(jnp.multiply, q, e_col)
        k_end = [kx * jnp.exp(gl - gc) for kx, gl, gc in zip(k, g_last, gc_col)]
        nstate = 2 * GDN_HB
        s = [s_ref[j] for j in range(nstate)]
        for pos in range(unroll):
            sel = [p for p, (si, _) in enumerate(probs) if streams[si][0] == pos]
            pick = lambda xs: [xs[p] for p in sel]
            wq = _each(_mm, [jnp.concatenate([a, b], axis=0) for a, b in zip(pick(w), pick(q_dec))], s)
            ws = [x[:CHUNK] for x in wq]
            o_inter = [x[CHUNK:] for x in wq]
            v_new = _each(jnp.subtract, pick(u), ws)
            upd = _each(_mm_tn, pick(k_end), v_new)
            o_intra = _each(_mm, pick(att), v_new)
            s = [sx * jnp.exp(gl) + up for sx, gl, up in zip(s, pick(g_last), upd)]
            for p, a, b in zip(sel, o_intra, o_inter):
                o_ref[0, rows[probs[p][0]], lanes[p]] += a + b
        for j in range(nstate):
            s_ref[j] = s[j]
        return carry

    lax.fori_loop(0, nc // unroll, body, 0)


def _gdn_call(qkvn, abc, abr, a_log, dt_bias, lc):
    b, l, _ = qkvn.shape
    nc, ncc = l // CHUNK, lc // CHUNK
    nhb = GDN_HEADS // GDN_HB
    wb = GDN_HB * GDN_DK
    na = 2 * GDN_HB
    rate = jnp.stack([a_log, dt_bias]).reshape(2, 2, nhb, GDN_HB).transpose(2, 0, 1, 3).reshape(nhb, 2, na)
    rate = jnp.pad(rate, ((0, 0), (0, 0), (0, 128 - na)))
    nr = _GDN_GATE_ROWS
    rate_rows = jnp.broadcast_to(rate[:, :, :nr, None], (nhb, 2, nr, 128))
    rate_lanes = jnp.broadcast_to(rate[:, :, None, :], (nhb, 2, 8, 128))
    return pl.pallas_call(
        functools.partial(_gdn_kernel, ncc=ncc, nc=nc),
        grid=(b, nhb),
        in_specs=[pl.BlockSpec((1, 2, nr, 128), lambda i, h: (h, 0, 0, 0)),
                  pl.BlockSpec((1, 2, 8, 128), lambda i, h: (h, 0, 0, 0)),
                  pl.BlockSpec((1, l, wb), lambda i, h: (i, 0, h)),
                  pl.BlockSpec((1, l, wb), lambda i, h: (i, 0, nhb + h)),
                  pl.BlockSpec((1, l, wb), lambda i, h: (i, 0, 2 * nhb + h)),
                  pl.BlockSpec((1, 1, l, 128), lambda i, h: (i, h, 0, 0)),
                  pl.BlockSpec((1, 1, nc, nr, CHUNK), lambda i, h: (i, h, 0, 0, 0))],
        out_specs=pl.BlockSpec((1, l, wb), lambda i, h: (i, 0, h)),
        out_shape=jax.ShapeDtypeStruct((b, l, GDN_HEADS * GDN_DV), F32),
        scratch_shapes=[pltpu.VMEM((2 * GDN_HB, GDN_DK, GDN_DV), F32)],
        compiler_params=_params(("parallel", "parallel")),
        name="gdn_scan",
    )(rate_rows, rate_lanes, qkvn, qkvn, qkvn, abc, abr)


def _gdn_gate_layouts(p_mla, lc):
    b, l, _ = p_mla.shape
    nhb = GDN_HEADS // GDN_HB
    na = 2 * GDN_HB
    o = MLA_Q_LORA + MLA_KV_LORA
    ab = p_mla[:, :, o + SM_A:o + SM_A + 32].reshape(b, l, 2, 2, nhb, GDN_HB)
    ab = ab.transpose(0, 4, 1, 2, 3, 5).reshape(b, nhb, l, 2 * na)
    abr = ab[..., :na].reshape(b, nhb, l // CHUNK, CHUNK, na).transpose(0, 1, 2, 4, 3)
    abr = jnp.pad(abr, ((0, 0), (0, 0), (0, 0), (0, _GDN_GATE_ROWS - na), (0, 0)))
    abc = jnp.pad(ab, ((0, 0), (0, 0), (0, 0), (0, 128 - 2 * na)))
    return abc, abr


def _rope_swap(x):
    lane = lax.broadcasted_iota(jnp.int32, x.shape, 1)
    n = x.shape[1]
    return jnp.where(lane % 32 < 16, pltpu.roll(x, n - 16, axis=1), pltpu.roll(x, 16, axis=1))


def _mla_prep_kernel(p_ref, cos_ref, sin_ref, qnw_ref, qup_ref, kvnw_ref, kvup_ref,
                     wqn_ref, wqr_ref, wkn_ref, wkr_ref, q_ref, k_ref, v_ref):
    p = p_ref[0]
    cos_t = cos_ref[...]
    sin_t = sin_ref[...]

    def rms(x, n):
        return x * lax.rsqrt(jnp.sum(x * x, axis=-1, keepdims=True) * (1.0 / n) + EPS)

    def rope(x):
        return x * cos_t + _rope_swap(x) * sin_t

    qd = p[:, :MLA_Q_LORA]
    kvd = p[:, MLA_Q_LORA:MLA_Q_LORA + MLA_KV_LORA]
    sm = p[:, MLA_Q_LORA + MLA_KV_LORA:]
    q = _mm(rms(qd, MLA_Q_LORA) * qnw_ref[...], qup_ref[...])
    kv = _mm(rms(kvd, MLA_KV_LORA) * kvnw_ref[...], kvup_ref[...])
    lane = lax.broadcasted_iota(jnp.int32, sm.shape, 1)
    kr = jnp.where(lane < MLA_ROPE, sm, 0.0)
    kr = rope(rms(kr, MLA_ROPE) * wkr_ref[...])
    for h in range(MLA_HEADS):
        base = h * 256
        qn = rms(q[:, base:base + 128], MLA_NOPE) * wqn_ref[...]
        q_ref[0, :, base:base + 128] = (qn * MLA_Q_PRESCALE).astype(q_ref.dtype)
        qr = rms(q[:, base + 128:base + 256], MLA_ROPE) * wqr_ref[...]
        q_ref[0, :, base + 128:base + 256] = (rope(qr) * MLA_Q_PRESCALE).astype(q_ref.dtype)
        kn = kv[:, h * 128:(h + 1) * 128]
        k_ref[0, :, base:base + 128] = (rms(kn, MLA_NOPE) * wkn_ref[...]).astype(k_ref.dtype)
        k_ref[0, :, base + 128:base + 256] = kr.astype(k_ref.dtype)
    v_ref[0] = kv[:, MLA_HEADS * MLA_NOPE:].T.astype(v_ref.dtype)


def _mla_prep_call(p_mla, cos_t, sin_t, qnw, qup, kvnw, kvup, wqn, wqr, wkn, wkr):
    b, l, gw = p_mla.shape
    tm = _pick(l, (256, 128))
    hq = MLA_HEADS * 256
    full = lambda shape: pl.BlockSpec(shape, lambda i, t: tuple(0 for _ in shape))
    return pl.pallas_call(
        _mla_prep_kernel,
        grid=(b, l // tm),
        in_specs=[pl.BlockSpec((1, tm, gw), lambda i, t: (i, t, 0)),
                  pl.BlockSpec((tm, 128), lambda i, t: (t, 0)),
                  pl.BlockSpec((tm, 128), lambda i, t: (t, 0)),
                  full((1, MLA_Q_LORA)), full((MLA_Q_LORA, hq)),
                  full((1, MLA_KV_LORA)), full((MLA_KV_LORA, hq)),
                  full((1, 128)), full((1, 128)), full((1, 128)), full((1, 128))],
        out_specs=[pl.BlockSpec((1, tm, hq), lambda i, t: (i, t, 0)),
                   pl.BlockSpec((1, tm, hq), lambda i, t: (i, t, 0)),
                   pl.BlockSpec((1, MLA_HEADS * MLA_DV, tm), lambda i, t: (i, 0, t))],
        out_shape=[jax.ShapeDtypeStruct((b, l, hq), BF16),
                   jax.ShapeDtypeStruct((b, l, hq), BF16),
                   jax.ShapeDtypeStruct((b, MLA_HEADS * MLA_DV, l), BF16)],
        compiler_params=_params(("parallel", "parallel")),
        name="mla_prep",
    )(p_mla, cos_t, sin_t, qnw, qup, kvnw, kvup, wqn, wqr, wkn, wkr)


def _mla_attn_kernel(q_ref, k_ref, vt_ref, o_ref, *, lc, tq, q_off):
    qi = pl.program_id(2) + q_off
    heads = range(MLA_HP)

    def attend(nk):
        def scores(h):
            return lax.dot_general(k_ref[0, :nk, h * 256:(h + 1) * 256], q_ref[0, :, h * 256:(h + 1) * 256],
                                   (((1,), (1,)), ((), ())), preferred_element_type=F32)

        ahead = MLA_HP
        st = {h: scores(h) for h in range(min(ahead, MLA_HP))}
        for h in heads:
            x = st.pop(h)
            e = jnp.exp2(x - jnp.max(x, axis=0, keepdims=True))
            den = jnp.sum(e, axis=0, keepdims=True)
            if h + ahead < MLA_HP:
                st[h + ahead] = scores(h + ahead)
            ot = jnp.dot(vt_ref[0, h * MLA_DV:(h + 1) * MLA_DV, :nk], e.astype(BF16),
                         preferred_element_type=F32)
            o_ref[0, :, h * MLA_DV:(h + 1) * MLA_DV] = (ot / den).T.astype(o_ref.dtype)

    if q_off == 0:
        @pl.when(qi * tq < lc)
        def _():
            attend(lc)

        @pl.when(qi * tq >= lc)
        def _():
            attend(k_ref.shape[1])
    else:
        attend(k_ref.shape[1])


def _mla_attn_call(q, k, vt, lc, with_ctx):
    b, l, _ = q.shape
    tq = _pick(lc, (256, 128))
    q_off = 0 if with_ctx else lc // tq
    nq = l // tq - q_off
    return pl.pallas_call(
        functools.partial(_mla_attn_kernel, lc=lc, tq=tq, q_off=q_off),
        grid=(b, MLA_HEADS // MLA_HP, nq),
        in_specs=[pl.BlockSpec((1, tq, MLA_HP * 256), lambda i, h, t: (i, t + q_off, h)),
                  pl.BlockSpec((1, l, MLA_HP * 256), lambda i, h, t: (i, 0, h)),
                  pl.BlockSpec((1, MLA_HP * MLA_DV, l), lambda i, h, t: (i, h, 0))],
        out_specs=pl.BlockSpec((1, tq, MLA_HP * MLA_DV), lambda i, h, t: (i, t, h)),
        out_shape=jax.ShapeDtypeStruct((b, nq * tq, MLA_HEADS * MLA_DV), F32),
        compiler_params=_params(("parallel", "parallel", "arbitrary")),
        name="mla_attn",
    )(q, k, vt)


def _merge_kernel(h_ref, oa_ref, ob_ref, oc_ref, za_ref, zb_ref, zc_ref, g0_ref, g1_ref, g2_ref,
                  modx_ref, modc_ref, wa_ref, wb_ref, wbr_ref, wout_ref, o_ref, *, lc, tm, t_off):
    t = pl.program_id(1) + t_off

    def head_norm(o, width):
        parts = []
        for h in range(BRANCH_W // width):
            x = o[:, h * width:(h + 1) * width]
            parts.append(x * lax.rsqrt(jnp.mean(x * x, axis=-1, keepdims=True) + EPS))
        return jnp.concatenate(parts, axis=-1)

    ya = head_norm(oa_ref[0], GLA_DV) * wa_ref[...] * _silu(za_ref[0].astype(F32))
    yb = head_norm(ob_ref[0], GDN_DV) * wb_ref[...] * _silu(zb_ref[0].astype(F32))
    yc = oc_ref[0] * _silu(zc_ref[0].astype(F32))
    acc = _sigmoid(g0_ref[0].astype(F32)) * _mm(ya, wbr_ref[0])
    acc = acc + _sigmoid(g1_ref[0].astype(F32)) * _mm(yb, wbr_ref[1])
    acc = acc + _sigmoid(g2_ref[0].astype(F32)) * _mm(yc, wbr_ref[2])
    d = D_MODEL
    gate = _row_select(t, tm, lc, modc_ref[:, 2 * d:], modx_ref[0][:, 2 * d:])
    o_ref[0] = h_ref[0] + gate * _mm(acc, wout_ref[...])


def _merge_call(h, oa, ob, oc, p_zg, modx, modc, wa, wb, w_branch, w_out, lc, with_ctx):
    b, l, d = h.shape
    tm = _pick(lc, (256, 128))
    t_off = 0 if with_ctx else lc // tm
    nt = l // tm - t_off
    oc_off = 0 if oc.shape[1] == l else lc // tm
    row = lambda i, t: (i, t + t_off, 0)
    zg = lambda n: pl.BlockSpec((1, tm, d), lambda i, t: (i, t + t_off, n))
    return pl.pallas_call(
        functools.partial(_merge_kernel, lc=lc, tm=tm, t_off=t_off),
        grid=(b, nt),
        in_specs=[pl.BlockSpec((1, tm, d), row), pl.BlockSpec((1, tm, d), row), pl.BlockSpec((1, tm, d), row),
                  pl.BlockSpec((1, tm, d), lambda i, t: (i, t + t_off - oc_off, 0)),
                  zg(0), zg(1), zg(2), zg(3), zg(4), zg(5),
                  pl.BlockSpec((1, 1, 3 * d), lambda i, t: (i, 0, 0)),
                  pl.BlockSpec((1, 3 * d), lambda i, t: (0, 0)),
                  pl.BlockSpec((1, d), lambda i, t: (0, 0)),
                  pl.BlockSpec((1, d), lambda i, t: (0, 0)),
                  pl.BlockSpec((3, d, d), lambda i, t: (0, 0, 0)),
                  pl.BlockSpec((d, d), lambda i, t: (0, 0))],
        out_specs=pl.BlockSpec((1, tm, d), lambda i, t: (i, t, 0)),
        out_shape=jax.ShapeDtypeStruct((b, nt * tm, d), F32),
        compiler_params=_params(("parallel", "parallel")),
        name="merge",
    )(h, oa, ob, oc, p_zg, p_zg, p_zg, p_zg, p_zg, p_zg, modx, modc, wa, wb, w_branch, w_out)


def _cols(w, *names):
    return jnp.concatenate([w[:, _IN_OFF[n]:_IN_OFF[n] + _IN_LEN[n]] for n in names], axis=1)


def _rope_tables(t_lat, lc):
    rows = t_lat // GRID_W
    row = jnp.repeat(jnp.arange(rows, dtype=F32), GRID_W)
    col = jnp.tile(jnp.arange(GRID_W, dtype=F32), rows)
    inv_freq = jnp.power(ROPE_BASE, -jnp.arange(ROPE_FREQ, dtype=F32) / ROPE_FREQ)
    ar = row[:, None] * inv_freq
    ac = col[:, None] * inv_freq
    zeros = jnp.zeros((t_lat, 128 - MLA_ROPE), F32)
    cos_t = jnp.concatenate([jnp.cos(ar), jnp.cos(ar), jnp.cos(ac), jnp.cos(ac), zeros], axis=1)
    sin_t = jnp.concatenate([-jnp.sin(ar), jnp.sin(ar), -jnp.sin(ac), jnp.sin(ac), zeros], axis=1)
    cos_c = jnp.concatenate([jnp.ones((lc, MLA_ROPE), F32), jnp.zeros((lc, 128 - MLA_ROPE), F32)], axis=1)
    return (jnp.concatenate([cos_c, cos_t], axis=0),
            jnp.concatenate([jnp.zeros((lc, 128), F32), sin_t], axis=0))


def _pad_lanes(w, n=128):
    return jnp.pad(w, (0, n - w.shape[0])).reshape(1, n)


def kernel(x, c, ctx, c_ctx, norm_w, ada_w, ada_b, w_in, gla_gate_w2, gla_gate_b, gla_norm_w, gdn_conv_w,
           gdn_a_log, gdn_dt_bias, gdn_norm_w, mla_q_norm_w, mla_q_up, mla_kv_norm_w, mla_kv_up,
           mla_qn_nope, mla_qn_rope, mla_kn_nope, mla_kn_rope, w_branch, w_out):
    b, t_lat, d = x.shape
    lc = ctx.shape[1]
    l = lc + t_lat
    depth = w_in.shape[0]
    assert d == D_MODEL and lc % 128 == 0 and t_lat % 128 == 0

    h = jnp.concatenate([ctx, x], axis=1)
    rows = ((b + 1 + 7) // 8) * 8
    c_all = jnp.concatenate([c, c_ctx[None, :], jnp.zeros((rows - b - 1, d), F32)], axis=0)
    mod = _ada_call(c_all, ada_w, ada_b)
    cos_t, sin_t = _rope_tables(t_lat, lc)

    for li in range(depth):
        last = li == depth - 1
        modx = mod[li, :b].reshape(b, 1, 3 * d)
        modc = mod[li, b:b + 1]
        w = w_in[li]
        w_mla = _cols(w, 'mla_q_down', 'mla_kv_down', 'mla_k_rope', 'gla_gate_lr', 'gdn_a', 'gdn_b').astype(BF16)
        w_gla = _cols(w, 'gla_q', 'gla_k', 'gla_v').astype(BF16)
        w_gdn = _cols(w, 'gdn_qkv').astype(BF16)
        w_zg = _cols(w, 'gla_z', 'gdn_z', 'mla_z', 'merge_gate').astype(BF16)

        hn = _prenorm_call(h, modx, modc, norm_w[li], lc).reshape(b * l, d)
        p_mla = _proj_call(hn, w_mla, F32, "proj_mla").reshape(b, l, -1)
        p_gla = _proj_call(hn, w_gla, BF16, "proj_gla").reshape(b, l, -1)
        p_gdn = _proj_call(hn, w_gdn, F32, "proj_gdn").reshape(b, l, -1)
        p_zg = _proj_call(hn, w_zg, BF16, "proj_zg").reshape(b, l, -1)

        w2e = jnp.stack([jnp.pad(gla_gate_w2[li, dd], ((SM_LR + dd * GLA_RANK, 128 - SM_LR - (dd + 1) * GLA_RANK),
                                                      (0, 0))) for dd in range(2)])
        oa = _gla_call(p_gla, p_mla, w2e, gla_gate_b[li].reshape(2, 1, -1), lc)

        qkvn = _gdn_prep_call(p_gdn, gdn_conv_w[li], lc)
        abc, abr = _gdn_gate_layouts(p_mla, lc)
        ob = _gdn_call(qkvn, abc, abr, gdn_a_log[li], gdn_dt_bias[li], lc)

        qup = mla_q_up[li].reshape(MLA_Q_LORA, MLA_HEADS, MLA_NOPE + MLA_ROPE)
        qup = jnp.pad(qup, ((0, 0), (0, 0), (0, 256 - MLA_NOPE - MLA_ROPE))).reshape(MLA_Q_LORA, -1).astype(BF16)
        kvup = mla_kv_up[li].reshape(MLA_KV_LORA, MLA_HEADS, MLA_NOPE + MLA_DV)
        kvup = jnp.concatenate([kvup[:, :, :MLA_NOPE].reshape(MLA_KV_LORA, -1),
                                kvup[:, :, MLA_NOPE:].reshape(MLA_KV_LORA, -1)], axis=1).astype(BF16)
        qm, km, vm = _mla_prep_call(
            p_mla, cos_t, sin_t, mla_q_norm_w[li].reshape(1, -1), qup, mla_kv_norm_w[li].reshape(1, -1), kvup,
            mla_qn_nope[li].reshape(1, -1), _pad_lanes(mla_qn_rope[li]),
            mla_kn_nope[li].reshape(1, -1), _pad_lanes(mla_kn_rope[li]))
        oc = _mla_attn_call(qm, km, vm, lc, with_ctx=not last)

        h = _merge_call(h, oa, ob, oc, p_zg, modx, modc,
                        jnp.tile(gla_norm_w[li], GLA_HEADS).reshape(1, -1),
                        jnp.tile(gdn_norm_w[li], GDN_HEADS).reshape(1, -1),
                        w_branch[li].astype(BF16), w_out[li].astype(BF16), lc, with_ctx=not last)
    return h
```

```python
import functools
import math

import numpy as np
import jax
import jax.numpy as jnp
from jax import lax
from jax.experimental import pallas as pl
from jax.experimental.pallas import tpu as pltpu

F32 = jnp.float32
BF16 = jnp.bfloat16

D_MODEL = 1024
EPS = 1e-6
GRID_W = 64
CHUNK = 64

GLA_HEADS, GLA_DK, GLA_DV, GLA_RANK, GLA_NORMALIZER = 4, 128, 256, 16, 16.0
GDN_HEADS, GDN_DK, GDN_DV, GDN_CONV = 8, 128, 128, 5
GLA_HB = 4
GDN_HB = 8
_GDN_GATE_ROWS = max(8, 2 * GDN_HB)
GDN_PREP_W = 128
GDN_UNROLL = 1
MLA_HEADS, MLA_Q_LORA, MLA_KV_LORA, MLA_NOPE, MLA_ROPE, MLA_DV = 8, 384, 256, 128, 64, 128
MLA_SCALE = (MLA_NOPE + MLA_ROPE) ** -0.5
MLA_Q_PRESCALE = MLA_SCALE * math.log2(math.e)
MLA_HP = 8
ROPE_FREQ = MLA_ROPE // 4
ROPE_BASE = 10000.0
BRANCH_W = 1024

_IN_SIZES = (512, 512, 1024, 32, 1024, 3072, 16, 16, 1024, 384, 256, 64, 1024, 3072)
_IN_NAMES = ('gla_q', 'gla_k', 'gla_v', 'gla_gate_lr', 'gla_z', 'gdn_qkv', 'gdn_a', 'gdn_b', 'gdn_z',
             'mla_q_down', 'mla_kv_down', 'mla_k_rope', 'mla_z', 'merge_gate')
_IN_OFF = dict(zip(_IN_NAMES, np.concatenate([[0], np.cumsum(_IN_SIZES)[:-1]]).tolist()))
_IN_LEN = dict(zip(_IN_NAMES, _IN_SIZES))

SM_ROPE, SM_LR, SM_A, SM_B = 0, 64, 96, 112
MLA_GROUP_W = MLA_Q_LORA + MLA_KV_LORA + 128
SMALL_BLOCK = (MLA_Q_LORA + MLA_KV_LORA) // 128

VMEM_LIMIT = 58 * 1024 * 1024


def _params(sem):
    return pltpu.CompilerParams(dimension_semantics=sem, vmem_limit_bytes=VMEM_LIMIT)


def _pick(n, cands):
    for c in cands:
        if n % c == 0:
            return c
    raise ValueError(f"no tile for {n} in {cands}")


def _mm(a, b):
    return jnp.dot(a.astype(BF16), b.astype(BF16), preferred_element_type=F32)


def _mm_nt(a, b):
    return lax.dot_general(a.astype(BF16), b.astype(BF16), (((1,), (1,)), ((), ())),
                           preferred_element_type=F32)


def _mm_tn(a, b):
    return lax.dot_general(a.astype(BF16), b.astype(BF16), (((0,), (0,)), ((), ())),
                           preferred_element_type=F32)


def _mm_hp(a, b):
    return jnp.dot(a, b, precision=lax.Precision.HIGHEST, preferred_element_type=F32)


def _split(a):
    hi = a.astype(BF16)
    lo = (a - hi.astype(F32)).astype(BF16)
    return hi, lo


def _mm3(a, b):
    ah, al = _split(a)
    bh, bl = _split(b)
    d = functools.partial(jnp.dot, preferred_element_type=F32)
    return d(ah, bh) + (d(ah, bl) + d(al, bh))


_MMX = _mm
_TRI_BLOCK = 2


def _cum_rows(cum, x):
    cb = cum.astype(BF16)
    d = functools.partial(jnp.dot, preferred_element_type=F32)
    hi, lo = _split(x)
    return d(cb, hi) + d(cb, lo)


def _cum_cols(x, cum):
    cb = cum.astype(BF16)
    d = functools.partial(jnp.dot, preferred_element_type=F32)
    hi, lo = _split(x)
    return d(hi, cb) + d(lo, cb)


def _softplus(x):
    return jnp.maximum(x, 0.0) + jnp.log(1.0 + jnp.exp(-jnp.abs(x)))


def _log_sigmoid(x):
    return jnp.minimum(x, 0.0) - jnp.log(1.0 + jnp.exp(-jnp.abs(x)))


_sigmoid = jax.nn.sigmoid


def _silu(x):
    return x * _sigmoid(x)


def _tri_masks(n, reverse):
    r = lax.broadcasted_iota(jnp.int32, (n, n), 0)
    c = lax.broadcasted_iota(jnp.int32, (n, n), 1)
    incl = (c >= r) if reverse else (c <= r)
    strict = (c > r) if reverse else (c < r)
    return r, c, incl, strict


def _ada_kernel(c_ref, w_ref, b_ref, o_ref):
    o_ref[0] = _mm_hp(_silu(c_ref[...]), w_ref[0]) + b_ref[0]


def _ada_call(c_all, ada_w, ada_b):
    nl, d, n3 = ada_w.shape
    r = c_all.shape[0]
    tn = 1024
    return pl.pallas_call(
        _ada_kernel,
        grid=(nl, n3 // tn),
        in_specs=[pl.BlockSpec((r, d), lambda l, j: (0, 0)),
                  pl.BlockSpec((1, d, tn), lambda l, j: (l, 0, j)),
                  pl.BlockSpec((1, 1, tn), lambda l, j: (l, 0, j))],
        out_specs=pl.BlockSpec((1, r, tn), lambda l, j: (l, 0, j)),
        out_shape=jax.ShapeDtypeStruct((nl, r, n3), F32),
        compiler_params=_params(("arbitrary", "arbitrary")),
        name="ada_rows",
    )(c_all, ada_w, ada_b.reshape(nl, 1, n3))


def _row_select(t, tm, lc, ctx_row, lat_row):
    rows = t * tm + lax.broadcasted_iota(jnp.int32, (tm, 1), 0)
    return jnp.where(rows < lc, ctx_row, lat_row)


def _prenorm_kernel(h_ref, modx_ref, modc_ref, nw_ref, o_ref, *, lc, tm):
    t = pl.program_id(1)
    x = h_ref[0]
    y = x * lax.rsqrt(jnp.mean(x * x, axis=-1, keepdims=True) + EPS) * nw_ref[...]
    mx = modx_ref[0]
    mc = modc_ref[...]
    d = D_MODEL
    shift = _row_select(t, tm, lc, mc[:, :d], mx[:, :d])
    scale = _row_select(t, tm, lc, mc[:, d:2 * d], mx[:, d:2 * d])
    o_ref[0] = (y * (1.0 + scale) + shift).astype(o_ref.dtype)


def _prenorm_call(h, modx, modc, norm_w, lc):
    b, l, d = h.shape
    tm = _pick(l, (768, 512, 384, 256, 128))
    return pl.pallas_call(
        functools.partial(_prenorm_kernel, lc=lc, tm=tm),
        grid=(b, l // tm),
        in_specs=[pl.BlockSpec((1, tm, d), lambda i, t: (i, t, 0)),
                  pl.BlockSpec((1, 1, 3 * d), lambda i, t: (i, 0, 0)),
                  pl.BlockSpec((1, 3 * d), lambda i, t: (0, 0)),
                  pl.BlockSpec((1, d), lambda i, t: (0, 0))],
        out_specs=pl.BlockSpec((1, tm, d), lambda i, t: (i, t, 0)),
        out_shape=jax.ShapeDtypeStruct((b, l, d), BF16),
        compiler_params=_params(("parallel", "parallel")),
        name="prenorm",
    )(h, modx, modc, norm_w.reshape(1, d))


def _proj_kernel(x_ref, w_ref, o_ref):
    o_ref[...] = jnp.dot(x_ref[...], w_ref[...], preferred_element_type=F32).astype(o_ref.dtype)


def _proj_call(x, w, out_dtype, name):
    m, k = x.shape
    n = w.shape[1]
    tm = _pick(m, (1024, 768, 512, 384, 256, 128))
    tn = _pick(n, (1536, 1024, 768, 512, 256, 128))
    return pl.pallas_call(
        _proj_kernel,
        grid=(m // tm, n // tn),
        in_specs=[pl.BlockSpec((tm, k), lambda i, j: (i, 0)),
                  pl.BlockSpec((k, tn), lambda i, j: (0, j))],
        out_specs=pl.BlockSpec((tm, tn), lambda i, j: (i, j)),
        out_shape=jax.ShapeDtypeStruct((m, n), out_dtype),
        compiler_params=_params(("parallel", "parallel")),
        name=name,
    )(x, w)


def _bwd_chunk(i, ncc, nc):
    return jnp.where(i < ncc, ncc - 1 - i, ncc + nc - 1 - i)


def _chunk_rows(c):
    return pl.ds(pl.multiple_of(c * CHUNK, CHUNK), CHUNK)


def _gla_kernel(q_ref, k_ref, v_ref, sm_ref, w2_ref, gb_ref, o_ref, la_ref, st_ref, *, ncc, nc):
    sm = sm_ref[0]
    for d in range(2):
        logit = _mm3(sm, w2_ref[d]) + gb_ref[d]
        la_ref[d] = _log_sigmoid(logit) * (1.0 / GLA_NORMALIZER)
    st_ref[...] = jnp.zeros_like(st_ref)
    o_ref[...] = jnp.zeros_like(o_ref)
    masks = [_tri_masks(CHUNK, rev) for rev in (False, True)]
    cum_mats = [m[2].astype(F32) for m in masks]
    probs = [(d, hh) for d in range(2) for hh in range(GLA_HB)]

    kl = [slice(hh * GLA_DK, (hh + 1) * GLA_DK) for _, hh in probs]
    vl = [slice(hh * GLA_DV, (hh + 1) * GLA_DV) for _, hh in probs]
    unroll = 2 if nc % 2 == 0 else 1

    def body(i, carry):
        steps = range(unroll)
        rows = [[_chunk_rows(i * unroll + u), _chunk_rows(_bwd_chunk(i * unroll + u, ncc, nc))] for u in steps]
        bcum_d = [[_cum_rows(cum_mats[d], la_ref[d, rows[u][d], :]) for d in range(2)] for u in steps]
        bcum = [[bcum_d[u][d][:, ln] for (d, _), ln in zip(probs, kl)] for u in steps]
        b_last = [[bc[CHUNK - 1:CHUNK] if d == 0 else bc[0:1] for (d, _), bc in zip(probs, bcum[u])] for u in steps]
        q = [[q_ref[0, rows[u][d], ln].astype(F32) * GLA_DK ** -0.5 for (d, _), ln in zip(probs, kl)] for u in steps]
        k = [[k_ref[0, rows[u][d], ln].astype(F32) for (d, _), ln in zip(probs, kl)] for u in steps]
        v = [[v_ref[0, rows[u][d], ln].astype(F32) for (d, _), ln in zip(probs, vl)] for u in steps]
        q_dec = [[a * jnp.exp(bc) for a, bc in zip(q[u], bcum[u])] for u in steps]
        k_inv = [[a * jnp.exp(-bc) for a, bc in zip(k[u], bcum[u])] for u in steps]
        k_end = [[a * jnp.exp(bl - bc) for a, bl, bc in zip(k[u], b_last[u], bcum[u])] for u in steps]
        st = [st_ref[j] for j in range(len(probs))]
        qk = [_each(_mm_nt, q_dec[u], k_inv[u]) for u in steps]
        o_inter = [_each(_mm_nt, q_dec[0], st)]
        upd = [_each(_mm_tn, v[u], k_end[u]) for u in steps]
        att = [[jnp.where(masks[d][2], a, 0.0) for (d, _), a in zip(probs, qk[u])] for u in steps]
        o_intra = [_each(_mm, att[u], v[u]) for u in steps]
        for u in steps:
            st = [sx * jnp.exp(bl) + up for sx, bl, up in zip(st, b_last[u], upd[u])]
            if u + 1 < unroll:
                o_inter.append(_each(_mm_nt, q_dec[u + 1], st))
        for j, sx in enumerate(st):
            st_ref[j] = sx
        for u in steps:
            for (d, _), ln, a, b in zip(probs, vl, o_intra[u], o_inter[u]):
                o_ref[0, rows[u][d], ln] += a + b
        return carry

    lax.fori_loop(0, nc // unroll, body, 0)


def _gla_call(p_gla, p_mla, w2e, gate_b, lc):
    b, l, _ = p_gla.shape
    nc, ncc = l // CHUNK, lc // CHUNK
    nhb = GLA_HEADS // GLA_HB
    wk, wv = GLA_HB * GLA_DK, GLA_HB * GLA_DV
    return pl.pallas_call(
        functools.partial(_gla_kernel, ncc=ncc, nc=nc),
        grid=(b, nhb),
        in_specs=[pl.BlockSpec((1, l, wk), lambda i, h: (i, 0, h)),
                  pl.BlockSpec((1, l, wk), lambda i, h: (i, 0, nhb + h)),
                  pl.BlockSpec((1, l, wv), lambda i, h: (i, 0, nhb + h)),
                  pl.BlockSpec((1, l, 128), lambda i, h: (i, 0, SMALL_BLOCK)),
                  pl.BlockSpec((2, 128, wk), lambda i, h: (0, 0, h)),
                  pl.BlockSpec((2, 1, wk), lambda i, h: (0, 0, h))],
        out_specs=pl.BlockSpec((1, l, wv), lambda i, h: (i, 0, h)),
        out_shape=jax.ShapeDtypeStruct((b, l, GLA_HEADS * GLA_DV), F32),
        scratch_shapes=[pltpu.VMEM((2, l, wk), F32), pltpu.VMEM((2 * GLA_HB, GLA_DV, GLA_DK), F32)],
        compiler_params=_params(("parallel", "parallel")),
        name="gla_scan",
    )(p_gla, p_gla, p_gla, p_mla, w2e, gate_b)


def _gdn_prep_kernel(x_ref, w_ref, o_ref, *, lc, l):
    half = GDN_CONV // 2
    nblk_head = GDN_HEADS * GDN_DK // 128
    grp, halo = 16, 8
    zeros = jnp.zeros((halo, 128), F32)
    slices = [(bb, g) for bb in range(x_ref.shape[0]) for g in range(GDN_PREP_W // 128)]
    for bb, g in slices:
        j = pl.program_id(1) * (GDN_PREP_W // 128) + g
        ln = slice(g * 128, (g + 1) * 128)
        w = w_ref[:, ln]

        def finish(acc):
            y = _silu(acc)
            inv = lax.rsqrt(jnp.sum(y * y, axis=-1, keepdims=True) + EPS)
            fac = jnp.where(j < nblk_head, inv * GDN_DK ** -0.5, jnp.where(j < 2 * nblk_head, inv, 1.0))
            return (y * fac).astype(o_ref.dtype)

        acc = None
        for s in range(-half, half + 1):
            term = x_ref[bb, grp + s:l - grp + s, ln].astype(F32) * w[s + half:s + half + 1]
            acc = term if acc is None else acc + term
        o_ref[bb, grp:l - grp, ln] = finish(acc)

        for r0 in sorted({0, lc - grp, lc, l - grp}):
            lo, hi = max(r0 - halo, 0), min(r0 + grp + halo, l)
            slab = x_ref[bb, lo:hi, ln].astype(F32)
            if r0 - halo < 0:
                slab = jnp.concatenate([zeros, slab], axis=0)
            if r0 + grp + halo > l:
                slab = jnp.concatenate([slab, zeros], axis=0)
            n = grp + 2 * halo
            t = r0 + lax.broadcasted_iota(jnp.int32, (grp, 128), 0)
            start = jnp.where(t < lc, 0, lc)
            end = jnp.where(t < lc, lc, l)
            acc = slab[halo:halo + grp] * w[half:half + 1]
            for s in range(-half, half + 1):
                if s != 0:
                    xs = pltpu.roll(slab, (-s) % n, axis=0)[halo:halo + grp]
                    ok = (t + s >= start) if s < 0 else (t + s < end)
                    acc = acc + jnp.where(ok, xs, 0.0) * w[s + half:s + half + 1]
            o_ref[bb, r0:r0 + grp, ln] = finish(acc)


def _gdn_prep_call(p_gdn, conv_w, lc):
    b, l, n = p_gdn.shape
    wd = GDN_PREP_W
    nbat = 2 if b % 2 == 0 else 1
    return pl.pallas_call(
        functools.partial(_gdn_prep_kernel, lc=lc, l=l),
        grid=(b // nbat, n // wd),
        in_specs=[pl.BlockSpec((nbat, l, wd), lambda i, j: (i, 0, j)),
                  pl.BlockSpec((GDN_CONV, wd), lambda i, j: (0, j))],
        out_specs=pl.BlockSpec((nbat, l, wd), lambda i, j: (i, 0, j)),
        out_shape=jax.ShapeDtypeStruct((b, l, n), BF16),
        compiler_params=_params(("parallel", "parallel")),
        name="gdn_prep",
    )(p_gdn, conv_w)


def _each(fn, *lists):
    return [fn(*args) for args in zip(*lists)]


class _TriConsts:
    def __init__(self):
        r = lax.broadcasted_iota(jnp.int32, (CHUNK, 2 * CHUNK), 0)
        lane = lax.broadcasted_iota(jnp.int32, (CHUNK, 2 * CHUNK), 1)
        c = lane % CHUNK
        same = lambda s: (r // s) == (c // s)
        nb = _TRI_BLOCK
        self.left = lane < CHUNK
        self.diag_blocks = same(nb)[:, :CHUNK]
        self.eye = (r == c).astype(F32)[:, :CHUNK]
        right = jnp.logical_not(self.left)
        self.first_is_eye = self.left & (r == c)
        self.first_is_lm = right & jnp.logical_not(same(nb))
        sizes = [nb * 2 ** j for j in range(int(math.log2(CHUNK // nb)))]
        self.level = [right & same(2 * s) & jnp.logical_not(same(s)) for s in sizes]


def _unit_tri_solves(lms, rhs, tc):
    nb = _TRI_BLOCK
    y = [-jnp.where(tc.diag_blocks, lm[:, :CHUNK], 0.0) for lm in lms]
    t = [tc.eye + a for a in y]
    p = y
    for _ in range(int(math.log2(nb)) - 1):
        p = _each(_MMX, p, p)
        t = _each(lambda a, b: a + _MMX(a, b), t, p)
    first = [jnp.where(tc.first_is_eye, 1.0, jnp.where(tc.first_is_lm, lm, 0.0)) for lm in lms]
    x = _each(_mm, t, first)
    zero = jnp.zeros((CHUNK, 2 * CHUNK), F32)
    for mask in tc.level:
        m = [jnp.where(mask, a, 0.0) for a in x]
        x = _each(lambda a, mj: a - _mm(mj, jnp.concatenate([zero, a], axis=0)), x, m)
    t_full = [jnp.where(tc.left, a, 0.0) for a in x]
    zero_rhs = jnp.zeros_like(rhs[0])
    return _each(lambda a, b: _mm(a, jnp.concatenate([b, zero_rhs], axis=0)), t_full, rhs)


def _gdn_kernel(rate_ref, ratel_ref, q_ref, k_ref, v_ref, abc_ref, abr_ref, o_ref, s_ref, *, ncc, nc):
    na = 2 * GDN_HB
    s_ref[...] = jnp.zeros_like(s_ref)
    o_ref[...] = jnp.zeros_like(o_ref)
    masks = [_tri_masks(CHUNK, rev) for rev in (False, True)]
    cum_mats = [m[2].astype(F32) for m in masks]
    r2 = lax.broadcasted_iota(jnp.int32, (CHUNK, 2 * CHUNK), 0)
    c2 = lax.broadcasted_iota(jnp.int32, (CHUNK, 2 * CHUNK), 1) % CHUNK
    incl2 = [c2 <= r2, c2 >= r2]
    strict2 = [c2 < r2, c2 > r2]
    cum_twice = [m.astype(F32) for m in incl2]
    tri = _TriConsts()
    unroll = GDN_UNROLL if nc % GDN_UNROLL == 0 else 1
    streams = [(u, d) for u in range(unroll) for d in range(2)]
    probs = [(si, hh) for si in range(len(streams)) for hh in range(GDN_HB)]
    pdir = [streams[si][1] for si, _ in probs]
    neg_rate_rows = -jnp.exp(rate_ref[0, 0])[:, :CHUNK]
    dtb_rows = rate_ref[0, 1][:, :CHUNK]
    neg_rate_lanes = -jnp.exp(ratel_ref[0, 0, 0:1, :])
    dtb_lanes = ratel_ref[0, 1, 0:1, :]

    def body(i, carry):
        dirs = range(len(streams))
        sdir = [d for _, d in streams]
        chunk = [i * unroll + u if d == 0 else _bwd_chunk(i * unroll + u, ncc, nc) for u, d in streams]
        rows = [_chunk_rows(c) for c in chunk]
        abc = [abc_ref[0, 0, rw, :] for rw in rows]
        abr = [abr_ref[0, 0, c] for c in chunk]
        incl = [incl2[d] for d in pdir]
        strict = [strict2[d] for d in pdir]
        g_cols = [neg_rate_lanes * _softplus(abc[si] + dtb_lanes) for si in dirs]
        g_rows = [neg_rate_rows * _softplus(abr[si] + dtb_rows) for si in dirs]
        gc_cols = [_cum_rows(cum_mats[sdir[si]], g_cols[si]) for si in dirs]
        gc_rows = [_cum_cols(g_rows[si], cum_twice[1 - sdir[si]]) for si in dirs]
        beta_cols = [jax.nn.sigmoid(abc[si]) for si in dirs]
        col = [d * GDN_HB + hh for d, (_, hh) in zip(pdir, probs)]
        gc_col = [jnp.broadcast_to(gc_cols[si][:, j:j + 1], (CHUNK, 128)) for (si, _), j in zip(probs, col)]
        gc_row = [gc_rows[si][j:j + 1, :] for (si, _), j in zip(probs, col)]
        beta_col = [beta_cols[si][:, na + j:na + j + 1] for (si, _), j in zip(probs, col)]
        g_last = [gc[CHUNK - 1:CHUNK] if d == 0 else gc[0:1] for d, gc in zip(pdir, gc_col)]
        decay = [jnp.where(m, jnp.exp(jnp.where(m, a - b, 0.0)), 0.0)
                 for m, a, b in zip(incl, gc_col, gc_row)]

        lanes = [slice(hh * GDN_DK, (hh + 1) * GDN_DK) for _, hh in probs]
        q = [q_ref[0, rows[si], ln].astype(F32) for (si, _), ln in zip(probs, lanes)]
        k = [k_ref[0, rows[si], ln].astype(F32) for (si, _), ln in zip(probs, lanes)]
        v = [v_ref[0, rows[si], ln].astype(F32) for (si, _), ln in zip(probs, lanes)]
        k_beta = _each(jnp.multiply, k, beta_col)
        v_beta = _each(jnp.multiply, v, beta_col)
        e_col = _each(jnp.exp, gc_col)
        kq = _each(_mm_nt, [jnp.concatenate([a, b], axis=0) for a, b in zip(k_beta, q)],
                   [jnp.concatenate([a, a], axis=0) for a in k])
        kk = [x[:CHUNK] for x in kq]
        qk = [x[CHUNK:] for x in kq]
        lm = [jnp.where(m, a * dc, 0.0) for m, a, dc in zip(strict, kk, decay)]
        rhs = [jnp.concatenate([vb, kb * e], axis=1) for vb, kb, e in zip(v_beta, k_beta, e_col)]
        uw = _unit_tri_solves(lm, rhs, tri)
        u = [x[:, :GDN_DV] for x in uw]
        w = [x[:, GDN_DV:] for x in uw]
        att = [jnp.where(m, a * dc, 0.0)[:, :CHUNK] for m, a, dc in zip(incl, qk, decay)]
        q_dec = _each(jnp.multiply, q, e_col)
        k_end = [kx * jnp.exp(gl - gc) for kx, gl, gc in zip(k, g_last, gc_col)]
        nstate = 2 * GDN_HB
        s = [s_ref[j] for j in range(nstate)]
        for pos in range(unroll):
            sel = [p for p, (si, _) in enumerate(probs) if streams[si][0] == pos]
            pick = lambda xs: [xs[p] for p in sel]
            wq = _each(_mm, [jnp.concatenate([a, b], axis=0) for a, b in zip(pick(w), pick(q_dec))], s)
            ws = [x[:CHUNK] for x in wq]
            o_inter = [x[CHUNK:] for x in wq]
            v_new = _each(jnp.subtract, pick(u), ws)
            upd = _each(_mm_tn, pick(k_end), v_new)
            o_intra = _each(_mm, pick(att), v_new)
            s = [sx * jnp.exp(gl) + up for sx, gl, up in zip(s, pick(g_last), upd)]
            for p, a, b in zip(sel, o_intra, o_inter):
                o_ref[0, rows[probs[p][0]], lanes[p]] += a + b
        for j in range(nstate):
            s_ref[j] = s[j]
        return carry

    lax.fori_loop(0, nc // unroll, body, 0)


def _gdn_call(qkvn, abc, abr, a_log, dt_bias, lc):
    b, l, _ = qkvn.shape
    nc, ncc = l // CHUNK, lc // CHUNK
    nhb = GDN_HEADS // GDN_HB
    wb = GDN_HB * GDN_DK
    na = 2 * GDN_HB
    rate = jnp.stack([a_log, dt_bias]).reshape(2, 2, nhb, GDN_HB).transpose(2, 0, 1, 3).reshape(nhb, 2, na)
    rate = jnp.pad(rate, ((0, 0), (0, 0), (0, 128 - na)))
    nr = _GDN_GATE_ROWS
    rate_rows = jnp.broadcast_to(rate[:, :, :nr, None], (nhb, 2, nr, 128))
    rate_lanes = jnp.broadcast_to(rate[:, :, None, :], (nhb, 2, 8, 128))
    return pl.pallas_call(
        functools.partial(_gdn_kernel, ncc=ncc, nc=nc),
        grid=(b, nhb),
        in_specs=[pl.BlockSpec((1, 2, nr, 128), lambda i, h: (h, 0, 0, 0)),
                  pl.BlockSpec((1, 2, 8, 128), lambda i, h: (h, 0, 0, 0)),
                  pl.BlockSpec((1, l, wb), lambda i, h: (i, 0, h)),
                  pl.BlockSpec((1, l, wb), lambda i, h: (i, 0, nhb + h)),
                  pl.BlockSpec((1, l, wb), lambda i, h: (i, 0, 2 * nhb + h)),
                  pl.BlockSpec((1, 1, l, 128), lambda i, h: (i, h, 0, 0)),
                  pl.BlockSpec((1, 1, nc, nr, CHUNK), lambda i, h: (i, h, 0, 0, 0))],
        out_specs=pl.BlockSpec((1, l, wb), lambda i, h: (i, 0, h)),
        out_shape=jax.ShapeDtypeStruct((b, l, GDN_HEADS * GDN_DV), F32),
        scratch_shapes=[pltpu.VMEM((2 * GDN_HB, GDN_DK, GDN_DV), F32)],
        compiler_params=_params(("parallel", "parallel")),
        name="gdn_scan",
    )(rate_rows, rate_lanes, qkvn, qkvn, qkvn, abc, abr)


def _gdn_gate_layouts(p_mla, lc):
    b, l, _ = p_mla.shape
    nhb = GDN_HEADS // GDN_HB
    na = 2 * GDN_HB
    o = MLA_Q_LORA + MLA_KV_LORA
    ab = p_mla[:, :, o + SM_A:o + SM_A + 32].reshape(b, l, 2, 2, nhb, GDN_HB)
    ab = ab.transpose(0, 4, 1, 2, 3, 5).reshape(b, nhb, l, 2 * na)
    abr = ab[..., :na].reshape(b, nhb, l // CHUNK, CHUNK, na).transpose(0, 1, 2, 4, 3)
    abr = jnp.pad(abr, ((0, 0), (0, 0), (0, 0), (0, _GDN_GATE_ROWS - na), (0, 0)))
    abc = jnp.pad(ab, ((0, 0), (0, 0), (0, 0), (0, 128 - 2 * na)))
    return abc, abr


def _rope_swap(x):
    lane = lax.broadcasted_iota(jnp.int32, x.shape, 1)
    n = x.shape[1]
    return jnp.where(lane % 32 < 16, pltpu.roll(x, n - 16, axis=1), pltpu.roll(x, 16, axis=1))


def _mla_prep_kernel(p_ref, cos_ref, sin_ref, qnw_ref, qup_ref, kvnw_ref, kvup_ref,
                     wqn_ref, wqr_ref, wkn_ref, wkr_ref, q_ref, k_ref, v_ref):
    p = p_ref[0]
    cos_t = cos_ref[...]
    sin_t = sin_ref[...]

    def rms(x, n):
        return x * lax.rsqrt(jnp.sum(x * x, axis=-1, keepdims=True) * (1.0 / n) + EPS)

    def rope(x):
        return x * cos_t + _rope_swap(x) * sin_t

    qd = p[:, :MLA_Q_LORA]
    kvd = p[:, MLA_Q_LORA:MLA_Q_LORA + MLA_KV_LORA]
    sm = p[:, MLA_Q_LORA + MLA_KV_LORA:]
    q = _mm(rms(qd, MLA_Q_LORA) * qnw_ref[...], qup_ref[...])
    kv = _mm(rms(kvd, MLA_KV_LORA) * kvnw_ref[...], kvup_ref[...])
    lane = lax.broadcasted_iota(jnp.int32, sm.shape, 1)
    kr = jnp.where(lane < MLA_ROPE, sm, 0.0)
    kr = rope(rms(kr, MLA_ROPE) * wkr_ref[...])
    for h in range(MLA_HEADS):
        base = h * 256
        qn = rms(q[:, base:base + 128], MLA_NOPE) * wqn_ref[...]
        q_ref[0, :, base:base + 128] = (qn * MLA_Q_PRESCALE).astype(q_ref.dtype)
        qr = rms(q[:, base + 128:base + 256], MLA_ROPE) * wqr_ref[...]
        q_ref[0, :, base + 128:base + 256] = (rope(qr) * MLA_Q_PRESCALE).astype(q_ref.dtype)
        kn = kv[:, h * 128:(h + 1) * 128]
        k_ref[0, :, base:base + 128] = (rms(kn, MLA_NOPE) * wkn_ref[...]).astype(k_ref.dtype)
        k_ref[0, :, base + 128:base + 256] = kr.astype(k_ref.dtype)
    v_ref[0] = kv[:, MLA_HEADS * MLA_NOPE:].T.astype(v_ref.dtype)


def _mla_prep_call(p_mla, cos_t, sin_t, qnw, qup, kvnw, kvup, wqn, wqr, wkn, wkr):
    b, l, gw = p_mla.shape
    tm = _pick(l, (256, 128))
    hq = MLA_HEADS * 256
    full = lambda shape: pl.BlockSpec(shape, lambda i, t: tuple(0 for _ in shape))
    return pl.pallas_call(
        _mla_prep_kernel,
        grid=(b, l // tm),
        in_specs=[pl.BlockSpec((1, tm, gw), lambda i, t: (i, t, 0)),
                  pl.BlockSpec((tm, 128), lambda i, t: (t, 0)),
                  pl.BlockSpec((tm, 128), lambda i, t: (t, 0)),
                  full((1, MLA_Q_LORA)), full((MLA_Q_LORA, hq)),
                  full((1, MLA_KV_LORA)), full((MLA_KV_LORA, hq)),
                  full((1, 128)), full((1, 128)), full((1, 128)), full((1, 128))],
        out_specs=[pl.BlockSpec((1, tm, hq), lambda i, t: (i, t, 0)),
                   pl.BlockSpec((1, tm, hq), lambda i, t: (i, t, 0)),
                   pl.BlockSpec((1, MLA_HEADS * MLA_DV, tm), lambda i, t: (i, 0, t))],
        out_shape=[jax.ShapeDtypeStruct((b, l, hq), BF16),
                   jax.ShapeDtypeStruct((b, l, hq), BF16),
                   jax.ShapeDtypeStruct((b, MLA_HEADS * MLA_DV, l), BF16)],
        compiler_params=_params(("parallel", "parallel")),
        name="mla_prep",
    )(p_mla, cos_t, sin_t, qnw, qup, kvnw, kvup, wqn, wqr, wkn, wkr)


def _mla_attn_kernel(q_ref, k_ref, vt_ref, o_ref, *, lc, tq, q_off):
    qi = pl.program_id(2) + q_off
    heads = range(MLA_HP)

    def attend(nk):
        def scores(h):
            return lax.dot_general(k_ref[0, :nk, h * 256:(h + 1) * 256], q_ref[0, :, h * 256:(h + 1) * 256],
                                   (((1,), (1,)), ((), ())), preferred_element_type=F32)

        ahead = MLA_HP
        st = {h: scores(h) for h in range(min(ahead, MLA_HP))}
        for h in heads:
            x = st.pop(h)
            e = jnp.exp2(x - jnp.max(x, axis=0, keepdims=True))
            den = jnp.sum(e, axis=0, keepdims=True)
            if h + ahead < MLA_HP:
                st[h + ahead] = scores(h + ahead)
            ot = jnp.dot(vt_ref[0, h * MLA_DV:(h + 1) * MLA_DV, :nk], e.astype(BF16),
                         preferred_element_type=F32)
            o_ref[0, :, h * MLA_DV:(h + 1) * MLA_DV] = (ot / den).T.astype(o_ref.dtype)

    if q_off == 0:
        @pl.when(qi * tq < lc)
        def _():
            attend(lc)

        @pl.when(qi * tq >= lc)
        def _():
            attend(k_ref.shape[1])
    else:
        attend(k_ref.shape[1])


def _mla_attn_call(q, k, vt, lc, with_ctx):
    b, l, _ = q.shape
    tq = _pick(lc, (256, 128))
    q_off = 0 if with_ctx else lc // tq
    nq = l // tq - q_off
    return pl.pallas_call(
        functools.partial(_mla_attn_kernel, lc=lc, tq=tq, q_off=q_off),
        grid=(b, MLA_HEADS // MLA_HP, nq),
        in_specs=[pl.BlockSpec((1, tq, MLA_HP * 256), lambda i, h, t: (i, t + q_off, h)),
                  pl.BlockSpec((1, l, MLA_HP * 256), lambda i, h, t: (i, 0, h)),
                  pl.BlockSpec((1, MLA_HP * MLA_DV, l), lambda i, h, t: (i, h, 0))],
        out_specs=pl.BlockSpec((1, tq, MLA_HP * MLA_DV), lambda i, h, t: (i, t, h)),
        out_shape=jax.ShapeDtypeStruct((b, nq * tq, MLA_HEADS * MLA_DV), F32),
        compiler_params=_params(("parallel", "parallel", "arbitrary")),
        name="mla_attn",
    )(q, k, vt)


def _merge_kernel(h_ref, oa_ref, ob_ref, oc_ref, za_ref, zb_ref, zc_ref, g0_ref, g1_ref, g2_ref,
                  modx_ref, modc_ref, wa_ref, wb_ref, wbr_ref, wout_ref, o_ref, *, lc, tm, t_off):
    t = pl.program_id(1) + t_off

    def head_norm(o, width):
        parts = []
        for h in range(BRANCH_W // width):
            x = o[:, h * width:(h + 1) * width]
            parts.append(x * lax.rsqrt(jnp.mean(x * x, axis=-1, keepdims=True) + EPS))
        return jnp.concatenate(parts, axis=-1)

    ya = head_norm(oa_ref[0], GLA_DV) * wa_ref[...] * _silu(za_ref[0].astype(F32))
    yb = head_norm(ob_ref[0], GDN_DV) * wb_ref[...] * _silu(zb_ref[0].astype(F32))
    yc = oc_ref[0] * _silu(zc_ref[0].astype(F32))
    acc = _sigmoid(g0_ref[0].astype(F32)) * _mm(ya, wbr_ref[0])
    acc = acc + _sigmoid(g1_ref[0].astype(F32)) * _mm(yb, wbr_ref[1])
    acc = acc + _sigmoid(g2_ref[0].astype(F32)) * _mm(yc, wbr_ref[2])
    d = D_MODEL
    gate = _row_select(t, tm, lc, modc_ref[:, 2 * d:], modx_ref[0][:, 2 * d:])
    o_ref[0] = h_ref[0] + gate * _mm(acc, wout_ref[...])


def _merge_call(h, oa, ob, oc, p_zg, modx, modc, wa, wb, w_branch, w_out, lc, with_ctx):
    b, l, d = h.shape
    tm = _pick(lc, (256, 128))
    t_off = 0 if with_ctx else lc // tm
    nt = l // tm - t_off
    oc_off = 0 if oc.shape[1] == l else lc // tm
    row = lambda i, t: (i, t + t_off, 0)
    zg = lambda n: pl.BlockSpec((1, tm, d), lambda i, t: (i, t + t_off, n))
    return pl.pallas_call(
        functools.partial(_merge_kernel, lc=lc, tm=tm, t_off=t_off),
        grid=(b, nt),
        in_specs=[pl.BlockSpec((1, tm, d), row), pl.BlockSpec((1, tm, d), row), pl.BlockSpec((1, tm, d), row),
                  pl.BlockSpec((1, tm, d), lambda i, t: (i, t + t_off - oc_off, 0)),
                  zg(0), zg(1), zg(2), zg(3), zg(4), zg(5),
                  pl.BlockSpec((1, 1, 3 * d), lambda i, t: (i, 0, 0)),
                  pl.BlockSpec((1, 3 * d), lambda i, t: (0, 0)),
                  pl.BlockSpec((1, d), lambda i, t: (0, 0)),
                  pl.BlockSpec((1, d), lambda i, t: (0, 0)),
                  pl.BlockSpec((3, d, d), lambda i, t: (0, 0, 0)),
                  pl.BlockSpec((d, d), lambda i, t: (0, 0))],
        out_specs=pl.BlockSpec((1, tm, d), lambda i, t: (i, t, 0)),
        out_shape=jax.ShapeDtypeStruct((b, nt * tm, d), F32),
        compiler_params=_params(("parallel", "parallel")),
        name="merge",
    )(h, oa, ob, oc, p_zg, p_zg, p_zg, p_zg, p_zg, p_zg, modx, modc, wa, wb, w_branch, w_out)


def _cols(w, *names):
    return jnp.concatenate([w[:, _IN_OFF[n]:_IN_OFF[n] + _IN_LEN[n]] for n in names], axis=1)


def _rope_tables(t_lat, lc):
    rows = t_lat // GRID_W
    row = jnp.repeat(jnp.arange(rows, dtype=F32), GRID_W)
    col = jnp.tile(jnp.arange(GRID_W, dtype=F32), rows)
    inv_freq = jnp.power(ROPE_BASE, -jnp.arange(ROPE_FREQ, dtype=F32) / ROPE_FREQ)
    ar = row[:, None] * inv_freq
    ac = col[:, None] * inv_freq
    zeros = jnp.zeros((t_lat, 128 - MLA_ROPE), F32)
    cos_t = jnp.concatenate([jnp.cos(ar), jnp.cos(ar), jnp.cos(ac), jnp.cos(ac), zeros], axis=1)
    sin_t = jnp.concatenate([-jnp.sin(ar), jnp.sin(ar), -jnp.sin(ac), jnp.sin(ac), zeros], axis=1)
    cos_c = jnp.concatenate([jnp.ones((lc, MLA_ROPE), F32), jnp.zeros((lc, 128 - MLA_ROPE), F32)], axis=1)
    return (jnp.concatenate([cos_c, cos_t], axis=0),
            jnp.concatenate([jnp.zeros((lc, 128), F32), sin_t], axis=0))


def _pad_lanes(w, n=128):
    return jnp.pad(w, (0, n - w.shape[0])).reshape(1, n)


def kernel(x, c, ctx, c_ctx, norm_w, ada_w, ada_b, w_in, gla_gate_w2, gla_gate_b, gla_norm_w, gdn_conv_w,
           gdn_a_log, gdn_dt_bias, gdn_norm_w, mla_q_norm_w, mla_q_up, mla_kv_norm_w, mla_kv_up,
           mla_qn_nope, mla_qn_rope, mla_kn_nope, mla_kn_rope, w_branch, w_out):
    b, t_lat, d = x.shape
    lc = ctx.shape[1]
    l = lc + t_lat
    depth = w_in.shape[0]
    assert d == D_MODEL and lc % 128 == 0 and t_lat % 128 == 0

    h = jnp.concatenate([ctx, x], axis=1)
    rows = ((b + 1 + 7) // 8) * 8
    c_all = jnp.concatenate([c, c_ctx[None, :], jnp.zeros((rows - b - 1, d), F32)], axis=0)
    mod = _ada_call(c_all, ada_w, ada_b)
    cos_t, sin_t = _rope_tables(t_lat, lc)

    for li in range(depth):
        last = li == depth - 1
        modx = mod[li, :b].reshape(b, 1, 3 * d)
        modc = mod[li, b:b + 1]
        w = w_in[li]
        w_mla = _cols(w, 'mla_q_down', 'mla_kv_down', 'mla_k_rope', 'gla_gate_lr', 'gdn_a', 'gdn_b').astype(BF16)
        w_gla = _cols(w, 'gla_q', 'gla_k', 'gla_v').astype(BF16)
        w_gdn = _cols(w, 'gdn_qkv').astype(BF16)
        w_zg = _cols(w, 'gla_z', 'gdn_z', 'mla_z', 'merge_gate').astype(BF16)

        hn = _prenorm_call(h, modx, modc, norm_w[li], lc).reshape(b * l, d)
        p_mla = _proj_call(hn, w_mla, F32, "proj_mla").reshape(b, l, -1)
        p_gla = _proj_call(hn, w_gla, BF16, "proj_gla").reshape(b, l, -1)
        p_gdn = _proj_call(hn, w_gdn, F32, "proj_gdn").reshape(b, l, -1)
        p_zg = _proj_call(hn, w_zg, BF16, "proj_zg").reshape(b, l, -1)

        w2e = jnp.stack([jnp.pad(gla_gate_w2[li, dd], ((SM_LR + dd * GLA_RANK, 128 - SM_LR - (dd + 1) * GLA_RANK),
                                                      (0, 0))) for dd in range(2)])
        oa = _gla_call(p_gla, p_mla, w2e, gla_gate_b[li].reshape(2, 1, -1), lc)

        qkvn = _gdn_prep_call(p_gdn, gdn_conv_w[li], lc)
        abc, abr = _gdn_gate_layouts(p_mla, lc)
        ob = _gdn_call(qkvn, abc, abr, gdn_a_log[li], gdn_dt_bias[li], lc)

        qup = mla_q_up[li].reshape(MLA_Q_LORA, MLA_HEADS, MLA_NOPE + MLA_ROPE)
        qup = jnp.pad(qup, ((0, 0), (0, 0), (0, 256 - MLA_NOPE - MLA_ROPE))).reshape(MLA_Q_LORA, -1).astype(BF16)
        kvup = mla_kv_up[li].reshape(MLA_KV_LORA, MLA_HEADS, MLA_NOPE + MLA_DV)
        kvup = jnp.concatenate([kvup[:, :, :MLA_NOPE].reshape(MLA_KV_LORA, -1),
                                kvup[:, :, MLA_NOPE:].reshape(MLA_KV_LORA, -1)], axis=1).astype(BF16)
        qm, km, vm = _mla_prep_call(
            p_mla, cos_t, sin_t, mla_q_norm_w[li].reshape(1, -1), qup, mla_kv_norm_w[li].reshape(1, -1), kvup,
            mla_qn_nope[li].reshape(1, -1), _pad_lanes(mla_qn_rope[li]),
            mla_kn_nope[li].reshape(1, -1), _pad_lanes(mla_kn_rope[li]))
        oc = _mla_attn_call(qm, km, vm, lc, with_ctx=not last)

        h = _merge_call(h, oa, ob, oc, p_zg, modx, modc,
                        jnp.tile(gla_norm_w[li], GLA_HEADS).reshape(1, -1),
                        jnp.tile(gdn_norm_w[li], GDN_HEADS).reshape(1, -1),
                        w_branch[li].astype(BF16), w_out[li].astype(BF16), lc, with_ctx=not last)
    return h
```

```python
import functools
import math

import numpy as np
import jax
import jax.numpy as jnp
from jax import lax
from jax.experimental import pallas as pl
from jax.experimental.pallas import tpu as pltpu

F32 = jnp.float32
BF16 = jnp.bfloat16

D_MODEL = 1024
EPS = 1e-6
GRID_W = 64
CHUNK = 64

GLA_HEADS, GLA_DK, GLA_DV, GLA_RANK, GLA_NORMALIZER = 4, 128, 256, 16, 16.0
GDN_HEADS, GDN_DK, GDN_DV, GDN_CONV = 8, 128, 128, 5
GLA_HB = 4
GDN_HB = 8
_GDN_GATE_ROWS = max(8, 2 * GDN_HB)
GDN_PREP_W = 128
GDN_UNROLL = 1
MLA_HEADS, MLA_Q_LORA, MLA_KV_LORA, MLA_NOPE, MLA_ROPE, MLA_DV = 8, 384, 256, 128, 64, 128
MLA_SCALE = (MLA_NOPE + MLA_ROPE) ** -0.5
MLA_Q_PRESCALE = MLA_SCALE * math.log2(math.e)
MLA_HP = 8
ROPE_FREQ = MLA_ROPE // 4
ROPE_BASE = 10000.0
BRANCH_W = 1024

_IN_SIZES = (512, 512, 1024, 32, 1024, 3072, 16, 16, 1024, 384, 256, 64, 1024, 3072)
_IN_NAMES = ('gla_q', 'gla_k', 'gla_v', 'gla_gate_lr', 'gla_z', 'gdn_qkv', 'gdn_a', 'gdn_b', 'gdn_z',
             'mla_q_down', 'mla_kv_down', 'mla_k_rope', 'mla_z', 'merge_gate')
_IN_OFF = dict(zip(_IN_NAMES, np.concatenate([[0], np.cumsum(_IN_SIZES)[:-1]]).tolist()))
_IN_LEN = dict(zip(_IN_NAMES, _IN_SIZES))

SM_ROPE, SM_LR, SM_A, SM_B = 0, 64, 96, 112
MLA_GROUP_W = MLA_Q_LORA + MLA_KV_LORA + 128
SMALL_BLOCK = (MLA_Q_LORA + MLA_KV_LORA) // 128

VMEM_LIMIT = 58 * 1024 * 1024


def _params(sem):
    return pltpu.CompilerParams(dimension_semantics=sem, vmem_limit_bytes=VMEM_LIMIT)


def _pick(n, cands):
    for c in cands:
        if n % c == 0:
            return c
    raise ValueError(f"no tile for {n} in {cands}")


def _mm(a, b):
    return jnp.dot(a.astype(BF16), b.astype(BF16), preferred_element_type=F32)


def _mm_nt(a, b):
    return lax.dot_general(a.astype(BF16), b.astype(BF16), (((1,), (1,)), ((), ())),
                           preferred_element_type=F32)


def _mm_tn(a, b):
    return lax.dot_general(a.astype(BF16), b.astype(BF16), (((0,), (0,)), ((), ())),
                           preferred_element_type=F32)


def _mm_hp(a, b):
    return jnp.dot(a, b, precision=lax.Precision.HIGHEST, preferred_element_type=F32)


def _split(a):
    hi = a.astype(BF16)
    lo = (a - hi.astype(F32)).astype(BF16)
    return hi, lo


def _mm3(a, b):
    ah, al = _split(a)
    bh, bl = _split(b)
    d = functools.partial(jnp.dot, preferred_element_type=F32)
    return d(ah, bh) + (d(ah, bl) + d(al, bh))


_MMX = _mm
_TRI_BLOCK = 2


def _cum_rows(cum, x):
    cb = cum.astype(BF16)
    d = functools.partial(jnp.dot, preferred_element_type=F32)
    hi, lo = _split(x)
    return d(cb, hi) + d(cb, lo)


def _cum_cols(x, cum):
    cb = cum.astype(BF16)
    d = functools.partial(jnp.dot, preferred_element_type=F32)
    hi, lo = _split(x)
    return d(hi, cb) + d(lo, cb)


def _softplus(x):
    return jnp.maximum(x, 0.0) + jnp.log(1.0 + jnp.exp(-jnp.abs(x)))


def _log_sigmoid(x):
    return jnp.minimum(x, 0.0) - jnp.log(1.0 + jnp.exp(-jnp.abs(x)))


_sigmoid = jax.nn.sigmoid


def _silu(x):
    return x * _sigmoid(x)


def _tri_masks(n, reverse):
    r = lax.broadcasted_iota(jnp.int32, (n, n), 0)
    c = lax.broadcasted_iota(jnp.int32, (n, n), 1)
    incl = (c >= r) if reverse else (c <= r)
    strict = (c > r) if reverse else (c < r)
    return r, c, incl, strict


def _ada_kernel(c_ref, w_ref, b_ref, o_ref):
    o_ref[0] = _mm_hp(_silu(c_ref[...]), w_ref[0]) + b_ref[0]


def _ada_call(c_all, ada_w, ada_b):
    nl, d, n3 = ada_w.shape
    r = c_all.shape[0]
    tn = 1024
    return pl.pallas_call(
        _ada_kernel,
        grid=(nl, n3 // tn),
        in_specs=[pl.BlockSpec((r, d), lambda l, j: (0, 0)),
                  pl.BlockSpec((1, d, tn), lambda l, j: (l, 0, j)),
                  pl.BlockSpec((1, 1, tn), lambda l, j: (l, 0, j))],
        out_specs=pl.BlockSpec((1, r, tn), lambda l, j: (l, 0, j)),
        out_shape=jax.ShapeDtypeStruct((nl, r, n3), F32),
        compiler_params=_params(("arbitrary", "arbitrary")),
        name="ada_rows",
    )(c_all, ada_w, ada_b.reshape(nl, 1, n3))


def _row_select(t, tm, lc, ctx_row, lat_row):
    rows = t * tm + lax.broadcasted_iota(jnp.int32, (tm, 1), 0)
    return jnp.where(rows < lc, ctx_row, lat_row)


def _prenorm_kernel(h_ref, modx_ref, modc_ref, nw_ref, o_ref, *, lc, tm):
    t = pl.program_id(1)
    x = h_ref[0]
    y = x * lax.rsqrt(jnp.mean(x * x, axis=-1, keepdims=True) + EPS) * nw_ref[...]
    mx = modx_ref[0]
    mc = modc_ref[...]
    d = D_MODEL
    shift = _row_select(t, tm, lc, mc[:, :d], mx[:, :d])
    scale = _row_select(t, tm, lc, mc[:, d:2 * d], mx[:, d:2 * d])
    o_ref[0] = (y * (1.0 + scale) + shift).astype(o_ref.dtype)


def _prenorm_call(h, modx, modc, norm_w, lc):
    b, l, d = h.shape
    tm = _pick(l, (768, 512, 384, 256, 128))
    return pl.pallas_call(
        functools.partial(_prenorm_kernel, lc=lc, tm=tm),
        grid=(b, l // tm),
        in_specs=[pl.BlockSpec((1, tm, d), lambda i, t: (i, t, 0)),
                  pl.BlockSpec((1, 1, 3 * d), lambda i, t: (i, 0, 0)),
                  pl.BlockSpec((1, 3 * d), lambda i, t: (0, 0)),
                  pl.BlockSpec((1, d), lambda i, t: (0, 0))],
        out_specs=pl.BlockSpec((1, tm, d), lambda i, t: (i, t, 0)),
        out_shape=jax.ShapeDtypeStruct((b, l, d), BF16),
        compiler_params=_params(("parallel", "parallel")),
        name="prenorm",
    )(h, modx, modc, norm_w.reshape(1, d))


def _proj_kernel(x_ref, w_ref, o_ref):
    o_ref[...] = jnp.dot(x_ref[...], w_ref[...], preferred_element_type=F32).astype(o_ref.dtype)


def _proj_call(x, w, out_dtype, name):
    m, k = x.shape
    n = w.shape[1]
    tm = _pick(m, (1024, 768, 512, 384, 256, 128))
    tn = _pick(n, (1536, 1024, 768, 512, 256, 128))
    return pl.pallas_call(
        _proj_kernel,
        grid=(m // tm, n // tn),
        in_specs=[pl.BlockSpec((tm, k), lambda i, j: (i, 0)),
                  pl.BlockSpec((k, tn), lambda i, j: (0, j))],
        out_specs=pl.BlockSpec((tm, tn), lambda i, j: (i, j)),
        out_shape=jax.ShapeDtypeStruct((m, n), out_dtype),
        compiler_params=_params(("parallel", "parallel")),
        name=name,
    )(x, w)


def _bwd_chunk(i, ncc, nc):
    return jnp.where(i < ncc, ncc - 1 - i, ncc + nc - 1 - i)


def _chunk_rows(c):
    return pl.ds(pl.multiple_of(c * CHUNK, CHUNK), CHUNK)


def _gla_kernel(q_ref, k_ref, v_ref, sm_ref, w2_ref, gb_ref, o_ref, la_ref, st_ref, *, ncc, nc):
    sm = sm_ref[0]
    for d in range(2):
        logit = _mm3(sm, w2_ref[d]) + gb_ref[d]
        la_ref[d] = _log_sigmoid(logit) * (1.0 / GLA_NORMALIZER)
    st_ref[...] = jnp.zeros_like(st_ref)
    o_ref[...] = jnp.zeros_like(o_ref)
    masks = [_tri_masks(CHUNK, rev) for rev in (False, True)]
    cum_mats = [m[2].astype(F32) for m in masks]
    probs = [(d, hh) for d in range(2) for hh in range(GLA_HB)]

    kl = [slice(hh * GLA_DK, (hh + 1) * GLA_DK) for _, hh in probs]
    vl = [slice(hh * GLA_DV, (hh + 1) * GLA_DV) for _, hh in probs]
    unroll = 2 if nc % 2 == 0 else 1

    def body(i, carry):
        steps = range(unroll)
        rows = [[_chunk_rows(i * unroll + u), _chunk_rows(_bwd_chunk(i * unroll + u, ncc, nc))] for u in steps]
        bcum_d = [[_cum_rows(cum_mats[d], la_ref[d, rows[u][d], :]) for d in range(2)] for u in steps]
        bcum = [[bcum_d[u][d][:, ln] for (d, _), ln in zip(probs, kl)] for u in steps]
        b_last = [[bc[CHUNK - 1:CHUNK] if d == 0 else bc[0:1] for (d, _), bc in zip(probs, bcum[u])] for u in steps]
        q = [[q_ref[0, rows[u][d], ln].astype(F32) * GLA_DK ** -0.5 for (d, _), ln in zip(probs, kl)] for u in steps]
        k = [[k_ref[0, rows[u][d], ln].astype(F32) for (d, _), ln in zip(probs, kl)] for u in steps]
        v = [[v_ref[0, rows[u][d], ln].astype(F32) for (d, _), ln in zip(probs, vl)] for u in steps]
        q_dec = [[a * jnp.exp(bc) for a, bc in zip(q[u], bcum[u])] for u in steps]
        k_inv = [[a * jnp.exp(-bc) for a, bc in zip(k[u], bcum[u])] for u in steps]
        k_end = [[a * jnp.exp(bl - bc) for a, bl, bc in zip(k[u], b_last[u], bcum[u])] for u in steps]
        st = [st_ref[j] for j in range(len(probs))]
        qk = [_each(_mm_nt, q_dec[u], k_inv[u]) for u in steps]
        o_inter = [_each(_mm_nt, q_dec[0], st)]
        upd = [_each(_mm_tn, v[u], k_end[u]) for u in steps]
        att = [[jnp.where(masks[d][2], a, 0.0) for (d, _), a in zip(probs, qk[u])] for u in steps]
        o_intra = [_each(_mm, att[u], v[u]) for u in steps]
        for u in steps:
            st = [sx * jnp.exp(bl) + up for sx, bl, up in zip(st, b_last[u], upd[u])]
            if u + 1 < unroll:
                o_inter.append(_each(_mm_nt, q_dec[u + 1], st))
        for j, sx in enumerate(st):
            st_ref[j] = sx
        for u in steps:
            for (d, _), ln, a, b in zip(probs, vl, o_intra[u], o_inter[u]):
                o_ref[0, rows[u][d], ln] += a + b
        return carry

    lax.fori_loop(0, nc // unroll, body, 0)


def _gla_call(p_gla, p_mla, w2e, gate_b, lc):
    b, l, _ = p_gla.shape
    nc, ncc = l // CHUNK, lc // CHUNK
    nhb = GLA_HEADS // GLA_HB
    wk, wv = GLA_HB * GLA_DK, GLA_HB * GLA_DV
    return pl.pallas_call(
        functools.partial(_gla_kernel, ncc=ncc, nc=nc),
        grid=(b, nhb),
        in_specs=[pl.BlockSpec((1, l, wk), lambda i, h: (i, 0, h)),
                  pl.BlockSpec((1, l, wk), lambda i, h: (i, 0, nhb + h)),
                  pl.BlockSpec((1, l, wv), lambda i, h: (i, 0, nhb + h)),
                  pl.BlockSpec((1, l, 128), lambda i, h: (i, 0, SMALL_BLOCK)),
                  pl.BlockSpec((2, 128, wk), lambda i, h: (0, 0, h)),
                  pl.BlockSpec((2, 1, wk), lambda i, h: (0, 0, h))],
        out_specs=pl.BlockSpec((1, l, wv), lambda i, h: (i, 0, h)),
        out_shape=jax.ShapeDtypeStruct((b, l, GLA_HEADS * GLA_DV), F32),
        scratch_shapes=[pltpu.VMEM((2, l, wk), F32), pltpu.VMEM((2 * GLA_HB, GLA_DV, GLA_DK), F32)],
        compiler_params=_params(("parallel", "parallel")),
        name="gla_scan",
    )(p_gla, p_gla, p_gla, p_mla, w2e, gate_b)


def _gdn_prep_kernel(x_ref, w_ref, o_ref, *, lc, l):
    half = GDN_CONV // 2
    nblk_head = GDN_HEADS * GDN_DK // 128
    grp, halo = 16, 8
    zeros = jnp.zeros((halo, 128), F32)
    slices = [(bb, g) for bb in range(x_ref.shape[0]) for g in range(GDN_PREP_W // 128)]
    for bb, g in slices:
        j = pl.program_id(1) * (GDN_PREP_W // 128) + g
        ln = slice(g * 128, (g + 1) * 128)
        w = w_ref[:, ln]

        def finish(acc):
            y = _silu(acc)
            inv = lax.rsqrt(jnp.sum(y * y, axis=-1, keepdims=True) + EPS)
            fac = jnp.where(j < nblk_head, inv * GDN_DK ** -0.5, jnp.where(j < 2 * nblk_head, inv, 1.0))
            return (y * fac).astype(o_ref.dtype)

        acc = None
        for s in range(-half, half + 1):
            term = x_ref[bb, grp + s:l - grp + s, ln].astype(F32) * w[s + half:s + half + 1]
            acc = term if acc is None else acc + term
        o_ref[bb, grp:l - grp, ln] = finish(acc)

        for r0 in sorted({0, lc - grp, lc, l - grp}):
            lo, hi = max(r0 - halo, 0), min(r0 + grp + halo, l)
            slab = x_ref[bb, lo:hi, ln].astype(F32)
            if r0 - halo < 0:
                slab = jnp.concatenate([zeros, slab], axis=0)
            if r0 + grp + halo > l:
                slab = jnp.concatenate([slab, zeros], axis=0)
            n = grp + 2 * halo
            t = r0 + lax.broadcasted_iota(jnp.int32, (grp, 128), 0)
            start = jnp.where(t < lc, 0, lc)
            end = jnp.where(t < lc, lc, l)
            acc = slab[halo:halo + grp] * w[half:half + 1]
            for s in range(-half, half + 1):
                if s != 0:
                    xs = pltpu.roll(slab, (-s) % n, axis=0)[halo:halo + grp]
                    ok = (t + s >= start) if s < 0 else (t + s < end)
                    acc = acc + jnp.where(ok, xs, 0.0) * w[s + half:s + half + 1]
            o_ref[bb, r0:r0 + grp, ln] = finish(acc)


def _gdn_prep_call(p_gdn, conv_w, lc):
    b, l, n = p_gdn.shape
    wd = GDN_PREP_W
    nbat = _pick(b, (4, 2, 1))
    return pl.pallas_call(
        functools.partial(_gdn_prep_kernel, lc=lc, l=l),
        grid=(b // nbat, n // wd),
        in_specs=[pl.BlockSpec((nbat, l, wd), lambda i, j: (i, 0, j)),
                  pl.BlockSpec((GDN_CONV, wd), lambda i, j: (0, j))],
        out_specs=pl.BlockSpec((nbat, l, wd), lambda i, j: (i, 0, j)),
        out_shape=jax.ShapeDtypeStruct((b, l, n), BF16),
        compiler_params=_params(("parallel", "parallel")),
        name="gdn_prep",
    )(p_gdn, conv_w)


def _each(fn, *lists):
    return [fn(*args) for args in zip(*lists)]


class _TriConsts:
    def __init__(self):
        r = lax.broadcasted_iota(jnp.int32, (CHUNK, 2 * CHUNK), 0)
        lane = lax.broadcasted_iota(jnp.int32, (CHUNK, 2 * CHUNK), 1)
        c = lane % CHUNK
        same = lambda s: (r // s) == (c // s)
        nb = _TRI_BLOCK
        self.left = lane < CHUNK
        self.diag_blocks = same(nb)[:, :CHUNK]
        self.eye = (r == c).astype(F32)[:, :CHUNK]
        right = jnp.logical_not(self.left)
        self.first_is_eye = self.left & (r == c)
        self.first_is_lm = right & jnp.logical_not(same(nb))
        sizes = [nb * 2 ** j for j in range(int(math.log2(CHUNK // nb)))]
        self.level = [right & same(2 * s) & jnp.logical_not(same(s)) for s in sizes]


def _unit_tri_solves(lms, rhs, tc):
    nb = _TRI_BLOCK
    y = [-jnp.where(tc.diag_blocks, lm[:, :CHUNK], 0.0) for lm in lms]
    t = [tc.eye + a for a in y]
    p = y
    for _ in range(int(math.log2(nb)) - 1):
        p = _each(_MMX, p, p)
        t = _each(lambda a, b: a + _MMX(a, b), t, p)
    first = [jnp.where(tc.first_is_eye, 1.0, jnp.where(tc.first_is_lm, lm, 0.0)) for lm in lms]
    x = _each(_mm, t, first)
    zero = jnp.zeros((CHUNK, 2 * CHUNK), F32)
    for mask in tc.level:
        m = [jnp.where(mask, a, 0.0) for a in x]
        x = _each(lambda a, mj: a - _mm(mj, jnp.concatenate([zero, a], axis=0)), x, m)
    t_full = [jnp.where(tc.left, a, 0.0) for a in x]
    zero_rhs = jnp.zeros_like(rhs[0])
    return _each(lambda a, b: _mm(a, jnp.concatenate([b, zero_rhs], axis=0)), t_full, rhs)


def _gdn_kernel(rate_ref, ratel_ref, q_ref, k_ref, v_ref, abc_ref, abr_ref, o_ref, s_ref, *, ncc, nc):
    na = 2 * GDN_HB
    s_ref[...] = jnp.zeros_like(s_ref)
    o_ref[...] = jnp.zeros_like(o_ref)
    masks = [_tri_masks(CHUNK, rev) for rev in (False, True)]
    cum_mats = [m[2].astype(F32) for m in masks]
    r2 = lax.broadcasted_iota(jnp.int32, (CHUNK, 2 * CHUNK), 0)
    c2 = lax.broadcasted_iota(jnp.int32, (CHUNK, 2 * CHUNK), 1) % CHUNK
    incl2 = [c2 <= r2, c2 >= r2]
    strict2 = [c2 < r2, c2 > r2]
    cum_twice = [m.astype(F32) for m in incl2]
    tri = _TriConsts()
    unroll = GDN_UNROLL if nc % GDN_UNROLL == 0 else 1
    streams = [(u, d) for u in range(unroll) for d in range(2)]
    probs = [(si, hh) for si in range(len(streams)) for hh in range(GDN_HB)]
    pdir = [streams[si][1] for si, _ in probs]
    neg_rate_rows = -jnp.exp(rate_ref[0, 0])[:, :CHUNK]
    dtb_rows = rate_ref[0, 1][:, :CHUNK]
    neg_rate_lanes = -jnp.exp(ratel_ref[0, 0, 0:1, :])
    dtb_lanes = ratel_ref[0, 1, 0:1, :]

    def body(i, carry):
        dirs = range(len(streams))
        sdir = [d for _, d in streams]
        chunk = [i * unroll + u if d == 0 else _bwd_chunk(i * unroll + u, ncc, nc) for u, d in streams]
        rows = [_chunk_rows(c) for c in chunk]
        abc = [abc_ref[0, 0, rw, :] for rw in rows]
        abr = [abr_ref[0, 0, c] for c in chunk]
        incl = [incl2[d] for d in pdir]
        strict = [strict2[d] for d in pdir]
        g_cols = [neg_rate_lanes * _softplus(abc[si] + dtb_lanes) for si in dirs]
        g_rows = [neg_rate_rows * _softplus(abr[si] + dtb_rows) for si in dirs]
        gc_cols = [_cum_rows(cum_mats[sdir[si]], g_cols[si]) for si in dirs]
        gc_rows = [_cum_cols(g_rows[si], cum_twice[1 - sdir[si]]) for si in dirs]
        beta_cols = [jax.nn.sigmoid(abc[si]) for si in dirs]
        col = [d * GDN_HB + hh for d, (_, hh) in zip(pdir, probs)]
        gc_col = [jnp.broadcast_to(gc_cols[si][:, j:j + 1], (CHUNK, 128)) for (si, _), j in zip(probs, col)]
        gc_row = [gc_rows[si][j:j + 1, :] for (si, _), j in zip(probs, col)]
        beta_col = [beta_cols[si][:, na + j:na + j + 1] for (si, _), j in zip(probs, col)]
        g_last = [gc[CHUNK - 1:CHUNK] if d == 0 else gc[0:1] for d, gc in zip(pdir, gc_col)]
        decay = [jnp.where(m, jnp.exp(jnp.where(m, a - b, 0.0)), 0.0)
                 for m, a, b in zip(incl, gc_col, gc_row)]

        lanes = [slice(hh * GDN_DK, (hh + 1) * GDN_DK) for _, hh in probs]
        q = [q_ref[0, rows[si], ln].astype(F32) for (si, _), ln in zip(probs, lanes)]
        k = [k_ref[0, rows[si], ln].astype(F32) for (si, _), ln in zip(probs, lanes)]
        v = [v_ref[0, rows[si], ln].astype(F32) for (si, _), ln in zip(probs, lanes)]
        k_beta = _each(jnp.multiply, k, beta_col)
        v_beta = _each(jnp.multiply, v, beta_col)
        e_col = _each(jnp.exp, gc_col)
        kq = _each(_mm_nt, [jnp.concatenate([a, b], axis=0) for a, b in zip(k_beta, q)],
                   [jnp.concatenate([a, a], axis=0) for a in k])
        kk = [x[:CHUNK] for x in kq]
        qk = [x[CHUNK:] for x in kq]
        lm = [jnp.where(m, a * dc, 0.0) for m, a, dc in zip(strict, kk, decay)]
        rhs = [jnp.concatenate([vb, kb * e], axis=1) for vb, kb, e in zip(v_beta, k_beta, e_col)]
        uw = _unit_tri_solves(lm, rhs, tri)
        u = [x[:, :GDN_DV] for x in uw]
        w = [x[:, GDN_DV:] for x in uw]
        att = [jnp.where(m, a * dc, 0.0)[:, :CHUNK] for m, a, dc in zip(incl, qk, decay)]
        q_dec = _each(jnp.multiply, q, e_col)
        k_end = [kx * jnp.exp(gl - gc) for kx, gl, gc in zip(k, g_last, gc_col)]
        nstate = 2 * GDN_HB
        s = [s_ref[j] for j in range(nstate)]
        for pos in range(unroll):
            sel = [p for p, (si, _) in enumerate(probs) if streams[si][0] == pos]
            pick = lambda xs: [xs[p] for p in sel]
            wq = _each(_mm, [jnp.concatenate([a, b], axis=0) for a, b in zip(pick(w), pick(q_dec))], s)
            ws = [x[:CHUNK] for x in wq]
            o_inter = [x[CHUNK:] for x in wq]
            v_new = _each(jnp.subtract, pick(u), ws)
            upd = _each(_mm_tn, pick(k_end), v_new)
            o_intra = _each(_mm, pick(att), v_new)
            s = [sx * jnp.exp(gl) + up for sx, gl, up in zip(s, pick(g_last), upd)]
            for p, a, b in zip(sel, o_intra, o_inter):
                o_ref[0, rows[probs[p][0]], lanes[p]] += a + b
        for j in range(nstate):
            s_ref[j] = s[j]
        return carry

    lax.fori_loop(0, nc // unroll, body, 0)


def _gdn_call(qkvn, abc, abr, a_log, dt_bias, lc):
    b, l, _ = qkvn.shape
    nc, ncc = l // CHUNK, lc // CHUNK
    nhb = GDN_HEADS // GDN_HB
    wb = GDN_HB * GDN_DK
    na = 2 * GDN_HB
    rate = jnp.stack([a_log, dt_bias]).reshape(2, 2, nhb, GDN_HB).transpose(2, 0, 1, 3).reshape(nhb, 2, na)
    rate = jnp.pad(rate, ((0, 0), (0, 0), (0, 128 - na)))
    nr = _GDN_GATE_ROWS
    rate_rows = jnp.broadcast_to(rate[:, :, :nr, None], (nhb, 2, nr, 128))
    rate_lanes = jnp.broadcast_to(rate[:, :, None, :], (nhb, 2, 8, 128))
    return pl.pallas_call(
        functools.partial(_gdn_kernel, ncc=ncc, nc=nc),
        grid=(b, nhb),
        in_specs=[pl.BlockSpec((1, 2, nr, 128), lambda i, h: (h, 0, 0, 0)),
                  pl.BlockSpec((1, 2, 8, 128), lambda i, h: (h, 0, 0, 0)),
                  pl.BlockSpec((1, l, wb), lambda i, h: (i, 0, h)),
                  pl.BlockSpec((1, l, wb), lambda i, h: (i, 0, nhb + h)),
                  pl.BlockSpec((1, l, wb), lambda i, h: (i, 0, 2 * nhb + h)),
                  pl.BlockSpec((1, 1, l, 128), lambda i, h: (i, h, 0, 0)),
                  pl.BlockSpec((1, 1, nc, nr, CHUNK), lambda i, h: (i, h, 0, 0, 0))],
        out_specs=pl.BlockSpec((1, l, wb), lambda i, h: (i, 0, h)),
        out_shape=jax.ShapeDtypeStruct((b, l, GDN_HEADS * GDN_DV), F32),
        scratch_shapes=[pltpu.VMEM((2 * GDN_HB, GDN_DK, GDN_DV), F32)],
        compiler_params=_params(("parallel", "parallel")),
        name="gdn_scan",
    )(rate_rows, rate_lanes, qkvn, qkvn, qkvn, abc, abr)


def _gdn_gate_layouts(p_mla, lc):
    b, l, _ = p_mla.shape
    nhb = GDN_HEADS // GDN_HB
    na = 2 * GDN_HB
    o = MLA_Q_LORA + MLA_KV_LORA
    ab = p_mla[:, :, o + SM_A:o + SM_A + 32].reshape(b, l, 2, 2, nhb, GDN_HB)
    ab = ab.transpose(0, 4, 1, 2, 3, 5).reshape(b, nhb, l, 2 * na)
    abr = ab[..., :na].reshape(b, nhb, l // CHUNK, CHUNK, na).transpose(0, 1, 2, 4, 3)
    abr = jnp.pad(abr, ((0, 0), (0, 0), (0, 0), (0, _GDN_GATE_ROWS - na), (0, 0)))
    abc = jnp.pad(ab, ((0, 0), (0, 0), (0, 0), (0, 128 - 2 * na)))
    return abc, abr


def _rope_swap(x):
    lane = lax.broadcasted_iota(jnp.int32, x.shape, 1)
    n = x.shape[1]
    return jnp.where(lane % 32 < 16, pltpu.roll(x, n - 16, axis=1), pltpu.roll(x, 16, axis=1))


def _mla_prep_kernel(p_ref, cos_ref, sin_ref, qnw_ref, qup_ref, kvnw_ref, kvup_ref,
                     wqn_ref, wqr_ref, wkn_ref, wkr_ref, q_ref, k_ref, v_ref):
    p = p_ref[0]
    cos_t = cos_ref[...]
    sin_t = sin_ref[...]

    def rms(x, n):
        return x * lax.rsqrt(jnp.sum(x * x, axis=-1, keepdims=True) * (1.0 / n) + EPS)

    def rope(x):
        return x * cos_t + _rope_swap(x) * sin_t

    qd = p[:, :MLA_Q_LORA]
    kvd = p[:, MLA_Q_LORA:MLA_Q_LORA + MLA_KV_LORA]
    sm = p[:, MLA_Q_LORA + MLA_KV_LORA:]
    q = _mm(rms(qd, MLA_Q_LORA) * qnw_ref[...], qup_ref[...])
    kv = _mm(rms(kvd, MLA_KV_LORA) * kvnw_ref[...], kvup_ref[...])
    lane = lax.broadcasted_iota(jnp.int32, sm.shape, 1)
    kr = jnp.where(lane < MLA_ROPE, sm, 0.0)
    kr = rope(rms(kr, MLA_ROPE) * wkr_ref[...])
    for h in range(MLA_HEADS):
        base = h * 256
        qn = rms(q[:, base:base + 128], MLA_NOPE) * wqn_ref[...]
        q_ref[0, :, base:base + 128] = (qn * MLA_Q_PRESCALE).astype(q_ref.dtype)
        qr = rms(q[:, base + 128:base + 256], MLA_ROPE) * wqr_ref[...]
        q_ref[0, :, base + 128:base + 256] = (rope(qr) * MLA_Q_PRESCALE).astype(q_ref.dtype)
        kn = kv[:, h * 128:(h + 1) * 128]
        k_ref[0, :, base:base + 128] = (rms(kn, MLA_NOPE) * wkn_ref[...]).astype(k_ref.dtype)
        k_ref[0, :, base + 128:base + 256] = kr.astype(k_ref.dtype)
    v_ref[0] = kv[:, MLA_HEADS * MLA_NOPE:].T.astype(v_ref.dtype)


def _mla_prep_call(p_mla, cos_t, sin_t, qnw, qup, kvnw, kvup, wqn, wqr, wkn, wkr):
    b, l, gw = p_mla.shape
    tm = _pick(l, (768, 384, 256, 128))
    hq = MLA_HEADS * 256
    full = lambda shape: pl.BlockSpec(shape, lambda i, t: tuple(0 for _ in shape))
    return pl.pallas_call(
        _mla_prep_kernel,
        grid=(b, l // tm),
        in_specs=[pl.BlockSpec((1, tm, gw), lambda i, t: (i, t, 0)),
                  pl.BlockSpec((tm, 128), lambda i, t: (t, 0)),
                  pl.BlockSpec((tm, 128), lambda i, t: (t, 0)),
                  full((1, MLA_Q_LORA)), full((MLA_Q_LORA, hq)),
                  full((1, MLA_KV_LORA)), full((MLA_KV_LORA, hq)),
                  full((1, 128)), full((1, 128)), full((1, 128)), full((1, 128))],
        out_specs=[pl.BlockSpec((1, tm, hq), lambda i, t: (i, t, 0)),
                   pl.BlockSpec((1, tm, hq), lambda i, t: (i, t, 0)),
                   pl.BlockSpec((1, MLA_HEADS * MLA_DV, tm), lambda i, t: (i, 0, t))],
        out_shape=[jax.ShapeDtypeStruct((b, l, hq), BF16),
                   jax.ShapeDtypeStruct((b, l, hq), BF16),
                   jax.ShapeDtypeStruct((b, MLA_HEADS * MLA_DV, l), BF16)],
        compiler_params=_params(("parallel", "parallel")),
        name="mla_prep",
    )(p_mla, cos_t, sin_t, qnw, qup, kvnw, kvup, wqn, wqr, wkn, wkr)


def _mla_attn_kernel(q_ref, k_ref, vt_ref, o_ref, *, lc, tq, q_off):
    qi = pl.program_id(2) + q_off
    heads = range(MLA_HP)

    def attend(nk):
        def scores(h):
            return lax.dot_general(k_ref[0, :nk, h * 256:(h + 1) * 256], q_ref[0, :, h * 256:(h + 1) * 256],
                                   (((1,), (1,)), ((), ())), preferred_element_type=F32)

        ahead = MLA_HP
        st = {h: scores(h) for h in range(min(ahead, MLA_HP))}
        for h in heads:
            x = st.pop(h)
            e = jnp.exp2(x - jnp.max(x, axis=0, keepdims=True))
            den = jnp.sum(e, axis=0, keepdims=True)
            if h + ahead < MLA_HP:
                st[h + ahead] = scores(h + ahead)
            ot = jnp.dot(vt_ref[0, h * MLA_DV:(h + 1) * MLA_DV, :nk], e.astype(BF16),
                         preferred_element_type=F32)
            o_ref[0, :, h * MLA_DV:(h + 1) * MLA_DV] = (ot / den).T.astype(o_ref.dtype)

    if q_off == 0:
        @pl.when(qi * tq < lc)
        def _():
            attend(lc)

        @pl.when(qi * tq >= lc)
        def _():
            attend(k_ref.shape[1])
    else:
        attend(k_ref.shape[1])


def _mla_attn_call(q, k, vt, lc, with_ctx):
    b, l, _ = q.shape
    tq = _pick(lc, (256, 128))
    q_off = 0 if with_ctx else lc // tq
    nq = l // tq - q_off
    return pl.pallas_call(
        functools.partial(_mla_attn_kernel, lc=lc, tq=tq, q_off=q_off),
        grid=(b, MLA_HEADS // MLA_HP, nq),
        in_specs=[pl.BlockSpec((1, tq, MLA_HP * 256), lambda i, h, t: (i, t + q_off, h)),
                  pl.BlockSpec((1, l, MLA_HP * 256), lambda i, h, t: (i, 0, h)),
                  pl.BlockSpec((1, MLA_HP * MLA_DV, l), lambda i, h, t: (i, h, 0))],
        out_specs=pl.BlockSpec((1, tq, MLA_HP * MLA_DV), lambda i, h, t: (i, t, h)),
        out_shape=jax.ShapeDtypeStruct((b, nq * tq, MLA_HEADS * MLA_DV), F32),
        compiler_params=_params(("parallel", "parallel", "arbitrary")),
        name="mla_attn",
    )(q, k, vt)


def _merge_kernel(h_ref, oa_ref, ob_ref, oc_ref, za_ref, zb_ref, zc_ref, g0_ref, g1_ref, g2_ref,
                  modx_ref, modc_ref, wa_ref, wb_ref, wbr_ref, wout_ref, o_ref, *, lc, tm, t_off):
    t = pl.program_id(1) + t_off

    def head_norm(o, width):
        parts = []
        for h in range(BRANCH_W // width):
            x = o[:, h * width:(h + 1) * width]
            parts.append(x * lax.rsqrt(jnp.mean(x * x, axis=-1, keepdims=True) + EPS))
        return jnp.concatenate(parts, axis=-1)

    ya = head_norm(oa_ref[0], GLA_DV) * wa_ref[...] * _silu(za_ref[0].astype(F32))
    yb = head_norm(ob_ref[0], GDN_DV) * wb_ref[...] * _silu(zb_ref[0].astype(F32))
    yc = oc_ref[0] * _silu(zc_ref[0].astype(F32))
    acc = _sigmoid(g0_ref[0].astype(F32)) * _mm(ya, wbr_ref[0])
    acc = acc + _sigmoid(g1_ref[0].astype(F32)) * _mm(yb, wbr_ref[1])
    acc = acc + _sigmoid(g2_ref[0].astype(F32)) * _mm(yc, wbr_ref[2])
    d = D_MODEL
    gate = _row_select(t, tm, lc, modc_ref[:, 2 * d:], modx_ref[0][:, 2 * d:])
    o_ref[0] = h_ref[0] + gate * _mm(acc, wout_ref[...])


def _merge_call(h, oa, ob, oc, p_zg, modx, modc, wa, wb, w_branch, w_out, lc, with_ctx):
    b, l, d = h.shape
    tm = _pick(lc, (256, 128))
    t_off = 0 if with_ctx else lc // tm
    nt = l // tm - t_off
    oc_off = 0 if oc.shape[1] == l else lc // tm
    row = lambda i, t: (i, t + t_off, 0)
    zg = lambda n: pl.BlockSpec((1, tm, d), lambda i, t: (i, t + t_off, n))
    return pl.pallas_call(
        functools.partial(_merge_kernel, lc=lc, tm=tm, t_off=t_off),
        grid=(b, nt),
        in_specs=[pl.BlockSpec((1, tm, d), row), pl.BlockSpec((1, tm, d), row), pl.BlockSpec((1, tm, d), row),
                  pl.BlockSpec((1, tm, d), lambda i, t: (i, t + t_off - oc_off, 0)),
                  zg(0), zg(1), zg(2), zg(3), zg(4), zg(5),
                  pl.BlockSpec((1, 1, 3 * d), lambda i, t: (i, 0, 0)),
                  pl.BlockSpec((1, 3 * d), lambda i, t: (0, 0)),
                  pl.BlockSpec((1, d), lambda i, t: (0, 0)),
                  pl.BlockSpec((1, d), lambda i, t: (0, 0)),
                  pl.BlockSpec((3, d, d), lambda i, t: (0, 0, 0)),
                  pl.BlockSpec((d, d), lambda i, t: (0, 0))],
        out_specs=pl.BlockSpec((1, tm, d), lambda i, t: (i, t, 0)),
        out_shape=jax.ShapeDtypeStruct((b, nt * tm, d), F32),
        compiler_params=_params(("parallel", "parallel")),
        name="merge",
    )(h, oa, ob, oc, p_zg, p_zg, p_zg, p_zg, p_zg, p_zg, modx, modc, wa, wb, w_branch, w_out)


def _cols(w, *names):
    return jnp.concatenate([w[:, _IN_OFF[n]:_IN_OFF[n] + _IN_LEN[n]] for n in names], axis=1)


def _rope_tables(t_lat, lc):
    rows = t_lat // GRID_W
    row = jnp.repeat(jnp.arange(rows, dtype=F32), GRID_W)
    col = jnp.tile(jnp.arange(GRID_W, dtype=F32), rows)
    inv_freq = jnp.power(ROPE_BASE, -jnp.arange(ROPE_FREQ, dtype=F32) / ROPE_FREQ)
    ar = row[:, None] * inv_freq
    ac = col[:, None] * inv_freq
    zeros = jnp.zeros((t_lat, 128 - MLA_ROPE), F32)
    cos_t = jnp.concatenate([jnp.cos(ar), jnp.cos(ar), jnp.cos(ac), jnp.cos(ac), zeros], axis=1)
    sin_t = jnp.concatenate([-jnp.sin(ar), jnp.sin(ar), -jnp.sin(ac), jnp.sin(ac), zeros], axis=1)
    cos_c = jnp.concatenate([jnp.ones((lc, MLA_ROPE), F32), jnp.zeros((lc, 128 - MLA_ROPE), F32)], axis=1)
    return (jnp.concatenate([cos_c, cos_t], axis=0),
            jnp.concatenate([jnp.zeros((lc, 128), F32), sin_t], axis=0))


def _pad_lanes(w, n=128):
    return jnp.pad(w, (0, n - w.shape[0])).reshape(1, n)


def kernel(x, c, ctx, c_ctx, norm_w, ada_w, ada_b, w_in, gla_gate_w2, gla_gate_b, gla_norm_w, gdn_conv_w,
           gdn_a_log, gdn_dt_bias, gdn_norm_w, mla_q_norm_w, mla_q_up, mla_kv_norm_w, mla_kv_up,
           mla_qn_nope, mla_qn_rope, mla_kn_nope, mla_kn_rope, w_branch, w_out):
    b, t_lat, d = x.shape
    lc = ctx.shape[1]
    l = lc + t_lat
    depth = w_in.shape[0]
    assert d == D_MODEL and lc % 128 == 0 and t_lat % 128 == 0

    h = jnp.concatenate([ctx, x], axis=1)
    rows = ((b + 1 + 7) // 8) * 8
    c_all = jnp.concatenate([c, c_ctx[None, :], jnp.zeros((rows - b - 1, d), F32)], axis=0)
    mod = _ada_call(c_all, ada_w, ada_b)
    cos_t, sin_t = _rope_tables(t_lat, lc)

    for li in range(depth):
        last = li == depth - 1
        modx = mod[li, :b].reshape(b, 1, 3 * d)
        modc = mod[li, b:b + 1]
        w = w_in[li]
        w_mla = _cols(w, 'mla_q_down', 'mla_kv_down', 'mla_k_rope', 'gla_gate_lr', 'gdn_a', 'gdn_b').astype(BF16)
        w_gla = _cols(w, 'gla_q', 'gla_k', 'gla_v').astype(BF16)
        w_gdn = _cols(w, 'gdn_qkv').astype(BF16)
        w_zg = _cols(w, 'gla_z', 'gdn_z', 'mla_z', 'merge_gate').astype(BF16)

        hn = _prenorm_call(h, modx, modc, norm_w[li], lc).reshape(b * l, d)
        p_mla = _proj_call(hn, w_mla, F32, "proj_mla").reshape(b, l, -1)
        p_gla = _proj_call(hn, w_gla, BF16, "proj_gla").reshape(b, l, -1)
        p_gdn = _proj_call(hn, w_gdn, F32, "proj_gdn").reshape(b, l, -1)
        p_zg = _proj_call(hn, w_zg, BF16, "proj_zg").reshape(b, l, -1)

        w2e = jnp.stack([jnp.pad(gla_gate_w2[li, dd], ((SM_LR + dd * GLA_RANK, 128 - SM_LR - (dd + 1) * GLA_RANK),
                                                      (0, 0))) for dd in range(2)])
        oa = _gla_call(p_gla, p_mla, w2e, gla_gate_b[li].reshape(2, 1, -1), lc)

        qkvn = _gdn_prep_call(p_gdn, gdn_conv_w[li], lc)
        abc, abr = _gdn_gate_layouts(p_mla, lc)
        ob = _gdn_call(qkvn, abc, abr, gdn_a_log[li], gdn_dt_bias[li], lc)

        qup = mla_q_up[li].reshape(MLA_Q_LORA, MLA_HEADS, MLA_NOPE + MLA_ROPE)
        qup = jnp.pad(qup, ((0, 0), (0, 0), (0, 256 - MLA_NOPE - MLA_ROPE))).reshape(MLA_Q_LORA, -1).astype(BF16)
        kvup = mla_kv_up[li].reshape(MLA_KV_LORA, MLA_HEADS, MLA_NOPE + MLA_DV)
        kvup = jnp.concatenate([kvup[:, :, :MLA_NOPE].reshape(MLA_KV_LORA, -1),
                                kvup[:, :, MLA_NOPE:].reshape(MLA_KV_LORA, -1)], axis=1).astype(BF16)
        qm, km, vm = _mla_prep_call(
            p_mla, cos_t, sin_t, mla_q_norm_w[li].reshape(1, -1), qup, mla_kv_norm_w[li].reshape(1, -1), kvup,
            mla_qn_nope[li].reshape(1, -1), _pad_lanes(mla_qn_rope[li]),
            mla_kn_nope[li].reshape(1, -1), _pad_lanes(mla_kn_rope[li]))
        oc = _mla_attn_call(qm, km, vm, lc, with_ctx=not last)

        h = _merge_call(h, oa, ob, oc, p_zg, modx, modc,
                        jnp.tile(gla_norm_w[li], GLA_HEADS).reshape(1, -1),
                        jnp.tile(gdn_norm_w[li], GDN_HEADS).reshape(1, -1),
                        w_branch[li].astype(BF16), w_out[li].astype(BF16), lc, with_ctx=not last)
    return h
```

```python
import functools
import math

import numpy as np
import jax
import jax.numpy as jnp
from jax import lax
from jax.experimental import pallas as pl
from jax.experimental.pallas import tpu as pltpu

F32 = jnp.float32
BF16 = jnp.bfloat16

D_MODEL = 1024
EPS = 1e-6
GRID_W = 64
CHUNK = 64

GLA_HEADS, GLA_DK, GLA_DV, GLA_RANK, GLA_NORMALIZER = 4, 128, 256, 16, 16.0
GDN_HEADS, GDN_DK, GDN_DV, GDN_CONV = 8, 128, 128, 5
GLA_HB = 4
GDN_HB = 8
_GDN_GATE_ROWS = max(8, 2 * GDN_HB)
GDN_PREP_W = 128
GDN_UNROLL = 1
MLA_HEADS, MLA_Q_LORA, MLA_KV_LORA, MLA_NOPE, MLA_ROPE, MLA_DV = 8, 384, 256, 128, 64, 128
MLA_SCALE = (MLA_NOPE + MLA_ROPE) ** -0.5
MLA_Q_PRESCALE = MLA_SCALE * math.log2(math.e)
MLA_HP = 8
ROPE_FREQ = MLA_ROPE // 4
ROPE_BASE = 10000.0
BRANCH_W = 1024

_IN_SIZES = (512, 512, 1024, 32, 1024, 3072, 16, 16, 1024, 384, 256, 64, 1024, 3072)
_IN_NAMES = ('gla_q', 'gla_k', 'gla_v', 'gla_gate_lr', 'gla_z', 'gdn_qkv', 'gdn_a', 'gdn_b', 'gdn_z',
             'mla_q_down', 'mla_kv_down', 'mla_k_rope', 'mla_z', 'merge_gate')
_IN_OFF = dict(zip(_IN_NAMES, np.concatenate([[0], np.cumsum(_IN_SIZES)[:-1]]).tolist()))
_IN_LEN = dict(zip(_IN_NAMES, _IN_SIZES))

SM_ROPE, SM_LR, SM_A, SM_B = 0, 64, 96, 112
MLA_GROUP_W = MLA_Q_LORA + MLA_KV_LORA + 128
SMALL_BLOCK = (MLA_Q_LORA + MLA_KV_LORA) // 128

VMEM_LIMIT = 58 * 1024 * 1024


def _params(sem):
    return pltpu.CompilerParams(dimension_semantics=sem, vmem_limit_bytes=VMEM_LIMIT)


def _pick(n, cands):
    for c in cands:
        if n % c == 0:
            return c
    raise ValueError(f"no tile for {n} in {cands}")


def _mm(a, b):
    return jnp.dot(a.astype(BF16), b.astype(BF16), preferred_element_type=F32)


def _mm_nt(a, b):
    return lax.dot_general(a.astype(BF16), b.astype(BF16), (((1,), (1,)), ((), ())),
                           preferred_element_type=F32)


def _mm_tn(a, b):
    return lax.dot_general(a.astype(BF16), b.astype(BF16), (((0,), (0,)), ((), ())),
                           preferred_element_type=F32)


def _split(a):
    hi = a.astype(BF16)
    lo = (a - hi.astype(F32)).astype(BF16)
    return hi, lo


def _mm3(a, b):
    ah, al = _split(a)
    bh, bl = _split(b)
    d = functools.partial(jnp.dot, preferred_element_type=F32)
    return d(ah, bh) + (d(ah, bl) + d(al, bh))


_MMX = _mm
_TRI_BLOCK = 2


def _cum_rows(cum, x):
    cb = cum.astype(BF16)
    d = functools.partial(jnp.dot, preferred_element_type=F32)
    hi, lo = _split(x)
    return d(cb, hi) + d(cb, lo)


def _cum_cols(x, cum):
    cb = cum.astype(BF16)
    d = functools.partial(jnp.dot, preferred_element_type=F32)
    hi, lo = _split(x)
    return d(hi, cb) + d(lo, cb)


def _softplus(x):
    return jnp.maximum(x, 0.0) + jnp.log(1.0 + jnp.exp(-jnp.abs(x)))


def _log_sigmoid(x):
    return jnp.minimum(x, 0.0) - jnp.log(1.0 + jnp.exp(-jnp.abs(x)))


_sigmoid = jax.nn.sigmoid


def _silu(x):
    return x * _sigmoid(x)


def _tri_masks(n, reverse):
    r = lax.broadcasted_iota(jnp.int32, (n, n), 0)
    c = lax.broadcasted_iota(jnp.int32, (n, n), 1)
    incl = (c >= r) if reverse else (c <= r)
    strict = (c > r) if reverse else (c < r)
    return r, c, incl, strict


def _ada_kernel(c_ref, w_ref, b_ref, o_ref):
    o_ref[0] = _mm3(_silu(c_ref[...]), w_ref[0]) + b_ref[0]


def _ada_call(c_all, ada_w, ada_b):
    nl, d, n3 = ada_w.shape
    r = c_all.shape[0]
    tn = 1024
    return pl.pallas_call(
        _ada_kernel,
        grid=(nl, n3 // tn),
        in_specs=[pl.BlockSpec((r, d), lambda l, j: (0, 0)),
                  pl.BlockSpec((1, d, tn), lambda l, j: (l, 0, j)),
                  pl.BlockSpec((1, 1, tn), lambda l, j: (l, 0, j))],
        out_specs=pl.BlockSpec((1, r, tn), lambda l, j: (l, 0, j)),
        out_shape=jax.ShapeDtypeStruct((nl, r, n3), F32),
        compiler_params=_params(("arbitrary", "arbitrary")),
        name="ada_rows",
    )(c_all, ada_w, ada_b.reshape(nl, 1, n3))


def _row_select(t, tm, lc, ctx_row, lat_row):
    rows = t * tm + lax.broadcasted_iota(jnp.int32, (tm, 1), 0)
    return jnp.where(rows < lc, ctx_row, lat_row)


def _prenorm_kernel(h_ref, modx_ref, modc_ref, nw_ref, o_ref, *, lc, tm):
    t = pl.program_id(1)
    x = h_ref[0]
    y = x * lax.rsqrt(jnp.mean(x * x, axis=-1, keepdims=True) + EPS) * nw_ref[...]
    mx = modx_ref[0]
    mc = modc_ref[...]
    d = D_MODEL
    shift = _row_select(t, tm, lc, mc[:, :d], mx[:, :d])
    scale = _row_select(t, tm, lc, mc[:, d:2 * d], mx[:, d:2 * d])
    o_ref[0] = (y * (1.0 + scale) + shift).astype(o_ref.dtype)


def _prenorm_call(h, modx, modc, norm_w, lc):
    b, l, d = h.shape
    tm = _pick(l, (768, 512, 384, 256, 128))
    return pl.pallas_call(
        functools.partial(_prenorm_kernel, lc=lc, tm=tm),
        grid=(b, l // tm),
        in_specs=[pl.BlockSpec((1, tm, d), lambda i, t: (i, t, 0)),
                  pl.BlockSpec((1, 1, 3 * d), lambda i, t: (i, 0, 0)),
                  pl.BlockSpec((1, 3 * d), lambda i, t: (0, 0)),
                  pl.BlockSpec((1, d), lambda i, t: (0, 0))],
        out_specs=pl.BlockSpec((1, tm, d), lambda i, t: (i, t, 0)),
        out_shape=jax.ShapeDtypeStruct((b, l, d), BF16),
        compiler_params=_params(("parallel", "parallel")),
        name="prenorm",
    )(h, modx, modc, norm_w.reshape(1, d))


def _proj_kernel(x_ref, w_ref, o_ref):
    o_ref[...] = jnp.dot(x_ref[...], w_ref[...], preferred_element_type=F32).astype(o_ref.dtype)


def _proj_call(x, w, out_dtype, name):
    m, k = x.shape
    n = w.shape[1]
    tm = _pick(m, (2048, 1024, 768, 512, 384, 256, 128))
    tn = _pick(n, (1536, 1024, 768, 512, 256, 128))
    return pl.pallas_call(
        _proj_kernel,
        grid=(m // tm, n // tn),
        in_specs=[pl.BlockSpec((tm, k), lambda i, j: (i, 0)),
                  pl.BlockSpec((k, tn), lambda i, j: (0, j))],
        out_specs=pl.BlockSpec((tm, tn), lambda i, j: (i, j)),
        out_shape=jax.ShapeDtypeStruct((m, n), out_dtype),
        compiler_params=_params(("parallel", "parallel")),
        name=name,
    )(x, w)


def _bwd_chunk(i, ncc, nc):
    return jnp.where(i < ncc, ncc - 1 - i, ncc + nc - 1 - i)


def _chunk_rows(c):
    return pl.ds(pl.multiple_of(c * CHUNK, CHUNK), CHUNK)


def _gla_kernel(q_ref, k_ref, v_ref, sm_ref, w2_ref, gb_ref, o_ref, la_ref, st_ref, *, ncc, nc):
    sm = sm_ref[0]
    for d in range(2):
        logit = _mm3(sm, w2_ref[d]) + gb_ref[d]
        la_ref[d] = _log_sigmoid(logit) * (1.0 / GLA_NORMALIZER)
    st_ref[...] = jnp.zeros_like(st_ref)
    o_ref[...] = jnp.zeros_like(o_ref)
    masks = [_tri_masks(CHUNK, rev) for rev in (False, True)]
    cum_mats = [m[2].astype(F32) for m in masks]
    probs = [(d, hh) for d in range(2) for hh in range(GLA_HB)]

    kl = [slice(hh * GLA_DK, (hh + 1) * GLA_DK) for _, hh in probs]
    vl = [slice(hh * GLA_DV, (hh + 1) * GLA_DV) for _, hh in probs]
    unroll = 2 if nc % 2 == 0 else 1

    def body(i, carry):
        steps = range(unroll)
        rows = [[_chunk_rows(i * unroll + u), _chunk_rows(_bwd_chunk(i * unroll + u, ncc, nc))] for u in steps]
        bcum_d = [[_cum_rows(cum_mats[d], la_ref[d, rows[u][d], :]) for d in range(2)] for u in steps]
        bcum = [[bcum_d[u][d][:, ln] for (d, _), ln in zip(probs, kl)] for u in steps]
        b_last = [[bc[CHUNK - 1:CHUNK] if d == 0 else bc[0:1] for (d, _), bc in zip(probs, bcum[u])] for u in steps]
        q = [[q_ref[0, rows[u][d], ln].astype(F32) * GLA_DK ** -0.5 for (d, _), ln in zip(probs, kl)] for u in steps]
        k = [[k_ref[0, rows[u][d], ln].astype(F32) for (d, _), ln in zip(probs, kl)] for u in steps]
        v = [[v_ref[0, rows[u][d], ln].astype(F32) for (d, _), ln in zip(probs, vl)] for u in steps]
        q_dec = [[a * jnp.exp(bc) for a, bc in zip(q[u], bcum[u])] for u in steps]
        k_inv = [[a * jnp.exp(-bc) for a, bc in zip(k[u], bcum[u])] for u in steps]
        k_end = [[a * jnp.exp(bl - bc) for a, bl, bc in zip(k[u], b_last[u], bcum[u])] for u in steps]
        st = [st_ref[j] for j in range(len(probs))]
        qk = [_each(_mm_nt, q_dec[u], k_inv[u]) for u in steps]
        o_inter = [_each(_mm_nt, q_dec[0], st)]
        upd = [_each(_mm_tn, v[u], k_end[u]) for u in steps]
        att = [[jnp.where(masks[d][2], a, 0.0) for (d, _), a in zip(probs, qk[u])] for u in steps]
        o_intra = [_each(_mm, att[u], v[u]) for u in steps]
        for u in steps:
            st = [sx * jnp.exp(bl) + up for sx, bl, up in zip(st, b_last[u], upd[u])]
            if u + 1 < unroll:
                o_inter.append(_each(_mm_nt, q_dec[u + 1], st))
        for j, sx in enumerate(st):
            st_ref[j] = sx
        for u in steps:
            for (d, _), ln, a, b in zip(probs, vl, o_intra[u], o_inter[u]):
                o_ref[0, rows[u][d], ln] += a + b
        return carry

    lax.fori_loop(0, nc // unroll, body, 0)


def _gla_call(p_gla, p_mla, w2e, gate_b, lc):
    b, l, _ = p_gla.shape
    nc, ncc = l // CHUNK, lc // CHUNK
    nhb = GLA_HEADS // GLA_HB
    wk, wv = GLA_HB * GLA_DK, GLA_HB * GLA_DV
    return pl.pallas_call(
        functools.partial(_gla_kernel, ncc=ncc, nc=nc),
        grid=(b, nhb),
        in_specs=[pl.BlockSpec((1, l, wk), lambda i, h: (i, 0, h)),
                  pl.BlockSpec((1, l, wk), lambda i, h: (i, 0, nhb + h)),
                  pl.BlockSpec((1, l, wv), lambda i, h: (i, 0, nhb + h)),
                  pl.BlockSpec((1, l, 128), lambda i, h: (i, 0, SMALL_BLOCK)),
                  pl.BlockSpec((2, 128, wk), lambda i, h: (0, 0, h)),
                  pl.BlockSpec((2, 1, wk), lambda i, h: (0, 0, h))],
        out_specs=pl.BlockSpec((1, l, wv), lambda i, h: (i, 0, h)),
        out_shape=jax.ShapeDtypeStruct((b, l, GLA_HEADS * GLA_DV), F32),
        scratch_shapes=[pltpu.VMEM((2, l, wk), F32), pltpu.VMEM((2 * GLA_HB, GLA_DV, GLA_DK), F32)],
        compiler_params=_params(("parallel", "parallel")),
        name="gla_scan",
    )(p_gla, p_gla, p_gla, p_mla, w2e, gate_b)


def _gdn_prep_kernel(x_ref, w_ref, o_ref, *, lc, l):
    half = GDN_CONV // 2
    nblk_head = GDN_HEADS * GDN_DK // 128
    grp, halo = 16, 8
    zeros = jnp.zeros((halo, 128), F32)
    slices = [(bb, g) for bb in range(x_ref.shape[0]) for g in range(GDN_PREP_W // 128)]
    for bb, g in slices:
        j = pl.program_id(1) * (GDN_PREP_W // 128) + g
        ln = slice(g * 128, (g + 1) * 128)
        w = w_ref[:, ln]

        def finish(acc):
            y = _silu(acc)
            inv = lax.rsqrt(jnp.sum(y * y, axis=-1, keepdims=True) + EPS)
            fac = jnp.where(j < nblk_head, inv * GDN_DK ** -0.5, jnp.where(j < 2 * nblk_head, inv, 1.0))
            return (y * fac).astype(o_ref.dtype)

        acc = None
        for s in range(-half, half + 1):
            term = x_ref[bb, grp + s:l - grp + s, ln].astype(F32) * w[s + half:s + half + 1]
            acc = term if acc is None else acc + term
        o_ref[bb, grp:l - grp, ln] = finish(acc)

        for r0 in sorted({0, lc - grp, lc, l - grp}):
            lo, hi = max(r0 - halo, 0), min(r0 + grp + halo, l)
            slab = x_ref[bb, lo:hi, ln].astype(F32)
            if r0 - halo < 0:
                slab = jnp.concatenate([zeros, slab], axis=0)
            if r0 + grp + halo > l:
                slab = jnp.concatenate([slab, zeros], axis=0)
            n = grp + 2 * halo
            t = r0 + lax.broadcasted_iota(jnp.int32, (grp, 128), 0)
            start = jnp.where(t < lc, 0, lc)
            end = jnp.where(t < lc, lc, l)
            acc = slab[halo:halo + grp] * w[half:half + 1]
            for s in range(-half, half + 1):
                if s != 0:
                    xs = pltpu.roll(slab, (-s) % n, axis=0)[halo:halo + grp]
                    ok = (t + s >= start) if s < 0 else (t + s < end)
                    acc = acc + jnp.where(ok, xs, 0.0) * w[s + half:s + half + 1]
            o_ref[bb, r0:r0 + grp, ln] = finish(acc)


def _gdn_prep_call(p_gdn, conv_w, lc):
    b, l, n = p_gdn.shape
    wd = GDN_PREP_W
    nbat = _pick(b, (4, 2, 1))
    return pl.pallas_call(
        functools.partial(_gdn_prep_kernel, lc=lc, l=l),
        grid=(b // nbat, n // wd),
        in_specs=[pl.BlockSpec((nbat, l, wd), lambda i, j: (i, 0, j)),
                  pl.BlockSpec((GDN_CONV, wd), lambda i, j: (0, j))],
        out_specs=pl.BlockSpec((nbat, l, wd), lambda i, j: (i, 0, j)),
        out_shape=jax.ShapeDtypeStruct((b, l, n), BF16),
        compiler_params=_params(("parallel", "parallel")),
        name="gdn_prep",
    )(p_gdn, conv_w)


def _each(fn, *lists):
    return [fn(*args) for args in zip(*lists)]


class _TriConsts:
    def __init__(self):
        r = lax.broadcasted_iota(jnp.int32, (CHUNK, 2 * CHUNK), 0)
        lane = lax.broadcasted_iota(jnp.int32, (CHUNK, 2 * CHUNK), 1)
        c = lane % CHUNK
        same = lambda s: (r // s) == (c // s)
        nb = _TRI_BLOCK
        self.left = lane < CHUNK
        self.diag_blocks = same(nb)[:, :CHUNK]
        self.eye = (r == c).astype(F32)[:, :CHUNK]
        right = jnp.logical_not(self.left)
        self.first_is_eye = self.left & (r == c)
        self.first_is_lm = right & jnp.logical_not(same(nb))
        sizes = [nb * 2 ** j for j in range(int(math.log2(CHUNK // nb)))]
        self.level = [right & same(2 * s) & jnp.logical_not(same(s)) for s in sizes]


def _unit_tri_solves(lms, rhs, tc):
    nb = _TRI_BLOCK
    y = [-jnp.where(tc.diag_blocks, lm[:, :CHUNK], 0.0) for lm in lms]
    t = [tc.eye + a for a in y]
    p = y
    for _ in range(int(math.log2(nb)) - 1):
        p = _each(_MMX, p, p)
        t = _each(lambda a, b: a + _MMX(a, b), t, p)
    first = [jnp.where(tc.first_is_eye, 1.0, jnp.where(tc.first_is_lm, lm, 0.0)) for lm in lms]
    x = _each(_mm, t, first)
    zero = jnp.zeros((CHUNK, 2 * CHUNK), F32)
    for mask in tc.level:
        m = [jnp.where(mask, a, 0.0) for a in x]
        x = _each(lambda a, mj: a - _mm(mj, jnp.concatenate([zero, a], axis=0)), x, m)
    t_full = [jnp.where(tc.left, a, 0.0) for a in x]
    zero_rhs = jnp.zeros_like(rhs[0])
    return _each(lambda a, b: _mm(a, jnp.concatenate([b, zero_rhs], axis=0)), t_full, rhs)


def _gdn_kernel(rate_ref, ratel_ref, q_ref, k_ref, v_ref, abc_ref, abr_ref, o_ref, s_ref, *, ncc, nc):
    na = 2 * GDN_HB
    s_ref[...] = jnp.zeros_like(s_ref)
    o_ref[...] = jnp.zeros_like(o_ref)
    masks = [_tri_masks(CHUNK, rev) for rev in (False, True)]
    cum_mats = [m[2].astype(F32) for m in masks]
    r2 = lax.broadcasted_iota(jnp.int32, (CHUNK, 2 * CHUNK), 0)
    c2 = lax.broadcasted_iota(jnp.int32, (CHUNK, 2 * CHUNK), 1) % CHUNK
    incl2 = [c2 <= r2, c2 >= r2]
    strict2 = [c2 < r2, c2 > r2]
    cum_twice = [m.astype(F32) for m in incl2]
    tri = _TriConsts()
    unroll = GDN_UNROLL if nc % GDN_UNROLL == 0 else 1
    streams = [(u, d) for u in range(unroll) for d in range(2)]
    probs = [(si, hh) for si in range(len(streams)) for hh in range(GDN_HB)]
    pdir = [streams[si][1] for si, _ in probs]
    neg_rate_rows = -jnp.exp(rate_ref[0, 0])[:, :CHUNK]
    dtb_rows = rate_ref[0, 1][:, :CHUNK]
    neg_rate_lanes = -jnp.exp(ratel_ref[0, 0, 0:1, :])
    dtb_lanes = ratel_ref[0, 1, 0:1, :]

    def body(i, carry):
        dirs = range(len(streams))
        sdir = [d for _, d in streams]
        chunk = [i * unroll + u if d == 0 else _bwd_chunk(i * unroll + u, ncc, nc) for u, d in streams]
        rows = [_chunk_rows(c) for c in chunk]
        abc = [abc_ref[0, 0, rw, :] for rw in rows]
        abr = [abr_ref[0, 0, c] for c in chunk]
        incl = [incl2[d] for d in pdir]
        strict = [strict2[d] for d in pdir]
        g_cols = [neg_rate_lanes * _softplus(abc[si] + dtb_lanes) for si in dirs]
        g_rows = [neg_rate_rows * _softplus(abr[si] + dtb_rows) for si in dirs]
        gc_cols = [_cum_rows(cum_mats[sdir[si]], g_cols[si]) for si in dirs]
        gc_rows = [_cum_cols(g_rows[si], cum_twice[1 - sdir[si]]) for si in dirs]
        beta_cols = [jax.nn.sigmoid(abc[si]) for si in dirs]
        col = [d * GDN_HB + hh for d, (_, hh) in zip(pdir, probs)]
        gc_col = [jnp.broadcast_to(gc_cols[si][:, j:j + 1], (CHUNK, 128)) for (si, _), j in zip(probs, col)]
        gc_row = [gc_rows[si][j:j + 1, :] for (si, _), j in zip(probs, col)]
        beta_col = [beta_cols[si][:, na + j:na + j + 1] for (si, _), j in zip(probs, col)]
        g_last = [gc[CHUNK - 1:CHUNK] if d == 0 else gc[0:1] for d, gc in zip(pdir, gc_col)]
        decay = [jnp.where(m, jnp.exp(jnp.where(m, a - b, 0.0)), 0.0)
                 for m, a, b in zip(incl, gc_col, gc_row)]

        lanes = [slice(hh * GDN_DK, (hh + 1) * GDN_DK) for _, hh in probs]
        q = [q_ref[0, rows[si], ln].astype(F32) for (si, _), ln in zip(probs, lanes)]
        k = [k_ref[0, rows[si], ln].astype(F32) for (si, _), ln in zip(probs, lanes)]
        v = [v_ref[0, rows[si], ln].astype(F32) for (si, _), ln in zip(probs, lanes)]
        k_beta = _each(jnp.multiply, k, beta_col)
        v_beta = _each(jnp.multiply, v, beta_col)
        e_col = _each(jnp.exp, gc_col)
        kq = _each(_mm_nt, [jnp.concatenate([a, b], axis=0) for a, b in zip(k_beta, q)],
                   [jnp.concatenate([a, a], axis=0) for a in k])
        kk = [x[:CHUNK] for x in kq]
        qk = [x[CHUNK:] for x in kq]
        lm = [jnp.where(m, a * dc, 0.0) for m, a, dc in zip(strict, kk, decay)]
        rhs = [jnp.concatenate([vb, kb * e], axis=1) for vb, kb, e in zip(v_beta, k_beta, e_col)]
        uw = _unit_tri_solves(lm, rhs, tri)
        u = [x[:, :GDN_DV] for x in uw]
        w = [x[:, GDN_DV:] for x in uw]
        att = [jnp.where(m, a * dc, 0.0)[:, :CHUNK] for m, a, dc in zip(incl, qk, decay)]
        q_dec = _each(jnp.multiply, q, e_col)
        k_end = [kx * jnp.exp(gl - gc) for kx, gl, gc in zip(k, g_last, gc_col)]
        nstate = 2 * GDN_HB
        s = [s_ref[j] for j in range(nstate)]
        for pos in range(unroll):
            sel = [p for p, (si, _) in enumerate(probs) if streams[si][0] == pos]
            pick = lambda xs: [xs[p] for p in sel]
            wq = _each(_mm, [jnp.concatenate([a, b], axis=0) for a, b in zip(pick(w), pick(q_dec))], s)
            ws = [x[:CHUNK] for x in wq]
            o_inter = [x[CHUNK:] for x in wq]
            v_new = _each(jnp.subtract, pick(u), ws)
            upd = _each(_mm_tn, pick(k_end), v_new)
            o_intra = _each(_mm, pick(att), v_new)
            s = [sx * jnp.exp(gl) + up for sx, gl, up in zip(s, pick(g_last), upd)]
            for p, a, b in zip(sel, o_intra, o_inter):
                o_ref[0, rows[probs[p][0]], lanes[p]] += a + b
        for j in range(nstate):
            s_ref[j] = s[j]
        return carry

    lax.fori_loop(0, nc // unroll, body, 0)


def _gdn_call(qkvn, abc, abr, a_log, dt_bias, lc):
    b, l, _ = qkvn.shape
    nc, ncc = l // CHUNK, lc // CHUNK
    nhb = GDN_HEADS // GDN_HB
    wb = GDN_HB * GDN_DK
    na = 2 * GDN_HB
    rate = jnp.stack([a_log, dt_bias]).reshape(2, 2, nhb, GDN_HB).transpose(2, 0, 1, 3).reshape(nhb, 2, na)
    rate = jnp.pad(rate, ((0, 0), (0, 0), (0, 128 - na)))
    nr = _GDN_GATE_ROWS
    rate_rows = jnp.broadcast_to(rate[:, :, :nr, None], (nhb, 2, nr, 128))
    rate_lanes = jnp.broadcast_to(rate[:, :, None, :], (nhb, 2, 8, 128))
    return pl.pallas_call(
        functools.partial(_gdn_kernel, ncc=ncc, nc=nc),
        grid=(b, nhb),
        in_specs=[pl.BlockSpec((1, 2, nr, 128), lambda i, h: (h, 0, 0, 0)),
                  pl.BlockSpec((1, 2, 8, 128), lambda i, h: (h, 0, 0, 0)),
                  pl.BlockSpec((1, l, wb), lambda i, h: (i, 0, h)),
                  pl.BlockSpec((1, l, wb), lambda i, h: (i, 0, nhb + h)),
                  pl.BlockSpec((1, l, wb), lambda i, h: (i, 0, 2 * nhb + h)),
                  pl.BlockSpec((1, 1, l, 128), lambda i, h: (i, h, 0, 0)),
                  pl.BlockSpec((1, 1, nc, nr, CHUNK), lambda i, h: (i, h, 0, 0, 0))],
        out_specs=pl.BlockSpec((1, l, wb), lambda i, h: (i, 0, h)),
        out_shape=jax.ShapeDtypeStruct((b, l, GDN_HEADS * GDN_DV), F32),
        scratch_shapes=[pltpu.VMEM((2 * GDN_HB, GDN_DK, GDN_DV), F32)],
        compiler_params=_params(("parallel", "parallel")),
        name="gdn_scan",
    )(rate_rows, rate_lanes, qkvn, qkvn, qkvn, abc, abr)


def _gdn_gate_layouts(p_mla, lc):
    b, l, _ = p_mla.shape
    nhb = GDN_HEADS // GDN_HB
    na = 2 * GDN_HB
    o = MLA_Q_LORA + MLA_KV_LORA
    ab = p_mla[:, :, o + SM_A:o + SM_A + 32].reshape(b, l, 2, 2, nhb, GDN_HB)
    ab = ab.transpose(0, 4, 1, 2, 3, 5).reshape(b, nhb, l, 2 * na)
    abr = ab[..., :na].reshape(b, nhb, l // CHUNK, CHUNK, na).transpose(0, 1, 2, 4, 3)
    abr = jnp.pad(abr, ((0, 0), (0, 0), (0, 0), (0, _GDN_GATE_ROWS - na), (0, 0)))
    abc = jnp.pad(ab, ((0, 0), (0, 0), (0, 0), (0, 128 - 2 * na)))
    return abc, abr


def _rope_swap(x):
    lane = lax.broadcasted_iota(jnp.int32, x.shape, 1)
    n = x.shape[1]
    return jnp.where(lane % 32 < 16, pltpu.roll(x, n - 16, axis=1), pltpu.roll(x, 16, axis=1))


def _mla_prep_kernel(p_ref, cos_ref, sin_ref, qnw_ref, qup_ref, kvnw_ref, kvup_ref,
                     wqn_ref, wqr_ref, wkn_ref, wkr_ref, q_ref, k_ref, v_ref):
    p = p_ref[0]
    cos_t = cos_ref[...]
    sin_t = sin_ref[...]

    def rms(x, n):
        return x * lax.rsqrt(jnp.sum(x * x, axis=-1, keepdims=True) * (1.0 / n) + EPS)

    def rope(x):
        return x * cos_t + _rope_swap(x) * sin_t

    qd = p[:, :MLA_Q_LORA]
    kvd = p[:, MLA_Q_LORA:MLA_Q_LORA + MLA_KV_LORA]
    sm = p[:, MLA_Q_LORA + MLA_KV_LORA:]
    q = _mm(rms(qd, MLA_Q_LORA) * qnw_ref[...], qup_ref[...])
    kv = _mm(rms(kvd, MLA_KV_LORA) * kvnw_ref[...], kvup_ref[...])
    lane = lax.broadcasted_iota(jnp.int32, sm.shape, 1)
    kr = jnp.where(lane < MLA_ROPE, sm, 0.0)
    kr = rope(rms(kr, MLA_ROPE) * wkr_ref[...])
    for h in range(MLA_HEADS):
        base = h * 256
        qn = rms(q[:, base:base + 128], MLA_NOPE) * wqn_ref[...]
        q_ref[0, :, base:base + 128] = (qn * MLA_Q_PRESCALE).astype(q_ref.dtype)
        qr = rms(q[:, base + 128:base + 256], MLA_ROPE) * wqr_ref[...]
        q_ref[0, :, base + 128:base + 256] = (rope(qr) * MLA_Q_PRESCALE).astype(q_ref.dtype)
        kn = kv[:, h * 128:(h + 1) * 128]
        k_ref[0, :, base:base + 128] = (rms(kn, MLA_NOPE) * wkn_ref[...]).astype(k_ref.dtype)
        k_ref[0, :, base + 128:base + 256] = kr.astype(k_ref.dtype)
    v_ref[0] = kv[:, MLA_HEADS * MLA_NOPE:].T.astype(v_ref.dtype)


def _mla_prep_call(p_mla, cos_t, sin_t, qnw, qup, kvnw, kvup, wqn, wqr, wkn, wkr):
    b, l, gw = p_mla.shape
    tm = _pick(l, (768, 384, 256, 128))
    hq = MLA_HEADS * 256
    full = lambda shape: pl.BlockSpec(shape, lambda i, t: tuple(0 for _ in shape))
    return pl.pallas_call(
        _mla_prep_kernel,
        grid=(b, l // tm),
        in_specs=[pl.BlockSpec((1, tm, gw), lambda i, t: (i, t, 0)),
                  pl.BlockSpec((tm, 128), lambda i, t: (t, 0)),
                  pl.BlockSpec((tm, 128), lambda i, t: (t, 0)),
                  full((1, MLA_Q_LORA)), full((MLA_Q_LORA, hq)),
                  full((1, MLA_KV_LORA)), full((MLA_KV_LORA, hq)),
                  full((1, 128)), full((1, 128)), full((1, 128)), full((1, 128))],
        out_specs=[pl.BlockSpec((1, tm, hq), lambda i, t: (i, t, 0)),
                   pl.BlockSpec((1, tm, hq), lambda i, t: (i, t, 0)),
                   pl.BlockSpec((1, MLA_HEADS * MLA_DV, tm), lambda i, t: (i, 0, t))],
        out_shape=[jax.ShapeDtypeStruct((b, l, hq), BF16),
                   jax.ShapeDtypeStruct((b, l, hq), BF16),
                   jax.ShapeDtypeStruct((b, MLA_HEADS * MLA_DV, l), BF16)],
        compiler_params=_params(("parallel", "parallel")),
        name="mla_prep",
    )(p_mla, cos_t, sin_t, qnw, qup, kvnw, kvup, wqn, wqr, wkn, wkr)


def _mla_attn_kernel(q_ref, k_ref, vt_ref, o_ref, *, lc, tq, q_off):
    qi = pl.program_id(2) + q_off
    heads = range(MLA_HP)

    def attend(nk):
        def scores(h):
            return lax.dot_general(k_ref[0, :nk, h * 256:(h + 1) * 256], q_ref[0, :, h * 256:(h + 1) * 256],
                                   (((1,), (1,)), ((), ())), preferred_element_type=F32)

        ahead = MLA_HP
        st = {h: scores(h) for h in range(min(ahead, MLA_HP))}
        for h in heads:
            x = st.pop(h)
            e = jnp.exp2(x - jnp.max(x, axis=0, keepdims=True))
            den = jnp.sum(e, axis=0, keepdims=True)
            if h + ahead < MLA_HP:
                st[h + ahead] = scores(h + ahead)
            ot = jnp.dot(vt_ref[0, h * MLA_DV:(h + 1) * MLA_DV, :nk], e.astype(BF16),
                         preferred_element_type=F32)
            o_ref[0, :, h * MLA_DV:(h + 1) * MLA_DV] = (ot / den).T.astype(o_ref.dtype)

    if q_off == 0:
        @pl.when(qi * tq < lc)
        def _():
            attend(lc)

        @pl.when(qi * tq >= lc)
        def _():
            attend(k_ref.shape[1])
    else:
        attend(k_ref.shape[1])


def _mla_attn_call(q, k, vt, lc, with_ctx):
    b, l, _ = q.shape
    tq = _pick(lc, (256, 128))
    q_off = 0 if with_ctx else lc // tq
    nq = l // tq - q_off
    return pl.pallas_call(
        functools.partial(_mla_attn_kernel, lc=lc, tq=tq, q_off=q_off),
        grid=(b, MLA_HEADS // MLA_HP, nq),
        in_specs=[pl.BlockSpec((1, tq, MLA_HP * 256), lambda i, h, t: (i, t + q_off, h)),
                  pl.BlockSpec((1, l, MLA_HP * 256), lambda i, h, t: (i, 0, h)),
                  pl.BlockSpec((1, MLA_HP * MLA_DV, l), lambda i, h, t: (i, h, 0))],
        out_specs=pl.BlockSpec((1, tq, MLA_HP * MLA_DV), lambda i, h, t: (i, t, h)),
        out_shape=jax.ShapeDtypeStruct((b, nq * tq, MLA_HEADS * MLA_DV), F32),
        compiler_params=_params(("parallel", "parallel", "arbitrary")),
        name="mla_attn",
    )(q, k, vt)


def _merge_kernel(h_ref, oa_ref, ob_ref, oc_ref, za_ref, zb_ref, zc_ref, g0_ref, g1_ref, g2_ref,
                  modx_ref, modc_ref, wa_ref, wb_ref, wbr_ref, wout_ref, o_ref, *, lc, tm, t_off):
    t = pl.program_id(1) + t_off

    def head_norm(o, width):
        parts = []
        for h in range(BRANCH_W // width):
            x = o[:, h * width:(h + 1) * width]
            parts.append(x * lax.rsqrt(jnp.mean(x * x, axis=-1, keepdims=True) + EPS))
        return jnp.concatenate(parts, axis=-1)

    ya = head_norm(oa_ref[0], GLA_DV) * wa_ref[...] * _silu(za_ref[0].astype(F32))
    yb = head_norm(ob_ref[0], GDN_DV) * wb_ref[...] * _silu(zb_ref[0].astype(F32))
    yc = oc_ref[0] * _silu(zc_ref[0].astype(F32))
    acc = _sigmoid(g0_ref[0].astype(F32)) * _mm(ya, wbr_ref[0])
    acc = acc + _sigmoid(g1_ref[0].astype(F32)) * _mm(yb, wbr_ref[1])
    acc = acc + _sigmoid(g2_ref[0].astype(F32)) * _mm(yc, wbr_ref[2])
    d = D_MODEL
    gate = _row_select(t, tm, lc, modc_ref[:, 2 * d:], modx_ref[0][:, 2 * d:])
    o_ref[0] = h_ref[0] + gate * _mm(acc, wout_ref[...])


def _merge_call(h, oa, ob, oc, p_zg, modx, modc, wa, wb, w_branch, w_out, lc, with_ctx):
    b, l, d = h.shape
    tm = _pick(lc, (256, 128))
    t_off = 0 if with_ctx else lc // tm
    nt = l // tm - t_off
    oc_off = 0 if oc.shape[1] == l else lc // tm
    row = lambda i, t: (i, t + t_off, 0)
    zg = lambda n: pl.BlockSpec((1, tm, d), lambda i, t: (i, t + t_off, n))
    return pl.pallas_call(
        functools.partial(_merge_kernel, lc=lc, tm=tm, t_off=t_off),
        grid=(b, nt),
        in_specs=[pl.BlockSpec((1, tm, d), row), pl.BlockSpec((1, tm, d), row), pl.BlockSpec((1, tm, d), row),
                  pl.BlockSpec((1, tm, d), lambda i, t: (i, t + t_off - oc_off, 0)),
                  zg(0), zg(1), zg(2), zg(3), zg(4), zg(5),
                  pl.BlockSpec((1, 1, 3 * d), lambda i, t: (i, 0, 0)),
                  pl.BlockSpec((1, 3 * d), lambda i, t: (0, 0)),
                  pl.BlockSpec((1, d), lambda i, t: (0, 0)),
                  pl.BlockSpec((1, d), lambda i, t: (0, 0)),
                  pl.BlockSpec((3, d, d), lambda i, t: (0, 0, 0)),
                  pl.BlockSpec((d, d), lambda i, t: (0, 0))],
        out_specs=pl.BlockSpec((1, tm, d), lambda i, t: (i, t, 0)),
        out_shape=jax.ShapeDtypeStruct((b, nt * tm, d), F32),
        compiler_params=_params(("parallel", "parallel")),
        name="merge",
    )(h, oa, ob, oc, p_zg, p_zg, p_zg, p_zg, p_zg, p_zg, modx, modc, wa, wb, w_branch, w_out)


def _cols(w, *names):
    return jnp.concatenate([w[:, _IN_OFF[n]:_IN_OFF[n] + _IN_LEN[n]] for n in names], axis=1)


def _rope_tables(t_lat, lc):
    rows = t_lat // GRID_W
    row = jnp.repeat(jnp.arange(rows, dtype=F32), GRID_W)
    col = jnp.tile(jnp.arange(GRID_W, dtype=F32), rows)
    inv_freq = jnp.power(ROPE_BASE, -jnp.arange(ROPE_FREQ, dtype=F32) / ROPE_FREQ)
    ar = row[:, None] * inv_freq
    ac = col[:, None] * inv_freq
    zeros = jnp.zeros((t_lat, 128 - MLA_ROPE), F32)
    cos_t = jnp.concatenate([jnp.cos(ar), jnp.cos(ar), jnp.cos(ac), jnp.cos(ac), zeros], axis=1)
    sin_t = jnp.concatenate([-jnp.sin(ar), jnp.sin(ar), -jnp.sin(ac), jnp.sin(ac), zeros], axis=1)
    cos_c = jnp.concatenate([jnp.ones((lc, MLA_ROPE), F32), jnp.zeros((lc, 128 - MLA_ROPE), F32)], axis=1)
    return (jnp.concatenate([cos_c, cos_t], axis=0),
            jnp.concatenate([jnp.zeros((lc, 128), F32), sin_t], axis=0))


def _pad_lanes(w, n=128):
    return jnp.pad(w, (0, n - w.shape[0])).reshape(1, n)


def kernel(x, c, ctx, c_ctx, norm_w, ada_w, ada_b, w_in, gla_gate_w2, gla_gate_b, gla_norm_w, gdn_conv_w,
           gdn_a_log, gdn_dt_bias, gdn_norm_w, mla_q_norm_w, mla_q_up, mla_kv_norm_w, mla_kv_up,
           mla_qn_nope, mla_qn_rope, mla_kn_nope, mla_kn_rope, w_branch, w_out):
    b, t_lat, d = x.shape
    lc = ctx.shape[1]
    l = lc + t_lat
    depth = w_in.shape[0]
    assert d == D_MODEL and lc % 128 == 0 and t_lat % 128 == 0

    h = jnp.concatenate([ctx, x], axis=1)
    rows = ((b + 1 + 7) // 8) * 8
    c_all = jnp.concatenate([c, c_ctx[None, :], jnp.zeros((rows - b - 1, d), F32)], axis=0)
    mod = _ada_call(c_all, ada_w, ada_b)
    cos_t, sin_t = _rope_tables(t_lat, lc)

    for li in range(depth):
        last = li == depth - 1
        modx = mod[li, :b].reshape(b, 1, 3 * d)
        modc = mod[li, b:b + 1]
        w = w_in[li]
        w_mla = _cols(w, 'mla_q_down', 'mla_kv_down', 'mla_k_rope', 'gla_gate_lr', 'gdn_a', 'gdn_b').astype(BF16)
        w_gla = _cols(w, 'gla_q', 'gla_k', 'gla_v').astype(BF16)
        w_gdn = _cols(w, 'gdn_qkv').astype(BF16)
        w_zg = _cols(w, 'gla_z', 'gdn_z', 'mla_z', 'merge_gate').astype(BF16)

        hn = _prenorm_call(h, modx, modc, norm_w[li], lc).reshape(b * l, d)
        p_mla = _proj_call(hn, w_mla, F32, "proj_mla").reshape(b, l, -1)
        p_gla = _proj_call(hn, w_gla, BF16, "proj_gla").reshape(b, l, -1)
        p_gdn = _proj_call(hn, w_gdn, F32, "proj_gdn").reshape(b, l, -1)
        p_zg = _proj_call(hn, w_zg, BF16, "proj_zg").reshape(b, l, -1)

        w2e = jnp.stack([jnp.pad(gla_gate_w2[li, dd], ((SM_LR + dd * GLA_RANK, 128 - SM_LR - (dd + 1) * GLA_RANK),
                                                      (0, 0))) for dd in range(2)])
        oa = _gla_call(p_gla, p_mla, w2e, gla_gate_b[li].reshape(2, 1, -1), lc)

        qkvn = _gdn_prep_call(p_gdn, gdn_conv_w[li], lc)
        abc, abr = _gdn_gate_layouts(p_mla, lc)
        ob = _gdn_call(qkvn, abc, abr, gdn_a_log[li], gdn_dt_bias[li], lc)

        qup = mla_q_up[li].reshape(MLA_Q_LORA, MLA_HEADS, MLA_NOPE + MLA_ROPE)
        qup = jnp.pad(qup, ((0, 0), (0, 0), (0, 256 - MLA_NOPE - MLA_ROPE))).reshape(MLA_Q_LORA, -1).astype(BF16)
        kvup = mla_kv_up[li].reshape(MLA_KV_LORA, MLA_HEADS, MLA_NOPE + MLA_DV)
        kvup = jnp.concatenate([kvup[:, :, :MLA_NOPE].reshape(MLA_KV_LORA, -1),
                                kvup[:, :, MLA_NOPE:].reshape(MLA_KV_LORA, -1)], axis=1).astype(BF16)
        qm, km, vm = _mla_prep_call(
            p_mla, cos_t, sin_t, mla_q_norm_w[li].reshape(1, -1), qup, mla_kv_norm_w[li].reshape(1, -1), kvup,
            mla_qn_nope[li].reshape(1, -1), _pad_lanes(mla_qn_rope[li]),
            mla_kn_nope[li].reshape(1, -1), _pad_lanes(mla_kn_rope[li]))
        oc = _mla_attn_call(qm, km, vm, lc, with_ctx=not last)

        h = _merge_call(h, oa, ob, oc, p_zg, modx, modc,
                        jnp.tile(gla_norm_w[li], GLA_HEADS).reshape(1, -1),
                        jnp.tile(gdn_norm_w[li], GDN_HEADS).reshape(1, -1),
                        w_branch[li].astype(BF16), w_out[li].astype(BF16), lc, with_ctx=not last)
    return h
```

```python
import functools
import math

import numpy as np
import jax
import jax.numpy as jnp
from jax import lax
from jax.experimental import pallas as pl
from jax.experimental.pallas import tpu as pltpu

F32 = jnp.float32
BF16 = jnp.bfloat16

D_MODEL = 1024
EPS = 1e-6
GRID_W = 64
CHUNK = 64

GLA_HEADS, GLA_DK, GLA_DV, GLA_RANK, GLA_NORMALIZER = 4, 128, 256, 16, 16.0
GDN_HEADS, GDN_DK, GDN_DV, GDN_CONV = 8, 128, 128, 5
GLA_HB = 4
GDN_HB = 8
_GDN_GATE_ROWS = max(8, 2 * GDN_HB)
GDN_PREP_W = 128
GDN_UNROLL = 1
MLA_HEADS, MLA_Q_LORA, MLA_KV_LORA, MLA_NOPE, MLA_ROPE, MLA_DV = 8, 384, 256, 128, 64, 128
MLA_SCALE = (MLA_NOPE + MLA_ROPE) ** -0.5
MLA_Q_PRESCALE = MLA_SCALE * math.log2(math.e)
MLA_HP = 8
ROPE_FREQ = MLA_ROPE // 4
ROPE_BASE = 10000.0
BRANCH_W = 1024

_IN_SIZES = (512, 512, 1024, 32, 1024, 3072, 16, 16, 1024, 384, 256, 64, 1024, 3072)
_IN_NAMES = ('gla_q', 'gla_k', 'gla_v', 'gla_gate_lr', 'gla_z', 'gdn_qkv', 'gdn_a', 'gdn_b', 'gdn_z',
             'mla_q_down', 'mla_kv_down', 'mla_k_rope', 'mla_z', 'merge_gate')
_IN_OFF = dict(zip(_IN_NAMES, np.concatenate([[0], np.cumsum(_IN_SIZES)[:-1]]).tolist()))
_IN_LEN = dict(zip(_IN_NAMES, _IN_SIZES))

SM_ROPE, SM_LR, SM_A, SM_B = 0, 64, 96, 112
SMALL_BLOCK = (MLA_Q_LORA + MLA_KV_LORA) // 128

VMEM_LIMIT = 58 * 1024 * 1024


def _params(sem):
    return pltpu.CompilerParams(dimension_semantics=sem, vmem_limit_bytes=VMEM_LIMIT)


def _pick(n, cands):
    for c in cands:
        if n % c == 0:
            return c
    raise ValueError(f"no tile for {n} in {cands}")


def _mm(a, b):
    return jnp.dot(a.astype(BF16), b.astype(BF16), preferred_element_type=F32)


def _mm_nt(a, b):
    return lax.dot_general(a.astype(BF16), b.astype(BF16), (((1,), (1,)), ((), ())),
                           preferred_element_type=F32)


def _mm_tn(a, b):
    return lax.dot_general(a.astype(BF16), b.astype(BF16), (((0,), (0,)), ((), ())),
                           preferred_element_type=F32)


def _split(a):
    hi = a.astype(BF16)
    lo = (a - hi.astype(F32)).astype(BF16)
    return hi, lo


def _mm3(a, b):
    ah, al = _split(a)
    bh, bl = _split(b)
    d = functools.partial(jnp.dot, preferred_element_type=F32)
    return d(ah, bh) + (d(ah, bl) + d(al, bh))


_TRI_BLOCK = 2


def _cum_rows(cum, x):
    cb = cum.astype(BF16)
    d = functools.partial(jnp.dot, preferred_element_type=F32)
    hi, lo = _split(x)
    return d(cb, hi) + d(cb, lo)


def _cum_cols(x, cum):
    cb = cum.astype(BF16)
    d = functools.partial(jnp.dot, preferred_element_type=F32)
    hi, lo = _split(x)
    return d(hi, cb) + d(lo, cb)


def _softplus(x):
    return jnp.maximum(x, 0.0) + jnp.log(1.0 + jnp.exp(-jnp.abs(x)))


def _log_sigmoid(x):
    return jnp.minimum(x, 0.0) - jnp.log(1.0 + jnp.exp(-jnp.abs(x)))


_sigmoid = jax.nn.sigmoid


def _silu(x):
    return x * _sigmoid(x)


def _tri_masks(n, reverse):
    r = lax.broadcasted_iota(jnp.int32, (n, n), 0)
    c = lax.broadcasted_iota(jnp.int32, (n, n), 1)
    incl = (c >= r) if reverse else (c <= r)
    strict = (c > r) if reverse else (c < r)
    return r, c, incl, strict


def _ada_kernel(c_ref, w_ref, b_ref, o_ref):
    o_ref[0] = _mm3(_silu(c_ref[...]), w_ref[0]) + b_ref[0]


def _ada_call(c_all, ada_w, ada_b):
    nl, d, n3 = ada_w.shape
    r = c_all.shape[0]
    tn = 1024
    return pl.pallas_call(
        _ada_kernel,
        grid=(nl, n3 // tn),
        in_specs=[pl.BlockSpec((r, d), lambda l, j: (0, 0)),
                  pl.BlockSpec((1, d, tn), lambda l, j: (l, 0, j)),
                  pl.BlockSpec((1, 1, tn), lambda l, j: (l, 0, j))],
        out_specs=pl.BlockSpec((1, r, tn), lambda l, j: (l, 0, j)),
        out_shape=jax.ShapeDtypeStruct((nl, r, n3), F32),
        compiler_params=_params(("arbitrary", "arbitrary")),
        name="ada_rows",
    )(c_all, ada_w, ada_b.reshape(nl, 1, n3))


def _row_select(t, tm, lc, ctx_row, lat_row):
    rows = t * tm + lax.broadcasted_iota(jnp.int32, (tm, 1), 0)
    return jnp.where(rows < lc, ctx_row, lat_row)


def _prenorm_kernel(h_ref, modx_ref, modc_ref, nw_ref, o_ref, *, lc, tm):
    t = pl.program_id(1)
    x = h_ref[0]
    y = x * lax.rsqrt(jnp.mean(x * x, axis=-1, keepdims=True) + EPS) * nw_ref[...]
    mx = modx_ref[0]
    mc = modc_ref[...]
    d = D_MODEL
    shift = _row_select(t, tm, lc, mc[:, :d], mx[:, :d])
    scale = _row_select(t, tm, lc, mc[:, d:2 * d], mx[:, d:2 * d])
    o_ref[0] = (y * (1.0 + scale) + shift).astype(o_ref.dtype)


def _prenorm_call(h, modx, modc, norm_w, lc):
    b, l, d = h.shape
    tm = _pick(l, (768, 512, 384, 256, 128))
    return pl.pallas_call(
        functools.partial(_prenorm_kernel, lc=lc, tm=tm),
        grid=(b, l // tm),
        in_specs=[pl.BlockSpec((1, tm, d), lambda i, t: (i, t, 0)),
                  pl.BlockSpec((1, 1, 3 * d), lambda i, t: (i, 0, 0)),
                  pl.BlockSpec((1, 3 * d), lambda i, t: (0, 0)),
                  pl.BlockSpec((1, d), lambda i, t: (0, 0))],
        out_specs=pl.BlockSpec((1, tm, d), lambda i, t: (i, t, 0)),
        out_shape=jax.ShapeDtypeStruct((b, l, d), BF16),
        compiler_params=_params(("parallel", "parallel")),
        name="prenorm",
    )(h, modx, modc, norm_w.reshape(1, d))


def _proj_kernel(x_ref, w_ref, o_ref):
    o_ref[...] = jnp.dot(x_ref[...], w_ref[...], preferred_element_type=F32).astype(o_ref.dtype)


def _proj_call(x, w, out_dtype, name):
    m, k = x.shape
    n = w.shape[1]
    tm = _pick(m, (2048, 1024, 768, 512, 384, 256, 128))
    tn = _pick(n, (1536, 1024, 768, 512, 256, 128))
    return pl.pallas_call(
        _proj_kernel,
        grid=(m // tm, n // tn),
        in_specs=[pl.BlockSpec((tm, k), lambda i, j: (i, 0)),
                  pl.BlockSpec((k, tn), lambda i, j: (0, j))],
        out_specs=pl.BlockSpec((tm, tn), lambda i, j: (i, j)),
        out_shape=jax.ShapeDtypeStruct((m, n), out_dtype),
        compiler_params=_params(("parallel", "parallel")),
        name=name,
    )(x, w)


def _bwd_chunk(i, ncc, nc):
    return jnp.where(i < ncc, ncc - 1 - i, ncc + nc - 1 - i)


def _chunk_rows(c):
    return pl.ds(pl.multiple_of(c * CHUNK, CHUNK), CHUNK)


def _gla_kernel(q_ref, k_ref, v_ref, sm_ref, w2_ref, gb_ref, o_ref, la_ref, st_ref, *, ncc, nc):
    sm = sm_ref[0]
    for d in range(2):
        logit = _mm3(sm, w2_ref[d]) + gb_ref[d]
        la_ref[d] = _log_sigmoid(logit) * (1.0 / GLA_NORMALIZER)
    st_ref[...] = jnp.zeros_like(st_ref)
    o_ref[...] = jnp.zeros_like(o_ref)
    masks = [_tri_masks(CHUNK, rev) for rev in (False, True)]
    cum_mats = [m[2].astype(F32) for m in masks]
    probs = [(d, hh) for d in range(2) for hh in range(GLA_HB)]

    kl = [slice(hh * GLA_DK, (hh + 1) * GLA_DK) for _, hh in probs]
    vl = [slice(hh * GLA_DV, (hh + 1) * GLA_DV) for _, hh in probs]
    unroll = 2 if nc % 2 == 0 else 1

    def body(i, carry):
        steps = range(unroll)
        rows = [[_chunk_rows(i * unroll + u), _chunk_rows(_bwd_chunk(i * unroll + u, ncc, nc))] for u in steps]
        bcum_d = [[_cum_rows(cum_mats[d], la_ref[d, rows[u][d], :]) for d in range(2)] for u in steps]
        bcum = [[bcum_d[u][d][:, ln] for (d, _), ln in zip(probs, kl)] for u in steps]
        b_last = [[bc[CHUNK - 1:CHUNK] if d == 0 else bc[0:1] for (d, _), bc in zip(probs, bcum[u])] for u in steps]
        q = [[q_ref[0, rows[u][d], ln].astype(F32) * GLA_DK ** -0.5 for (d, _), ln in zip(probs, kl)] for u in steps]
        k = [[k_ref[0, rows[u][d], ln].astype(F32) for (d, _), ln in zip(probs, kl)] for u in steps]
        v = [[v_ref[0, rows[u][d], ln].astype(F32) for (d, _), ln in zip(probs, vl)] for u in steps]
        q_dec = [[a * jnp.exp(bc) for a, bc in zip(q[u], bcum[u])] for u in steps]
        k_inv = [[a * jnp.exp(-bc) for a, bc in zip(k[u], bcum[u])] for u in steps]
        k_end = [[a * jnp.exp(bl - bc) for a, bl, bc in zip(k[u], b_last[u], bcum[u])] for u in steps]
        st = [st_ref[j] for j in range(len(probs))]
        qk = [_each(_mm_nt, q_dec[u], k_inv[u]) for u in steps]
        o_inter = [_each(_mm_nt, q_dec[0], st)]
        upd = [_each(_mm_tn, v[u], k_end[u]) for u in steps]
        att = [[jnp.where(masks[d][2], a, 0.0) for (d, _), a in zip(probs, qk[u])] for u in steps]
        o_intra = [_each(_mm, att[u], v[u]) for u in steps]
        for u in steps:
            st = [sx * jnp.exp(bl) + up for sx, bl, up in zip(st, b_last[u], upd[u])]
            if u + 1 < unroll:
                o_inter.append(_each(_mm_nt, q_dec[u + 1], st))
        for j, sx in enumerate(st):
            st_ref[j] = sx
        for u in steps:
            for (d, _), ln, a, b in zip(probs, vl, o_intra[u], o_inter[u]):
                o_ref[0, rows[u][d], ln] += a + b
        return carry

    lax.fori_loop(0, nc // unroll, body, 0)


def _gla_call(p_gla, p_mla, w2e, gate_b, lc):
    b, l, _ = p_gla.shape
    nc, ncc = l // CHUNK, lc // CHUNK
    nhb = GLA_HEADS // GLA_HB
    wk, wv = GLA_HB * GLA_DK, GLA_HB * GLA_DV
    return pl.pallas_call(
        functools.partial(_gla_kernel, ncc=ncc, nc=nc),
        grid=(b, nhb),
        in_specs=[pl.BlockSpec((1, l, wk), lambda i, h: (i, 0, h)),
                  pl.BlockSpec((1, l, wk), lambda i, h: (i, 0, nhb + h)),
                  pl.BlockSpec((1, l, wv), lambda i, h: (i, 0, nhb + h)),
                  pl.BlockSpec((1, l, 128), lambda i, h: (i, 0, SMALL_BLOCK)),
                  pl.BlockSpec((2, 128, wk), lambda i, h: (0, 0, h)),
                  pl.BlockSpec((2, 1, wk), lambda i, h: (0, 0, h))],
        out_specs=pl.BlockSpec((1, l, wv), lambda i, h: (i, 0, h)),
        out_shape=jax.ShapeDtypeStruct((b, l, GLA_HEADS * GLA_DV), F32),
        scratch_shapes=[pltpu.VMEM((2, l, wk), F32), pltpu.VMEM((2 * GLA_HB, GLA_DV, GLA_DK), F32)],
        compiler_params=_params(("parallel", "parallel")),
        name="gla_scan",
    )(p_gla, p_gla, p_gla, p_mla, w2e, gate_b)


def _gdn_prep_kernel(x_ref, w_ref, o_ref, *, lc, l):
    half = GDN_CONV // 2
    nblk_head = GDN_HEADS * GDN_DK // 128
    grp, halo = 16, 8
    zeros = jnp.zeros((halo, 128), F32)
    slices = [(bb, g) for bb in range(x_ref.shape[0]) for g in range(GDN_PREP_W // 128)]
    for bb, g in slices:
        j = pl.program_id(1) * (GDN_PREP_W // 128) + g
        ln = slice(g * 128, (g + 1) * 128)
        w = w_ref[:, ln]

        def finish(acc):
            y = _silu(acc)
            inv = lax.rsqrt(jnp.sum(y * y, axis=-1, keepdims=True) + EPS)
            fac = jnp.where(j < nblk_head, inv * GDN_DK ** -0.5, jnp.where(j < 2 * nblk_head, inv, 1.0))
            return (y * fac).astype(o_ref.dtype)

        acc = None
        for s in range(-half, half + 1):
            term = x_ref[bb, grp + s:l - grp + s, ln].astype(F32) * w[s + half:s + half + 1]
            acc = term if acc is None else acc + term
        o_ref[bb, grp:l - grp, ln] = finish(acc)

        for r0 in sorted({0, lc - grp, lc, l - grp}):
            lo, hi = max(r0 - halo, 0), min(r0 + grp + halo, l)
            slab = x_ref[bb, lo:hi, ln].astype(F32)
            if r0 - halo < 0:
                slab = jnp.concatenate([zeros, slab], axis=0)
            if r0 + grp + halo > l:
                slab = jnp.concatenate([slab, zeros], axis=0)
            n = grp + 2 * halo
            t = r0 + lax.broadcasted_iota(jnp.int32, (grp, 128), 0)
            start = jnp.where(t < lc, 0, lc)
            end = jnp.where(t < lc, lc, l)
            acc = slab[halo:halo + grp] * w[half:half + 1]
            for s in range(-half, half + 1):
                if s != 0:
                    xs = pltpu.roll(slab, (-s) % n, axis=0)[halo:halo + grp]
                    ok = (t + s >= start) if s < 0 else (t + s < end)
                    acc = acc + jnp.where(ok, xs, 0.0) * w[s + half:s + half + 1]
            o_ref[bb, r0:r0 + grp, ln] = finish(acc)


def _gdn_prep_call(p_gdn, conv_w, lc):
    b, l, n = p_gdn.shape
    wd = GDN_PREP_W
    nbat = _pick(b, (4, 2, 1))
    return pl.pallas_call(
        functools.partial(_gdn_prep_kernel, lc=lc, l=l),
        grid=(b // nbat, n // wd),
        in_specs=[pl.BlockSpec((nbat, l, wd), lambda i, j: (i, 0, j)),
                  pl.BlockSpec((GDN_CONV, wd), lambda i, j: (0, j))],
        out_specs=pl.BlockSpec((nbat, l, wd), lambda i, j: (i, 0, j)),
        out_shape=jax.ShapeDtypeStruct((b, l, n), BF16),
        compiler_params=_params(("parallel", "parallel")),
        name="gdn_prep",
    )(p_gdn, conv_w)


def _each(fn, *lists):
    return [fn(*args) for args in zip(*lists)]


class _TriConsts:
    def __init__(self):
        r = lax.broadcasted_iota(jnp.int32, (CHUNK, 2 * CHUNK), 0)
        lane = lax.broadcasted_iota(jnp.int32, (CHUNK, 2 * CHUNK), 1)
        c = lane % CHUNK
        same = lambda s: (r // s) == (c // s)
        nb = _TRI_BLOCK
        self.left = lane < CHUNK
        self.diag_blocks = same(nb)[:, :CHUNK]
        self.eye = (r == c).astype(F32)[:, :CHUNK]
        right = jnp.logical_not(self.left)
        self.first_is_eye = self.left & (r == c)
        self.first_is_lm = right & jnp.logical_not(same(nb))
        sizes = [nb * 2 ** j for j in range(int(math.log2(CHUNK // nb)))]
        self.level = [right & same(2 * s) & jnp.logical_not(same(s)) for s in sizes]


def _unit_tri_solves(lms, rhs, tc):
    t = [tc.eye - jnp.where(tc.diag_blocks, lm[:, :CHUNK], 0.0) for lm in lms]
    first =[jnp.where(tc.first_is_eye, 1.0, jnp.where(tc.first_is_lm, lm, 0.0)) for lm in lms]
    x = _each(_mm, t, first)
    zero = jnp.zeros((CHUNK, 2 * CHUNK), F32)
    for mask in tc.level:
        m = [jnp.where(mask, a, 0.0) for a in x]
        x = _each(lambda a, mj: a - _mm(mj, jnp.concatenate([zero, a], axis=0)), x, m)
    t_full = [jnp.where(tc.left, a, 0.0) for a in x]
    zero_rhs = jnp.zeros_like(rhs[0])
    return _each(lambda a, b: _mm(a, jnp.concatenate([b, zero_rhs], axis=0)), t_full, rhs)


def _gdn_kernel(rate_ref, ratel_ref, q_ref, k_ref, v_ref, abc_ref, abr_ref, o_ref, s_ref, *, ncc, nc):
    na = 2 * GDN_HB
    s_ref[...] = jnp.zeros_like(s_ref)
    o_ref[...] = jnp.zeros_like(o_ref)
    masks = [_tri_masks(CHUNK, rev) for rev in (False, True)]
    cum_mats = [m[2].astype(F32) for m in masks]
    r2 = lax.broadcasted_iota(jnp.int32, (CHUNK, 2 * CHUNK), 0)
    c2 = lax.broadcasted_iota(jnp.int32, (CHUNK, 2 * CHUNK), 1) % CHUNK
    incl2 = [c2 <= r2, c2 >= r2]
    strict2 = [c2 < r2, c2 > r2]
    cum_twice = [m.astype(F32) for m in incl2]
    tri = _TriConsts()
    unroll = GDN_UNROLL if nc % GDN_UNROLL == 0 else 1
    streams = [(u, d) for u in range(unroll) for d in range(2)]
    probs = [(si, hh) for si in range(len(streams)) for hh in range(GDN_HB)]
    pdir = [streams[si][1] for si, _ in probs]
    neg_rate_rows = -jnp.exp(rate_ref[0, 0])[:, :CHUNK]
    dtb_rows = rate_ref[0, 1][:, :CHUNK]
    neg_rate_lanes = -jnp.exp(ratel_ref[0, 0, 0:1, :])
    dtb_lanes = ratel_ref[0, 1, 0:1, :]

    def body(i, carry):
        dirs = range(len(streams))
        sdir = [d for _, d in streams]
        chunk = [i * unroll + u if d == 0 else _bwd_chunk(i * unroll + u, ncc, nc) for u, d in streams]
        rows = [_chunk_rows(c) for c in chunk]
        abc = [abc_ref[0, 0, rw, :] for rw in rows]
        abr = [abr_ref[0, 0, c] for c in chunk]
        incl = [incl2[d] for d in pdir]
        strict = [strict2[d] for d in pdir]
        g_cols = [neg_rate_lanes * _softplus(abc[si] + dtb_lanes) for si in dirs]
        g_rows = [neg_rate_rows * _softplus(abr[si] + dtb_rows) for si in dirs]
        gc_cols = [_cum_rows(cum_mats[sdir[si]], g_cols[si]) for si in dirs]
        gc_rows = [_cum_cols(g_rows[si], cum_twice[1 - sdir[si]]) for si in dirs]
        beta_cols = [jax.nn.sigmoid(abc[si]) for si in dirs]
        col = [d * GDN_HB + hh for d, (_, hh) in zip(pdir, probs)]
        gc_col = [jnp.broadcast_to(gc_cols[si][:, j:j + 1], (CHUNK, 128)) for (si, _), j in zip(probs, col)]
        gc_row = [gc_rows[si][j:j + 1, :] for (si, _), j in zip(probs, col)]
        beta_col = [beta_cols[si][:, na + j:na + j + 1] for (si, _), j in zip(probs, col)]
        g_last = [gc[CHUNK - 1:CHUNK] if d == 0 else gc[0:1] for d, gc in zip(pdir, gc_col)]
        decay = [jnp.where(m, jnp.exp(jnp.where(m, a - b, 0.0)), 0.0)
                 for m, a, b in zip(incl, gc_col, gc_row)]

        lanes = [slice(hh * GDN_DK, (hh + 1) * GDN_DK) for _, hh in probs]
        q = [q_ref[0, rows[si], ln].astype(F32) for (si, _), ln in zip(probs, lanes)]
        k = [k_ref[0, rows[si], ln].astype(F32) for (si, _), ln in zip(probs, lanes)]
        v = [v_ref[0, rows[si], ln].astype(F32) for (si, _), ln in zip(probs, lanes)]
        k_beta = _each(jnp.multiply, k, beta_col)
        v_beta = _each(jnp.multiply, v, beta_col)
        e_col = _each(jnp.exp, gc_col)
        kq = _each(_mm_nt, [jnp.concatenate([a, b], axis=0) for a, b in zip(k_beta, q)],
                   [jnp.concatenate([a, a], axis=0) for a in k])
        kk = [x[:CHUNK] for x in kq]
        qk = [x[CHUNK:] for x in kq]
        lm = [jnp.where(m, a * dc, 0.0) for m, a, dc in zip(strict, kk, decay)]
        rhs = [jnp.concatenate([vb, kb * e], axis=1) for vb, kb, e in zip(v_beta, k_beta, e_col)]
        uw = _unit_tri_solves(lm, rhs, tri)
        u = [x[:, :GDN_DV] for x in uw]
        w = [x[:, GDN_DV:] for x in uw]
        att = [jnp.where(m, a * dc, 0.0)[:, :CHUNK] for m, a, dc in zip(incl, qk, decay)]
        q_dec = _each(jnp.multiply, q, e_col)
        k_end = [kx * jnp.exp(gl - gc) for kx, gl, gc in zip(k, g_last, gc_col)]
        nstate = 2 * GDN_HB
        s = [s_ref[j] for j in range(nstate)]
        for pos in range(unroll):
            sel = [p for p, (si, _) in enumerate(probs) if streams[si][0] == pos]
            pick = lambda xs: [xs[p] for p in sel]
            wq = _each(_mm, [jnp.concatenate([a, b], axis=0) for a, b in zip(pick(w), pick(q_dec))], s)
            ws = [x[:CHUNK] for x in wq]
            o_inter = [x[CHUNK:] for x in wq]
            v_new = _each(jnp.subtract, pick(u), ws)
            upd = _each(_mm_tn, pick(k_end), v_new)
            o_intra = _each(_mm, pick(att), v_new)
            s = [sx * jnp.exp(gl) + up for sx, gl, up in zip(s, pick(g_last), upd)]
            for p, a, b in zip(sel, o_intra, o_inter):
                o_ref[0, rows[probs[p][0]], lanes[p]] += a + b
        for j in range(nstate):
            s_ref[j] = s[j]
        return carry

    lax.fori_loop(0, nc // unroll, body, 0)


def _gdn_call(qkvn, abc, abr, a_log, dt_bias, lc):
    b, l, _ = qkvn.shape
    nc, ncc = l // CHUNK, lc // CHUNK
    nhb = GDN_HEADS // GDN_HB
    wb = GDN_HB * GDN_DK
    na = 2 * GDN_HB
    rate = jnp.stack([a_log, dt_bias]).reshape(2, 2, nhb, GDN_HB).transpose(2, 0, 1, 3).reshape(nhb, 2, na)
    rate = jnp.pad(rate, ((0, 0), (0, 0), (0, 128 - na)))
    nr = _GDN_GATE_ROWS
    rate_rows = jnp.broadcast_to(rate[:, :, :nr, None], (nhb, 2, nr, 128))
    rate_lanes = jnp.broadcast_to(rate[:, :, None, :], (nhb, 2, 8, 128))
    return pl.pallas_call(
        functools.partial(_gdn_kernel, ncc=ncc, nc=nc),
        grid=(b, nhb),
        in_specs=[pl.BlockSpec((1, 2, nr, 128), lambda i, h: (h, 0, 0, 0)),
                  pl.BlockSpec((1, 2, 8, 128), lambda i, h: (h, 0, 0, 0)),
                  pl.BlockSpec((1, l, wb), lambda i, h: (i, 0, h)),
                  pl.BlockSpec((1, l, wb), lambda i, h: (i, 0, nhb + h)),
                  pl.BlockSpec((1, l, wb), lambda i, h: (i, 0, 2 * nhb + h)),
                  pl.BlockSpec((1, 1, l, 128), lambda i, h: (i, h, 0, 0)),
                  pl.BlockSpec((1, 1, nc, nr, CHUNK), lambda i, h: (i, h, 0, 0, 0))],
        out_specs=pl.BlockSpec((1, l, wb), lambda i, h: (i, 0, h)),
        out_shape=jax.ShapeDtypeStruct((b, l, GDN_HEADS * GDN_DV), F32),
        scratch_shapes=[pltpu.VMEM((2 * GDN_HB, GDN_DK, GDN_DV), F32)],
        compiler_params=_params(("parallel", "parallel")),
        name="gdn_scan",
    )(rate_rows, rate_lanes, qkvn, qkvn, qkvn, abc, abr)


def _gdn_gate_layouts(p_mla, lc):
    b, l, _ = p_mla.shape
    nhb = GDN_HEADS // GDN_HB
    na = 2 * GDN_HB
    o = MLA_Q_LORA + MLA_KV_LORA
    ab = p_mla[:, :, o + SM_A:o + SM_A + 32].reshape(b, l, 2, 2, nhb, GDN_HB)
    ab = ab.transpose(0, 4, 1, 2, 3, 5).reshape(b, nhb, l, 2 * na)
    abr = ab[..., :na].reshape(b, nhb, l // CHUNK, CHUNK, na).transpose(0, 1, 2, 4, 3)
    abr = jnp.pad(abr, ((0, 0), (0, 0), (0, 0), (0, _GDN_GATE_ROWS - na), (0, 0)))
    abc = jnp.pad(ab, ((0, 0), (0, 0), (0, 0), (0, 128 - 2 * na)))
    return abc, abr


def _rope_swap(x):
    lane = lax.broadcasted_iota(jnp.int32, x.shape, 1)
    n = x.shape[1]
    return jnp.where(lane % 32 < 16, pltpu.roll(x, n - 16, axis=1), pltpu.roll(x, 16, axis=1))


def _mla_prep_kernel(p_ref, cos_ref, sin_ref, qnw_ref, qup_ref, kvnw_ref, kvup_ref,
                     wqn_ref, wqr_ref, wkn_ref, wkr_ref, q_ref, k_ref, v_ref):
    p = p_ref[0]
    cos_t = cos_ref[...]
    sin_t = sin_ref[...]

    def rms(x, n):
        return x * lax.rsqrt(jnp.sum(x * x, axis=-1, keepdims=True) * (1.0 / n) + EPS)

    def rope(x):
        return x * cos_t + _rope_swap(x) * sin_t

    qd = p[:, :MLA_Q_LORA]
    kvd = p[:, MLA_Q_LORA:MLA_Q_LORA + MLA_KV_LORA]
    sm = p[:, MLA_Q_LORA + MLA_KV_LORA:]
    q = _mm(rms(qd, MLA_Q_LORA) * qnw_ref[...], qup_ref[...])
    kv = _mm(rms(kvd, MLA_KV_LORA) * kvnw_ref[...], kvup_ref[...])
    lane = lax.broadcasted_iota(jnp.int32, sm.shape, 1)
    kr = jnp.where(lane < MLA_ROPE, sm, 0.0)
    kr = rope(rms(kr, MLA_ROPE) * wkr_ref[...])
    for h in range(MLA_HEADS):
        base = h * 256
        qn = rms(q[:, base:base + 128], MLA_NOPE) * wqn_ref[...]
        q_ref[0, :, base:base + 128] = (qn * MLA_Q_PRESCALE).astype(q_ref.dtype)
        qr = rms(q[:, base + 128:base + 256], MLA_ROPE) * wqr_ref[...]
        q_ref[0, :, base + 128:base + 256] = (rope(qr) * MLA_Q_PRESCALE).astype(q_ref.dtype)
        kn = kv[:, h * 128:(h + 1) * 128]
        k_ref[0, :, base:base + 128] = (rms(kn, MLA_NOPE) * wkn_ref[...]).astype(k_ref.dtype)
        k_ref[0, :, base + 128:base + 256] = kr.astype(k_ref.dtype)
    v_ref[0] = kv[:, MLA_HEADS * MLA_NOPE:].T.astype(v_ref.dtype)


def _mla_prep_call(p_mla, cos_t, sin_t, qnw, qup, kvnw, kvup, wqn, wqr, wkn, wkr):
    b, l, gw = p_mla.shape
    tm = _pick(l, (768, 384, 256, 128))
    hq = MLA_HEADS * 256
    full = lambda shape: pl.BlockSpec(shape, lambda i, t: tuple(0 for _ in shape))
    return pl.pallas_call(
        _mla_prep_kernel,
        grid=(b, l // tm),
        in_specs=[pl.BlockSpec((1, tm, gw), lambda i, t: (i, t, 0)),
                  pl.BlockSpec((tm, 128), lambda i, t: (t, 0)),
                  pl.BlockSpec((tm, 128), lambda i, t: (t, 0)),
                  full((1, MLA_Q_LORA)), full((MLA_Q_LORA, hq)),
                  full((1, MLA_KV_LORA)), full((MLA_KV_LORA, hq)),
                  full((1, 128)), full((1, 128)), full((1, 128)), full((1, 128))],
        out_specs=[pl.BlockSpec((1, tm, hq), lambda i, t: (i, t, 0)),
                   pl.BlockSpec((1, tm, hq), lambda i, t: (i, t, 0)),
                   pl.BlockSpec((1, MLA_HEADS * MLA_DV, tm), lambda i, t: (i, 0, t))],
        out_shape=[jax.ShapeDtypeStruct((b, l, hq), BF16),
                   jax.ShapeDtypeStruct((b, l, hq), BF16),
                   jax.ShapeDtypeStruct((b, MLA_HEADS * MLA_DV, l), BF16)],
        compiler_params=_params(("parallel", "parallel")),
        name="mla_prep",
    )(p_mla, cos_t, sin_t, qnw, qup, kvnw, kvup, wqn, wqr, wkn, wkr)


def _mla_attn_kernel(q_ref, k_ref, vt_ref, o_ref, *, lc, tq, q_off):
    qi = pl.program_id(2) + q_off
    heads = range(MLA_HP)

    def attend(nk):
        st = [lax.dot_general(k_ref[0, :nk, h * 256:(h + 1) * 256], q_ref[0, :, h * 256:(h + 1) * 256],
                              (((1,), (1,)), ((), ())), preferred_element_type=F32) for h in heads]
        for h in heads:
            e = jnp.exp2(st[h] - jnp.max(st[h], axis=0, keepdims=True))
            den = jnp.sum(e, axis=0, keepdims=True)
            ot = jnp.dot(vt_ref[0, h * MLA_DV:(h + 1) * MLA_DV, :nk], e.astype(BF16),
                         preferred_element_type=F32)
            o_ref[0, :, h * MLA_DV:(h + 1) * MLA_DV] = (ot / den).T.astype(o_ref.dtype)

    if q_off == 0:
        @pl.when(qi * tq < lc)
        def _():
            attend(lc)

        @pl.when(qi * tq >= lc)
        def _():
            attend(k_ref.shape[1])
    else:
        attend(k_ref.shape[1])


def _mla_attn_call(q, k, vt, lc, with_ctx):
    b, l, _ = q.shape
    tq = _pick(lc, (256, 128))
    q_off = 0 if with_ctx else lc // tq
    nq = l // tq - q_off
    return pl.pallas_call(
        functools.partial(_mla_attn_kernel, lc=lc, tq=tq, q_off=q_off),
        grid=(b, MLA_HEADS // MLA_HP, nq),
        in_specs=[pl.BlockSpec((1, tq, MLA_HP * 256), lambda i, h, t: (i, t + q_off, h)),
                  pl.BlockSpec((1, l, MLA_HP * 256), lambda i, h, t: (i, 0, h)),
                  pl.BlockSpec((1, MLA_HP * MLA_DV, l), lambda i, h, t: (i, h, 0))],
        out_specs=pl.BlockSpec((1, tq, MLA_HP * MLA_DV), lambda i, h, t: (i, t, h)),
        out_shape=jax.ShapeDtypeStruct((b, nq * tq, MLA_HEADS * MLA_DV), F32),
        compiler_params=_params(("parallel", "parallel", "arbitrary")),
        name="mla_attn",
    )(q, k, vt)


def _merge_kernel(h_ref, oa_ref, ob_ref, oc_ref, za_ref, zb_ref, zc_ref, g0_ref, g1_ref, g2_ref,
                  modx_ref, modc_ref, wa_ref, wb_ref, wbr_ref, wout_ref, o_ref, *, lc, tm, t_off):
    t = pl.program_id(1) + t_off

    def head_norm(o, width):
        parts = []
        for h in range(BRANCH_W // width):
            x = o[:, h * width:(h + 1) * width]
            parts.append(x * lax.rsqrt(jnp.mean(x * x, axis=-1, keepdims=True) + EPS))
        return jnp.concatenate(parts, axis=-1)

    ya = head_norm(oa_ref[0], GLA_DV) * wa_ref[...] * _silu(za_ref[0].astype(F32))
    yb = head_norm(ob_ref[0], GDN_DV) * wb_ref[...] * _silu(zb_ref[0].astype(F32))
    yc = oc_ref[0] * _silu(zc_ref[0].astype(F32))
    acc = _sigmoid(g0_ref[0].astype(F32)) * _mm(ya, wbr_ref[0])
    acc = acc + _sigmoid(g1_ref[0].astype(F32)) * _mm(yb, wbr_ref[1])
    acc = acc + _sigmoid(g2_ref[0].astype(F32)) * _mm(yc, wbr_ref[2])
    d = D_MODEL
    gate = _row_select(t, tm, lc, modc_ref[:, 2 * d:], modx_ref[0][:, 2 * d:])
    o_ref[0] = h_ref[0] + gate * _mm(acc, wout_ref[...])


def _merge_call(h, oa, ob, oc, p_zg, modx, modc, wa, wb, w_branch, w_out, lc, with_ctx):
    b, l, d = h.shape
    tm = _pick(lc, (256, 128))
    t_off = 0 if with_ctx else lc // tm
    nt = l // tm - t_off
    oc_off = 0 if oc.shape[1] == l else lc // tm
    row = lambda i, t: (i, t + t_off, 0)
    zg = lambda n: pl.BlockSpec((1, tm, d), lambda i, t: (i, t + t_off, n))
    return pl.pallas_call(
        functools.partial(_merge_kernel, lc=lc, tm=tm, t_off=t_off),
        grid=(b, nt),
        in_specs=[pl.BlockSpec((1, tm, d), row), pl.BlockSpec((1, tm, d), row), pl.BlockSpec((1, tm, d), row),
                  pl.BlockSpec((1, tm, d), lambda i, t: (i, t + t_off - oc_off, 0)),
                  zg(0), zg(1), zg(2), zg(3), zg(4), zg(5),
                  pl.BlockSpec((1, 1, 3 * d), lambda i, t: (i, 0, 0)),
                  pl.BlockSpec((1, 3 * d), lambda i, t: (0, 0)),
                  pl.BlockSpec((1, d), lambda i, t: (0, 0)),
                  pl.BlockSpec((1, d), lambda i, t: (0, 0)),
                  pl.BlockSpec((3, d, d), lambda i, t: (0, 0, 0)),
                  pl.BlockSpec((d, d), lambda i, t: (0, 0))],
        out_specs=pl.BlockSpec((1, tm, d), lambda i, t: (i, t, 0)),
        out_shape=jax.ShapeDtypeStruct((b, nt * tm, d), F32),
        compiler_params=_params(("parallel", "parallel")),
        name="merge",
    )(h, oa, ob, oc, p_zg, p_zg, p_zg, p_zg, p_zg, p_zg, modx, modc, wa, wb, w_branch, w_out)


def _cols(w, *names):
    return jnp.concatenate([w[:, _IN_OFF[n]:_IN_OFF[n] + _IN_LEN[n]] for n in names], axis=1)


def _rope_tables(t_lat, lc):
    rows = t_lat // GRID_W
    row = jnp.repeat(jnp.arange(rows, dtype=F32), GRID_W)
    col = jnp.tile(jnp.arange(GRID_W, dtype=F32), rows)
    inv_freq = jnp.power(ROPE_BASE, -jnp.arange(ROPE_FREQ, dtype=F32) / ROPE_FREQ)
    ar = row[:, None] * inv_freq
    ac = col[:, None] * inv_freq
    zeros = jnp.zeros((t_lat, 128 - MLA_ROPE), F32)
    cos_t = jnp.concatenate([jnp.cos(ar), jnp.cos(ar), jnp.cos(ac), jnp.cos(ac), zeros], axis=1)
    sin_t = jnp.concatenate([-jnp.sin(ar), jnp.sin(ar), -jnp.sin(ac), jnp.sin(ac), zeros], axis=1)
    cos_c = jnp.concatenate([jnp.ones((lc, MLA_ROPE), F32), jnp.zeros((lc, 128 - MLA_ROPE), F32)], axis=1)
    return (jnp.concatenate([cos_c, cos_t], axis=0),
            jnp.concatenate([jnp.zeros((lc, 128), F32), sin_t], axis=0))


def _pad_lanes(w, n=128):
    return jnp.pad(w, (0, n - w.shape[0])).reshape(1, n)


def kernel(x, c, ctx, c_ctx, norm_w, ada_w, ada_b, w_in, gla_gate_w2, gla_gate_b, gla_norm_w, gdn_conv_w,
           gdn_a_log, gdn_dt_bias, gdn_norm_w, mla_q_norm_w, mla_q_up, mla_kv_norm_w, mla_kv_up,
           mla_qn_nope, mla_qn_rope, mla_kn_nope, mla_kn_rope, w_branch, w_out):
    b, t_lat, d = x.shape
    lc = ctx.shape[1]
    l = lc + t_lat
    depth = w_in.shape[0]
    assert d == D_MODEL and lc % 128 == 0 and t_lat % 128 == 0

    h = jnp.concatenate([ctx, x], axis=1)
    rows = ((b + 1 + 7) // 8) * 8
    c_all = jnp.concatenate([c, c_ctx[None, :], jnp.zeros((rows - b - 1, d), F32)], axis=0)
    mod = _ada_call(c_all, ada_w, ada_b)
    cos_t, sin_t = _rope_tables(t_lat, lc)

    for li in range(depth):
        last = li == depth - 1
        modx = mod[li, :b].reshape(b, 1, 3 * d)
        modc = mod[li, b:b + 1]
        w = w_in[li]
        w_mla = _cols(w, 'mla_q_down', 'mla_kv_down', 'mla_k_rope', 'gla_gate_lr', 'gdn_a', 'gdn_b').astype(BF16)
        w_gla = _cols(w, 'gla_q', 'gla_k', 'gla_v').astype(BF16)
        w_gdn = _cols(w, 'gdn_qkv').astype(BF16)
        w_zg = _cols(w, 'gla_z', 'gdn_z', 'mla_z', 'merge_gate').astype(BF16)

        hn = _prenorm_call(h, modx, modc, norm_w[li], lc).reshape(b * l, d)
        p_mla = _proj_call(hn, w_mla, F32, "proj_mla").reshape(b, l, -1)
        p_gla = _proj_call(hn, w_gla, BF16, "proj_gla").reshape(b, l, -1)
        p_gdn = _proj_call(hn, w_gdn, F32, "proj_gdn").reshape(b, l, -1)
        p_zg = _proj_call(hn, w_zg, BF16, "proj_zg").reshape(b, l, -1)

        w2e = jnp.stack([jnp.pad(gla_gate_w2[li, dd], ((SM_LR + dd * GLA_RANK, 128 - SM_LR - (dd + 1) * GLA_RANK),
                                                      (0, 0))) for dd in range(2)])
        oa = _gla_call(p_gla, p_mla, w2e, gla_gate_b[li].reshape(2, 1, -1), lc)

        qkvn = _gdn_prep_call(p_gdn, gdn_conv_w[li], lc)
        abc, abr = _gdn_gate_layouts(p_mla, lc)
        ob = _gdn_call(qkvn, abc, abr, gdn_a_log[li], gdn_dt_bias[li], lc)

        qup = mla_q_up[li].reshape(MLA_Q_LORA, MLA_HEADS, MLA_NOPE + MLA_ROPE)
        qup = jnp.pad(qup, ((0, 0), (0, 0), (0, 256 - MLA_NOPE - MLA_ROPE))).reshape(MLA_Q_LORA, -1).astype(BF16)
        kvup = mla_kv_up[li].reshape(MLA_KV_LORA, MLA_HEADS, MLA_NOPE + MLA_DV)
        kvup = jnp.concatenate([kvup[:, :, :MLA_NOPE].reshape(MLA_KV_LORA, -1),
                                kvup[:, :, MLA_NOPE:].reshape(MLA_KV_LORA, -1)], axis=1).astype(BF16)
        qm, km, vm = _mla_prep_call(
            p_mla, cos_t, sin_t, mla_q_norm_w[li].reshape(1, -1), qup, mla_kv_norm_w[li].reshape(1, -1), kvup,
            mla_qn_nope[li].reshape(1, -1), _pad_lanes(mla_qn_rope[li]),
            mla_kn_nope[li].reshape(1, -1), _pad_lanes(mla_kn_rope[li]))
        oc = _mla_attn_call(qm, km, vm, lc, with_ctx=not last)

        h = _merge_call(h, oa, ob, oc, p_zg, modx, modc,
                        jnp.tile(gla_norm_w[li], GLA_HEADS).reshape(1, -1),
                        jnp.tile(gdn_norm_w[li], GDN_HEADS).reshape(1, -1),
                        w_branch[li].astype(BF16), w_out[li].astype(BF16), lc, with_ctx=not last)
    return h
```
